```python
import math
import jax, jax.numpy as jnp
from jax import lax
import numpy as np

D_MODEL = 1024
BATCH = 16
SEQ = 2048
DEPTH = 2

N_META = 16
N_EVEN = (DEPTH + 1) // 2
N_ODD = DEPTH // 2

LRU_WIDTH = D_MODEL // 2
LRU_HEADS = 4
LRU_HEAD_DIM = LRU_WIDTH // LRU_HEADS
CONV_WIDTH = 4
LRU_C = 8.0

MLA_HEADS = 8
MLA_NOPE = 64
MLA_ROPE = 32
MLA_V = 64
MLA_Q_RANK = D_MODEL // 4
MLA_KV_RANK = D_MODEL // 8
ATTN_BLOCK = 128

EVEN_IN = 2 * LRU_WIDTH + MLA_Q_RANK + MLA_KV_RANK + MLA_ROPE
EVEN_MIX = LRU_WIDTH + MLA_HEADS * MLA_V

RET_HEADS = 4
RET_QK_DIM = D_MODEL // RET_HEADS
RET_V_DIM = 2 * RET_QK_DIM
RET_CHUNK = 128
RET_IN = 2 * RET_HEADS * RET_QK_DIM + 2 * RET_HEADS * RET_V_DIM
RET_MIX = RET_HEADS * RET_V_DIM

D_FF = 4 * D_MODEL
ROPE_BASE = 10000.0
DN_ALPHA = (2 * DEPTH) ** 0.25
DN_BETA = (8 * DEPTH) ** -0.25
EPS = 1e-5
NEG_INF = -1e30

kernel_name = 'hybrid_rglru_mla_retention_deepnorm'


def _layernorm(x, g, b):
    xf = x.astype(jnp.float32)
    mu = jnp.mean(xf, axis=-1, keepdims=True)
    xc = xf - mu
    var = jnp.mean(jnp.square(xc), axis=-1, keepdims=True)
    return (xc * lax.rsqrt(var + EPS) * g + b).astype(x.dtype)


def _rmsnorm(x, g):
    xf = x.astype(jnp.float32)
    y = xf * lax.rsqrt(jnp.mean(jnp.square(xf), axis=-1, keepdims=True) + EPS)
    return (y * g).astype(x.dtype)


def _rope(x, pos):
    half = x.shape[-1] // 2
    inv = ROPE_BASE ** (-jnp.arange(half, dtype=jnp.float32) / half)
    ang = pos.astype(jnp.float32)[:, None] * inv[None, :]
    cos = jnp.cos(ang)[None, :, None, :].astype(x.dtype)
    sin = jnp.sin(ang)[None, :, None, :].astype(x.dtype)
    x1, x2 = x[..., :half], x[..., half:]
    return jnp.concatenate([x1 * cos - x2 * sin, x1 * sin + x2 * cos], axis=-1)


def _lru_combine(c1, c2):
    a1, b1 = c1
    a2, b2 = c2
    return a1 * a2, a2 * b1 + b2


def _rglru_group(p_gate, p_rec, conv_w, conv_b, w_rg_a, b_rg_a, w_rg_x, b_rg_x, lru_lambda):
    B, T, _ = p_rec.shape
    xc = lax.conv_general_dilated(
        p_rec, conv_w[:, None, :], window_strides=(1,), padding=[(CONV_WIDTH - 1, 0)],
        dimension_numbers=('NWC', 'WIO', 'NWC'), feature_group_count=LRU_WIDTH) + conv_b
    xh = xc.reshape(B, T, LRU_HEADS, LRU_HEAD_DIM)
    r = jax.nn.sigmoid(jnp.einsum('bthi,hij->bthj', xh, w_rg_a).reshape(B, T, LRU_WIDTH) + b_rg_a)
    i = jax.nn.sigmoid(jnp.einsum('bthi,hij->bthj', xh, w_rg_x).reshape(B, T, LRU_WIDTH) + b_rg_x)
    log_a = (-LRU_C * r * jax.nn.softplus(-lru_lambda)).astype(jnp.float32)
    a = jnp.exp(log_a)
    mult = jnp.sqrt(-jnp.expm1(2.0 * log_a))
    b = mult * (i * xc).astype(jnp.float32)
    _, h = lax.associative_scan(_lru_combine, (a, b), axis=1)
    return h.astype(p_rec.dtype) * jax.nn.gelu(p_gate)


def _attend(qb, qpos, k, v, kpos):
    scale = qb.shape[-1] ** -0.5
    s = jnp.einsum('bqhd,bkhd->bhqk', qb, k).astype(jnp.float32) * scale
    mask = kpos[None, :] <= qpos[:, None]
    s = jnp.where(mask[None, None], s, NEG_INF)
    p = jax.nn.softmax(s, axis=-1).astype(v.dtype)
    return jnp.einsum('bhqk,bkhd->bqhd', p, v)


def _causal_attention(q, k, v, pos):
    B, T, H, d = q.shape
    dv = v.shape[-1]
    out_meta = _attend(q[:, :N_META], pos[:N_META], k[:, :N_META], v[:, :N_META], pos[:N_META])
    nb = (T - N_META) // ATTN_BLOCK
    qr = q[:, N_META:].reshape(B, nb, ATTN_BLOCK, H, d).swapaxes(0, 1)
    pr = pos[N_META:].reshape(nb, ATTN_BLOCK)
    out_r = lax.map(lambda a: _attend(a[0], a[1], k, v, pos), (qr, pr))
    out_r = out_r.swapaxes(0, 1).reshape(B, T - N_META, H, dv)
    return jnp.concatenate([out_meta, out_r], axis=1)


def _mla_group(p_q, p_kv, p_kpe, pos, q_norm_g, w_uq, kv_norm_g, w_ukv):
    B, T, _ = p_q.shape
    q = (_rmsnorm(p_q, q_norm_g) @ w_uq).reshape(B, T, MLA_HEADS, MLA_NOPE + MLA_ROPE)
    q_nope, q_pe = q[..., :MLA_NOPE], _rope(q[..., MLA_NOPE:], pos)
    kv = (_rmsnorm(p_kv, kv_norm_g) @ w_ukv).reshape(B, T, MLA_HEADS, MLA_NOPE + MLA_V)
    k_nope, v = kv[..., :MLA_NOPE], kv[..., MLA_NOPE:]
    k_pe = _rope(p_kpe[:, :, None, :], pos)
    q = jnp.concatenate([q_nope, q_pe], axis=-1)
    k = jnp.concatenate([k_nope, jnp.broadcast_to(k_pe, (B, T, MLA_HEADS, MLA_ROPE))], axis=-1)
    o = _causal_attention(q, k, v, pos)
    return o.reshape(B, T, MLA_HEADS * MLA_V)


def _even_mixer(x, pos, w_in, conv_w, conv_b, w_rg_a, b_rg_a, w_rg_x, b_rg_x, lru_lambda,
                q_norm_g, w_uq, kv_norm_g, w_ukv, w_out):
    p = x @ w_in
    cuts = [LRU_WIDTH, 2 * LRU_WIDTH, 2 * LRU_WIDTH + MLA_Q_RANK,
            2 * LRU_WIDTH + MLA_Q_RANK + MLA_KV_RANK]
    p_gate, p_rec, p_q, p_kv, p_kpe = jnp.split(p, cuts, axis=-1)
    y_rec = _rglru_group(p_gate, p_rec, conv_w, conv_b, w_rg_a, b_rg_a, w_rg_x, b_rg_x, lru_lambda)
    y_att = _mla_group(p_q, p_kv, p_kpe, pos, q_norm_g, w_uq, kv_norm_g, w_ukv)
    return jnp.concatenate([y_rec, y_att], axis=-1) @ w_out


def _retention_chunk(q, k, v, s_prev, log_gamma):
    dt = q.dtype
    c = q.shape[2]
    idx = jnp.arange(c, dtype=jnp.float32)
    diff = idx[:, None] - idx[None, :]
    decay = jnp.where(diff >= 0, jnp.exp(log_gamma[:, None, None] * jnp.maximum(diff, 0.0)), 0.0).astype(dt)
    q_decay = jnp.exp(log_gamma[:, None] * (idx + 1.0))[None, :, :, None].astype(dt)
    k_decay = jnp.exp(log_gamma[:, None] * (c - 1.0 - idx))[None, :, :, None].astype(dt)
    chunk_decay = jnp.exp(log_gamma * c)[None, :, None, None].astype(dt)
    scores = jnp.einsum('bhid,bhjd->bhij', q, k) * decay
    o = jnp.einsum('bhij,bhjv->bhiv', scores, v) + q_decay * jnp.einsum('bhid,bhdv->bhiv', q, s_prev)
    s_new = chunk_decay * s_prev + jnp.einsum('bhjd,bhjv->bhdv', k * k_decay, v)
    return o, s_new


def _odd_mixer(x, pos, w_in, w_out):
    B, T, _ = x.shape
    qk = RET_HEADS * RET_QK_DIM
    p = x @ w_in
    q, k, v, g = jnp.split(p, [qk, 2 * qk, 2 * qk + RET_MIX], axis=-1)
    q = _rope(q.reshape(B, T, RET_HEADS, RET_QK_DIM), pos)
    k = _rope(k.reshape(B, T, RET_HEADS, RET_QK_DIM), pos) * (RET_QK_DIM ** -0.5)
    v = v.reshape(B, T, RET_HEADS, RET_V_DIM)
    q, k, v = (t.transpose(0, 2, 1, 3) for t in (q, k, v))
    log_gamma = jnp.log(1.0 - 2.0 ** (-5.0 - jnp.arange(RET_HEADS, dtype=jnp.float32)))
    s0 = jnp.zeros((B, RET_HEADS, RET_QK_DIM, RET_V_DIM), dtype=q.dtype)
    o_meta, s = _retention_chunk(q[:, :, :N_META], k[:, :, :N_META], v[:, :, :N_META], s0, log_gamma)
    nc = (T - N_META) // RET_CHUNK

    def to_chunks(t):
        return t[:, :, N_META:].reshape(B, RET_HEADS, nc, RET_CHUNK, t.shape[-1]).transpose(2, 0, 1, 3, 4)

    def body(state, qkv):
        qc, kc, vc = qkv
        o, state = _retention_chunk(qc, kc, vc, state, log_gamma)
        return state, o

    _, o_r = lax.scan(body, s, (to_chunks(q), to_chunks(k), to_chunks(v)))
    o_r = o_r.transpose(1, 2, 0, 3, 4).reshape(B, RET_HEADS, T - N_META, RET_V_DIM)
    o = jnp.concatenate([o_meta, o_r], axis=2)
    of = o.astype(jnp.float32)
    o = (of * lax.rsqrt(jnp.mean(jnp.square(of), axis=-1, keepdims=True) + EPS)).astype(x.dtype)
    y = o.transpose(0, 2, 1, 3).reshape(B, T, RET_MIX)
    return (jax.nn.silu(g) * y) @ w_out


def setup_inputs(seed: int = 0) -> dict:
    key = jax.random.key(seed)
    ks = iter(jax.random.split(key, 40))
    f32 = jnp.float32

    def nrm(shape, scale):
        return jax.random.normal(next(ks), shape, f32) * scale

    u = jax.random.uniform(next(ks), (N_EVEN, LRU_WIDTH), f32, minval=0.9, maxval=0.999)
    a_base = u ** (1.0 / LRU_C)
    lru_lambda = jnp.log(a_base) - jnp.log1p(-a_base)
    return {
        'x': nrm((BATCH, SEQ, D_MODEL), 1.0),
        'meta_tokens': nrm((N_META, D_MODEL), 1.0),
        'ev_w_in': nrm((N_EVEN, D_MODEL, EVEN_IN), D_MODEL ** -0.5),
        'ev_conv_w': nrm((N_EVEN, CONV_WIDTH, LRU_WIDTH), CONV_WIDTH ** -0.5),
        'ev_conv_b': nrm((N_EVEN, LRU_WIDTH), 0.02),
        'ev_w_rg_a': nrm((N_EVEN, LRU_HEADS, LRU_HEAD_DIM, LRU_HEAD_DIM), LRU_HEAD_DIM ** -0.5),
        'ev_b_rg_a': nrm((N_EVEN, LRU_WIDTH), 0.02),
        'ev_w_rg_x': nrm((N_EVEN, LRU_HEADS, LRU_HEAD_DIM, LRU_HEAD_DIM), LRU_HEAD_DIM ** -0.5),
        'ev_b_rg_x': nrm((N_EVEN, LRU_WIDTH), 0.02),
        'ev_lru_lambda': lru_lambda,
        'ev_q_norm_g': 1.0 + nrm((N_EVEN, MLA_Q_RANK), 0.02),
        'ev_w_uq': nrm((N_EVEN, MLA_Q_RANK, MLA_HEADS * (MLA_NOPE + MLA_ROPE)), MLA_Q_RANK ** -0.5),
        'ev_kv_norm_g': 1.0 + nrm((N_EVEN, MLA_KV_RANK), 0.02),
        'ev_w_ukv': nrm((N_EVEN, MLA_KV_RANK, MLA_HEADS * (MLA_NOPE + MLA_V)), MLA_KV_RANK ** -0.5),
        'ev_w_out': nrm((N_EVEN, EVEN_MIX, D_MODEL), DN_BETA * EVEN_MIX ** -0.5),
        'od_w_in': nrm((N_ODD, D_MODEL, RET_IN), D_MODEL ** -0.5),
        'od_w_out': nrm((N_ODD, RET_MIX, D_MODEL), DN_BETA * RET_MIX ** -0.5),
        'ln_mix_g': 1.0 + nrm((DEPTH, D_MODEL), 0.02),
        'ln_mix_b': nrm((DEPTH, D_MODEL), 0.02),
        'mlp_w1': nrm((DEPTH, D_MODEL, D_FF), D_MODEL ** -0.5),
        'mlp_w2': nrm((DEPTH, D_FF, D_MODEL), DN_BETA * D_FF ** -0.5),
        'ln_mlp_g': 1.0 + nrm((DEPTH, D_MODEL), 0.02),
        'ln_mlp_b': nrm((DEPTH, D_MODEL), 0.02),
    }


def reference(x, meta_tokens, ev_w_in, ev_conv_w, ev_conv_b, ev_w_rg_a, ev_b_rg_a, ev_w_rg_x,
              ev_b_rg_x, ev_lru_lambda, ev_q_norm_g, ev_w_uq, ev_kv_norm_g, ev_w_ukv, ev_w_out,
              od_w_in, od_w_out, ln_mix_g, ln_mix_b, mlp_w1, mlp_w2, ln_mlp_g, ln_mlp_b):
    B = x.shape[0]
    meta = jnp.broadcast_to(meta_tokens[None].astype(x.dtype), (B, N_META, D_MODEL))
    h = jnp.concatenate([meta, x], axis=1)
    pos = jnp.arange(h.shape[1], dtype=jnp.int32)
    for l in range(DEPTH):
        if l % 2 == 0:
            e = l // 2
            mix = _even_mixer(h, pos, ev_w_in[e], ev_conv_w[e], ev_conv_b[e], ev_w_rg_a[e], ev_b_rg_a[e],
                              ev_w_rg_x[e], ev_b_rg_x[e], ev_lru_lambda[e], ev_q_norm_g[e], ev_w_uq[e],
                              ev_kv_norm_g[e], ev_w_ukv[e], ev_w_out[e])
        else:
            o = l // 2
            mix = _odd_mixer(h, pos, od_w_in[o], od_w_out[o])
        h = _layernorm(DN_ALPHA * h + mix, ln_mix_g[l], ln_mix_b[l])
        f = jnp.square(jax.nn.relu(h @ mlp_w1[l])) @ mlp_w2[l]
        h = _layernorm(DN_ALPHA * h + f, ln_mlp_g[l], ln_mlp_b[l])
    return h[:, N_META:]
```

```python
import functools
import math

import jax
import jax.numpy as jnp
import numpy as np
from jax import lax
from jax.experimental import pallas as pl
from jax.experimental.pallas import tpu as pltpu

D_MODEL = 1024
N_META = 16
LRU_WIDTH = 512
LRU_HEADS = 4
LRU_HEAD_DIM = 128
CONV_WIDTH = 4
LRU_C = 8.0
MLA_HEADS = 8
MLA_NOPE = 64
MLA_ROPE = 32
MLA_V = 64
MLA_Q_RANK = 256
MLA_KV_RANK = 128
RET_HEADS = 4
RET_QK_DIM = 256
RET_V_DIM = 512
D_FF = 4096
ROPE_BASE = 10000.0
DN_ALPHA = 4.0 ** 0.25
EPS = 1e-5
NEG_INF = -1e30

LANE = 128
SUBLANE = 8
VMEM_LIMIT = 56 * 1024 * 1024

BF16 = jnp.bfloat16
F32 = jnp.float32

SEQ_TILE = 512
ATT_TILE = 512
ROW_TILE = 512
FF_TILE = 1024
RET_CHUNK = 128


def _dot(a, b):
    return jnp.dot(a, b, preferred_element_type=F32)


def _dot_nt(a, b):
    return lax.dot_general(a, b, (((1,), (1,)), ((), ())), preferred_element_type=F32)


def _dot_tn(a, b):
    return lax.dot_general(a, b, (((0,), (0,)), ((), ())), preferred_element_type=F32)


def _layernorm(x, g, b):
    mu = jnp.mean(x, axis=-1, keepdims=True)
    xc = x - mu
    var = jnp.mean(xc * xc, axis=-1, keepdims=True)
    return xc * lax.rsqrt(var + EPS) * g + b


def _rmsnorm(x, g):
    return x * lax.rsqrt(jnp.mean(x * x, axis=-1, keepdims=True) + EPS) * g


def _rope_mla(x, c, s_up, s_dn):
    return x * c + pltpu.roll(x, MLA_ROPE // 2, 1) * s_up + pltpu.roll(x, LANE - MLA_ROPE // 2, 1) * s_dn


def _lru_gates(xc, gate_w_ref, b_a, b_x, sp_lambda):
    rs, is_ = [], []
    for h in range(LRU_HEADS):
        g = _dot(xc[:, h * LRU_HEAD_DIM:(h + 1) * LRU_HEAD_DIM].astype(BF16), gate_w_ref[h])
        rs.append(g[:, :LRU_HEAD_DIM])
        is_.append(g[:, LRU_HEAD_DIM:])
    r = jax.nn.sigmoid(jnp.concatenate(rs, axis=1) + b_a)
    i = jax.nn.sigmoid(jnp.concatenate(is_, axis=1) + b_x)
    log_a = -LRU_C * r * sp_lambda
    a = jnp.exp(log_a)
    mult = jnp.sqrt(1.0 - a * a)
    return a, mult * (i * xc)


def _scan8(a, b):
    row = lax.broadcasted_iota(jnp.int32, a.shape, 0)
    for k in (1, 2, 4):
        keep = row >= k
        a_prev = jnp.where(keep, pltpu.roll(a, k, 0), 1.0)
        b_prev = jnp.where(keep, pltpu.roll(b, k, 0), 0.0)
        b = a * b_prev + b
        a = a * a_prev
    return a, b


def _lru_scan(a_ref, b_ref, h0, rows):
    def body(g, h_prev):
        sl = pl.ds(pl.multiple_of(g * SUBLANE, SUBLANE), SUBLANE)
        a_c, b_c = _scan8(a_ref[sl, :], b_ref[sl, :])
        h = a_c * h_prev + b_c
        b_ref[sl, :] = h
        return h[SUBLANE - 1:SUBLANE, :]
    return lax.fori_loop(0, rows // SUBLANE, body, h0, unroll=4)


def _mla_project(qlat, kvlat, kpe, q_g, kv_g, w_uq_ref, w_ukv_ref, rope_c, rope_up, rope_dn):
    scale = (MLA_NOPE + MLA_ROPE) ** -0.5
    q_all = _dot(_rmsnorm(qlat, q_g).astype(BF16), w_uq_ref[...])
    kv_all = _dot(_rmsnorm(kvlat, kv_g).astype(BF16), w_ukv_ref[...])
    kpe_r = _rope_mla(kpe, rope_c, rope_up, rope_dn)
    qs, ks = [], []
    for h in range(MLA_HEADS):
        sl = slice(h * LANE, (h + 1) * LANE)
        qs.append((_rope_mla(q_all[:, sl], rope_c, rope_up, rope_dn) * scale).astype(BF16))
        ks.append((kv_all[:, sl] + kpe_r).astype(BF16))
    v = kv_all[:, MLA_HEADS * LANE:].astype(BF16)
    return jnp.concatenate(qs, axis=1), jnp.concatenate(ks, axis=1), v


def _mlp_block(h_in, mix, ln1_g, ln1_b, w1_ref, w2_ref, ln2_g, ln2_b):
    h1 = _layernorm(DN_ALPHA * h_in + mix, ln1_g, ln1_b)
    h1b = h1.astype(BF16)
    f = None
    for c in range(D_FF // FF_TILE):
        a = _dot(h1b, w1_ref[:, c * FF_TILE:(c + 1) * FF_TILE])
        a = jnp.maximum(a, 0.0)
        part = _dot((a * a).astype(BF16), w2_ref[c * FF_TILE:(c + 1) * FF_TILE, :])
        f = part if f is None else f + part
    return _layernorm(DN_ALPHA * h1 + f, ln2_g, ln2_b)


def _meta_kernel(meta_ref, w_in_ref, conv_w_ref, conv_b_ref, gate_w_ref, b_a_ref, b_x_ref, lam_ref,
                 q_g_ref, kv_g_ref, w_uq_ref, w_ukv_ref, rope_c_ref, rope_up_ref, rope_dn_ref,
                 w_out_ref, ln1_g_ref, ln1_b_ref, w1_ref, w2_ref, ln2_g_ref, ln2_b_ref,
                 w_k_ref, w_v_ref, cos1_ref, sin1_ref, lg_ref,
                 rec_tail_ref, h_tail_ref, k_meta_ref, v_meta_ref, s_meta_ref,
                 conv_scr, a_scr, b_scr):
    n = N_META
    x = meta_ref[...]
    p = _dot(x.astype(BF16), w_in_ref[...])
    gate, rec = p[:, :LRU_WIDTH], p[:, LRU_WIDTH:2 * LRU_WIDTH]
    conv_scr[0:SUBLANE, :] = jnp.zeros((SUBLANE, LRU_WIDTH), F32)
    conv_scr[SUBLANE:SUBLANE + n, :] = rec
    cw = conv_w_ref[...]
    xc = conv_b_ref[...] + cw[3:4, :] * rec
    for j in range(CONV_WIDTH - 1):
        off = SUBLANE - (CONV_WIDTH - 1) + j
        xc = xc + cw[j:j + 1, :] * conv_scr[off:off + n, :]
    sp_lambda = jax.nn.softplus(-lam_ref[...])
    a, b = _lru_gates(xc, gate_w_ref, b_a_ref[...], b_x_ref[...], sp_lambda)
    a_scr[...] = a
    b_scr[...] = b
    _lru_scan(a_scr, b_scr, jnp.zeros((1, LRU_WIDTH), F32), n)
    h = b_scr[...]
    y_rec = (h * jax.nn.gelu(gate)).astype(BF16)
    rec_tail_ref[...] = rec[n - SUBLANE:, :]
    h_tail_ref[...] = h[n - SUBLANE:, :]

    off = 2 * LRU_WIDTH
    q, k, v = _mla_project(p[:, off:off + MLA_Q_RANK],
                           p[:, off + MLA_Q_RANK:off + MLA_Q_RANK + MLA_KV_RANK],
                           p[:, off + MLA_Q_RANK + MLA_KV_RANK:],
                           q_g_ref[...], kv_g_ref[...], w_uq_ref, w_ukv_ref,
                           rope_c_ref[...], rope_up_ref[...], rope_dn_ref[...])
    k_meta_ref[...] = jnp.zeros(k_meta_ref.shape, BF16)
    v_meta_ref[...] = jnp.zeros(v_meta_ref.shape, BF16)
    k_meta_ref[0:n, :] = k
    v_meta_ref[0:n, :] = v
    causal = (lax.broadcasted_iota(jnp.int32, (n, n), 1) <= lax.broadcasted_iota(jnp.int32, (n, n), 0))
    outs = []
    for hh in range(MLA_HEADS):
        sl = slice(hh * LANE, (hh + 1) * LANE)
        s = jnp.where(causal, _dot_nt(q[:, sl], k[:, sl]), NEG_INF)
        e = jnp.exp(s - jnp.max(s, axis=-1, keepdims=True))
        pr = e / jnp.sum(e, axis=-1, keepdims=True)
        outs.append(_dot(pr.astype(BF16), v[:, hh * MLA_V:(hh + 1) * MLA_V]))
    y_att = jnp.concatenate(outs, axis=1).astype(BF16)
    mix = _dot(y_rec, w_out_ref[0:LRU_WIDTH, :]) + _dot(y_att, w_out_ref[LRU_WIDTH:, :])
    h2 = _mlp_block(x, mix, ln1_g_ref[...], ln1_b_ref[...], w1_ref, w2_ref, ln2_g_ref[...], ln2_b_ref[...])

    h2b = h2.astype(BF16)
    kk = _dot(h2b, w_k_ref[...])
    vv = _dot(h2b, w_v_ref[...]).astype(BF16)
    cos, sin = cos1_ref[...], sin1_ref[...]
    idx = lax.broadcasted_iota(jnp.int32, (n, 1), 0).astype(F32)
    half = RET_QK_DIM // 2
    for hh in range(RET_HEADS):
        log_gamma = lg_ref[hh][:, 0:1]
        k1 = kk[:, hh * RET_QK_DIM:hh * RET_QK_DIM + half]
        k2 = kk[:, hh * RET_QK_DIM + half:(hh + 1) * RET_QK_DIM]
        kr = jnp.concatenate([k1 * cos - k2 * sin, k1 * sin + k2 * cos], axis=1).astype(BF16)
        k_dec = jnp.exp(log_gamma * (n - 1.0 - idx))
        kd = (kr.astype(F32) * k_dec).astype(BF16)
        s_meta_ref[hh] = _dot_tn(kd, vv[:, hh * RET_V_DIM:(hh + 1) * RET_V_DIM])


def _seq0_kernel(x_ref, w_in_ref, conv_w_ref, conv_b_ref, gate_w_ref, b_a_ref, b_x_ref, lam_ref,
                 q_g_ref, kv_g_ref, w_uq_ref, w_ukv_ref, rope_c_ref, rope_up_ref, rope_dn_ref,
                 rec_tail_ref, h_tail_ref,
                 y_rec_ref, q_ref, k_ref, v_ref,
                 conv_scr, a_scr, b_scr, h_scr):
    ts = SEQ_TILE
    t = pl.program_id(1)

    @pl.when(t == 0)
    def _():
        conv_scr[0:SUBLANE, :] = rec_tail_ref[...]
        h_scr[...] = h_tail_ref[...]

    p = _dot(x_ref[0].astype(BF16), w_in_ref[...])
    gate, rec = p[:, :LRU_WIDTH], p[:, LRU_WIDTH:2 * LRU_WIDTH]
    conv_scr[SUBLANE:SUBLANE + ts, :] = rec
    cw = conv_w_ref[...]
    xc = conv_b_ref[...] + cw[3:4, :] * rec
    for j in range(CONV_WIDTH - 1):
        off = SUBLANE - (CONV_WIDTH - 1) + j
        xc = xc + cw[j:j + 1, :] * conv_scr[off:off + ts, :]
    conv_scr[0:SUBLANE, :] = rec[ts - SUBLANE:, :]
    sp_lambda = jax.nn.softplus(-lam_ref[...])
    a, b = _lru_gates(xc, gate_w_ref, b_a_ref[...], b_x_ref[...], sp_lambda)
    a_scr[...] = a
    b_scr[...] = b
    h_last = _lru_scan(a_scr, b_scr, h_scr[SUBLANE - 1:SUBLANE, :], ts)
    h_scr[SUBLANE - 1:SUBLANE, :] = h_last
    y_rec_ref[0] = (b_scr[...] * jax.nn.gelu(gate)).astype(BF16)

    off = 2 * LRU_WIDTH
    q, k, v = _mla_project(p[:, off:off + MLA_Q_RANK],
                           p[:, off + MLA_Q_RANK:off + MLA_Q_RANK + MLA_KV_RANK],
                           p[:, off + MLA_Q_RANK + MLA_KV_RANK:],
                           q_g_ref[...], kv_g_ref[...], w_uq_ref, w_ukv_ref,
                           rope_c_ref[...], rope_up_ref[...], rope_dn_ref[...])
    q_ref[0] = q
    k_ref[0] = k
    v_ref[0] = v


def _attn_kernel(q_ref, k_ref, v_ref, k_meta_ref, v_meta_ref, o_ref, m_scr, l_scr, acc_scr):
    tq = ATT_TILE
    qi = pl.program_id(2)
    lane = lax.broadcasted_iota(jnp.int32, (tq, LANE), 1)
    meta_valid = lane < N_META
    v_meta = v_meta_ref[...]
    for hh in range(2):
        sl = slice(hh * LANE, (hh + 1) * LANE)
        q = q_ref[0, :, sl]
        s = jnp.where(meta_valid, _dot_nt(q, k_meta_ref[:, sl]), NEG_INF)
        m = jnp.max(s, axis=-1, keepdims=True)
        e = jnp.exp(s - m)
        m_scr[hh] = m
        l_scr[hh] = jnp.sum(e, axis=-1, keepdims=True)
        acc_scr[hh] = _dot(e.astype(BF16), v_meta)

        def step(kj, masked):
            ks = pl.ds(pl.multiple_of(kj * tq, tq), tq)
            s = _dot_nt(q, k_ref[0, ks, sl])
            if masked:
                row = lax.broadcasted_iota(jnp.int32, (tq, tq), 0)
                col = lax.broadcasted_iota(jnp.int32, (tq, tq), 1)
                s = jnp.where(col <= row, s, NEG_INF)
            m_old = m_scr[hh]
            m_new = jnp.maximum(m_old, jnp.max(s, axis=-1, keepdims=True))
            alpha = jnp.exp(m_old - m_new)
            e = jnp.exp(s - m_new)
            m_scr[hh] = m_new
            l_scr[hh] = alpha * l_scr[hh] + jnp.sum(e, axis=-1, keepdims=True)
            acc_scr[hh] = alpha * acc_scr[hh] + _dot(e.astype(BF16), v_ref[0, ks, :])

        def body(kj, c):
            step(kj, False)
            return c

        lax.fori_loop(0, qi, body, 0)
        step(qi, True)
    out0 = acc_scr[0] / l_scr[0]
    out1 = acc_scr[1] / l_scr[1]
    o_ref[0] = jnp.where(lane < MLA_V, out0, out1).astype(BF16)


def _mix_mlp_kernel(*refs, n_mix):
    h_ref = refs[0]
    y_refs = refs[1:1 + n_mix]
    w_out_ref, ln1_g, ln1_b, w1_ref, w2_ref, ln2_g, ln2_b, o_ref = refs[1 + n_mix:]
    mix = None
    row = 0
    for y_ref in y_refs:
        width = y_ref.shape[-1]
        part = _dot(y_ref[...], w_out_ref[row:row + width, :])
        mix = part if mix is None else mix + part
        row += width
    o_ref[...] = _mlp_block(h_ref[...], mix, ln1_g[...], ln1_b[...], w1_ref, w2_ref, ln2_g[...], ln2_b[...])


def _ret_proj_kernel(h_ref, w_ref, cos_ref, sin_ref, q_ref, k_ref, v_ref, g_ref):
    hb = h_ref[...].astype(BF16)
    qk = RET_HEADS * RET_QK_DIM
    half = RET_QK_DIM // 2
    k_scale = RET_QK_DIM ** -0.5
    for out_ref, base, scale in ((q_ref, 0, None), (k_ref, qk, k_scale)):
        cos, sin = cos_ref[...], sin_ref[...]
        if scale is not None:
            cos, sin = cos * scale, sin * scale
        pr = _dot(hb, w_ref[:, base:base + qk])
        parts = []
        for hh in range(RET_HEADS):
            x1 = pr[:, hh * RET_QK_DIM:hh * RET_QK_DIM + half]
            x2 = pr[:, hh * RET_QK_DIM + half:(hh + 1) * RET_QK_DIM]
            parts.append((x1 * cos - x2 * sin).astype(BF16))
            parts.append((x1 * sin + x2 * cos).astype(BF16))
        out_ref[...] = jnp.concatenate(parts, axis=1)
    mixw = RET_HEADS * RET_V_DIM
    v_ref[...] = _dot(hb, w_ref[:, 2 * qk:2 * qk + mixw]).astype(BF16)
    g_ref[...] = _dot(hb, w_ref[:, 2 * qk + mixw:]).astype(BF16)


def _retention_kernel(q_ref, k_ref, v_ref, g_ref, s0_ref, lg_ref, y_ref, s_scr):
    c = RET_CHUNK
    log_gamma = lg_ref[0][:, 0:1]
    ii = lax.broadcasted_iota(jnp.int32, (c, c), 0)
    jj = lax.broadcasted_iota(jnp.int32, (c, c), 1)
    diff = (ii - jj).astype(F32)
    decay = jnp.where(diff >= 0, jnp.exp(log_gamma * jnp.maximum(diff, 0.0)), 0.0)
    idx = lax.broadcasted_iota(jnp.int32, (c, 1), 0).astype(F32)
    q_decay = jnp.exp(log_gamma * (idx + 1.0))
    k_decay = jnp.exp(log_gamma * (c - 1.0 - idx))
    chunk_decay = jnp.exp(log_gamma * c)
    s_scr[...] = s0_ref[0]

    def body(ci, _):
        rows = pl.ds(pl.multiple_of(ci * c, c), c)
        q = q_ref[0, rows, :]
        k = k_ref[0, rows, :]
        v = v_ref[0, rows, :]
        s_prev = s_scr[...]
        scores = _dot_nt(q, k) * decay
        o = _dot(scores.astype(BF16), v) + q_decay * _dot(q, s_prev.astype(BF16))
        kd = (k.astype(F32) * k_decay).astype(BF16)
        s_scr[...] = chunk_decay * s_prev + _dot_tn(kd, v)
        o = o * lax.rsqrt(jnp.mean(o * o, axis=-1, keepdims=True) + EPS)
        y_ref[0, rows, :] = (jax.nn.silu(g_ref[0, rows, :].astype(F32)) * o).astype(BF16)
        return 0

    lax.fori_loop(0, q_ref.shape[1] // c, body, 0)


def _const_spec(shape):
    zeros = (0,) * len(shape)
    return pl.BlockSpec(shape, lambda *_: zeros, pipeline_mode=pl.Buffered(1))


def _params(*semantics):
    return pltpu.CompilerParams(dimension_semantics=semantics, vmem_limit_bytes=VMEM_LIMIT)


def _rope_tables(positions, half):
    inv = ROPE_BASE ** (-jnp.arange(half, dtype=F32) / half)
    ang = positions.astype(F32)[:, None] * inv[None, :]
    return jnp.cos(ang), jnp.sin(ang)


def _mla_rope_tables(positions):
    cos, sin = _rope_tables(positions, MLA_ROPE // 2)
    n = positions.shape[0]
    h = MLA_ROPE // 2
    ones = jnp.ones((n, MLA_NOPE), F32)
    zeros = lambda w: jnp.zeros((n, w), F32)
    c = jnp.concatenate([ones, cos, cos, zeros(LANE - MLA_NOPE - MLA_ROPE)], axis=1)
    s_up = jnp.concatenate([zeros(MLA_NOPE + h), sin, zeros(LANE - MLA_NOPE - MLA_ROPE)], axis=1)
    s_dn = jnp.concatenate([zeros(MLA_NOPE), -sin, zeros(LANE - MLA_NOPE - h)], axis=1)
    return c, s_up, s_dn


def _mix_mlp_call(name, h, ys, w_out, ln1_g, ln1_b, w1, w2, ln2_g, ln2_b):
    m = h.shape[0]
    tm = ROW_TILE
    row_spec = lambda w: pl.BlockSpec((tm, w), lambda i: (i, 0))
    vec = _const_spec((1, D_MODEL))
    return pl.pallas_call(
        functools.partial(_mix_mlp_kernel, n_mix=len(ys)),
        grid=(m // tm,),
        in_specs=[row_spec(D_MODEL)] + [row_spec(y.shape[1]) for y in ys] + [
            _const_spec(w_out.shape), vec, vec, _const_spec(w1.shape), _const_spec(w2.shape), vec, vec],
        out_specs=row_spec(D_MODEL),
        out_shape=jax.ShapeDtypeStruct((m, D_MODEL), F32),
        compiler_params=_params("parallel"),
        name=name,
    )(h, *ys, w_out, ln1_g, ln1_b, w1, w2, ln2_g, ln2_b)


def kernel(x, meta_tokens, ev_w_in, ev_conv_w, ev_conv_b, ev_w_rg_a, ev_b_rg_a, ev_w_rg_x, ev_b_rg_x,
           ev_lru_lambda, ev_q_norm_g, ev_w_uq, ev_kv_norm_g, ev_w_ukv, ev_w_out, od_w_in, od_w_out,
           ln_mix_g, ln_mix_b, mlp_w1, mlp_w2, ln_mlp_g, ln_mlp_b):
    bsz, seq, _ = x.shape
    row = lambda v: v.reshape(1, -1).astype(F32)

    w_in0 = ev_w_in[0]
    lat0 = 2 * LRU_WIDTH
    kpe0 = lat0 + MLA_Q_RANK + MLA_KV_RANK
    w_kpe = jnp.zeros((D_MODEL, LANE), F32).at[:, MLA_NOPE:MLA_NOPE + MLA_ROPE].set(w_in0[:, kpe0:])
    w_in_ev = jnp.concatenate([w_in0[:, :kpe0], w_kpe], axis=1).astype(BF16)
    gate_w = jnp.concatenate([ev_w_rg_a[0], ev_w_rg_x[0]], axis=2).astype(BF16)
    w_uq = ev_w_uq[0].reshape(MLA_Q_RANK, MLA_HEADS, MLA_NOPE + MLA_ROPE)
    w_uq = jnp.pad(w_uq, ((0, 0), (0, 0), (0, LANE - MLA_NOPE - MLA_ROPE)))
    w_uq = w_uq.reshape(MLA_Q_RANK, MLA_HEADS * LANE).astype(BF16)
    w_ukv = ev_w_ukv[0].reshape(MLA_KV_RANK, MLA_HEADS, MLA_NOPE + MLA_V)
    w_uk = jnp.pad(w_ukv[:, :, :MLA_NOPE], ((0, 0), (0, 0), (0, LANE - MLA_NOPE)))
    w_uk = w_uk.reshape(MLA_KV_RANK, MLA_HEADS * LANE)
    w_uv = w_ukv[:, :, MLA_NOPE:].reshape(MLA_KV_RANK, MLA_HEADS * MLA_V)
    w_ukv_p = jnp.concatenate([w_uk, w_uv], axis=1).astype(BF16)
    w_out_ev = ev_w_out[0].astype(BF16)
    w_in_od = od_w_in[0].astype(BF16)
    w_out_od = od_w_out[0].astype(BF16)
    w1 = mlp_w1.astype(BF16)
    w2 = mlp_w2.astype(BF16)
    conv_w = ev_conv_w[0].astype(F32)
    conv_b, b_a, b_x, lam = row(ev_conv_b[0]), row(ev_b_rg_a[0]), row(ev_b_rg_x[0]), row(ev_lru_lambda[0])
    q_g, kv_g = row(ev_q_norm_g[0]), row(ev_kv_norm_g[0])
    ln = [(row(ln_mix_g[l]), row(ln_mix_b[l]), row(ln_mlp_g[l]), row(ln_mlp_b[l])) for l in range(2)]

    pos = jnp.arange(N_META + seq, dtype=jnp.int32)
    mla_tabs = _mla_rope_tables(pos)
    cos1, sin1 = _rope_tables(pos, RET_QK_DIM // 2)
    k_scale = RET_QK_DIM ** -0.5
    qk = RET_HEADS * RET_QK_DIM
    log_gamma = jnp.log(1.0 - 2.0 ** (-5.0 - jnp.arange(RET_HEADS, dtype=F32)))
    log_gamma = jnp.broadcast_to(log_gamma[:, None, None], (RET_HEADS, 1, LANE))

    meta_out = pl.pallas_call(
        _meta_kernel,
        out_shape=(jax.ShapeDtypeStruct((SUBLANE, LRU_WIDTH), F32),
                   jax.ShapeDtypeStruct((SUBLANE, LRU_WIDTH), F32),
                   jax.ShapeDtypeStruct((LANE, MLA_HEADS * LANE), BF16),
                   jax.ShapeDtypeStruct((LANE, MLA_HEADS * MLA_V), BF16),
                   jax.ShapeDtypeStruct((RET_HEADS, RET_QK_DIM, RET_V_DIM), F32)),
        scratch_shapes=[pltpu.VMEM((SUBLANE + N_META, LRU_WIDTH), F32),
                        pltpu.VMEM((N_META, LRU_WIDTH), F32),
                        pltpu.VMEM((N_META, LRU_WIDTH), F32)],
        compiler_params=pltpu.CompilerParams(vmem_limit_bytes=VMEM_LIMIT),
        name="meta_tokens",
    )(meta_tokens.astype(F32), w_in_ev, conv_w, conv_b, gate_w, b_a, b_x, lam, q_g, kv_g, w_uq, w_ukv_p,
      *[t[:N_META] for t in mla_tabs], w_out_ev, *ln[0][:2], w1[0], w2[0], *ln[0][2:],
      w_in_od[:, qk:2 * qk], w_in_od[:, 2 * qk:2 * qk + RET_HEADS * RET_V_DIM],
      cos1[:N_META] * k_scale, sin1[:N_META] * k_scale, log_gamma)
    rec_tail, h_tail, k_meta, v_meta, s_meta = meta_out

    ts = SEQ_TILE
    nt = seq // ts
    tab_spec = pl.BlockSpec((ts, LANE), lambda b, t: (t, 0))
    seq_spec = lambda w: pl.BlockSpec((1, ts, w), lambda b, t: (b, t, 0))
    y_rec, q0, k0, v0 = pl.pallas_call(
        _seq0_kernel,
        grid=(bsz, nt),
        in_specs=[seq_spec(D_MODEL), _const_spec(w_in_ev.shape), _const_spec(conv_w.shape),
                  _const_spec(conv_b.shape), _const_spec(gate_w.shape), _const_spec(b_a.shape),
                  _const_spec(b_x.shape), _const_spec(lam.shape), _const_spec(q_g.shape),
                  _const_spec(kv_g.shape), _const_spec(w_uq.shape), _const_spec(w_ukv_p.shape),
                  tab_spec, tab_spec, tab_spec,
                  _const_spec(rec_tail.shape), _const_spec(h_tail.shape)],
        out_specs=[seq_spec(LRU_WIDTH), seq_spec(MLA_HEADS * LANE), seq_spec(MLA_HEADS * LANE),
                   seq_spec(MLA_HEADS * MLA_V)],
        out_shape=(jax.ShapeDtypeStruct((bsz, seq, LRU_WIDTH), BF16),
                   jax.ShapeDtypeStruct((bsz, seq, MLA_HEADS * LANE), BF16),
                   jax.ShapeDtypeStruct((bsz, seq, MLA_HEADS * LANE), BF16),
                   jax.ShapeDtypeStruct((bsz, seq, MLA_HEADS * MLA_V), BF16)),
        scratch_shapes=[pltpu.VMEM((SUBLANE + ts, LRU_WIDTH), F32),
                        pltpu.VMEM((ts, LRU_WIDTH), F32),
                        pltpu.VMEM((ts, LRU_WIDTH), F32),
                        pltpu.VMEM((SUBLANE, LRU_WIDTH), F32)],
        compiler_params=_params("parallel", "arbitrary"),
        name="seq0_mixer_proj",
    )(x, w_in_ev, conv_w, conv_b, gate_w, b_a, b_x, lam, q_g, kv_g, w_uq, w_ukv_p,
      *[t[N_META:] for t in mla_tabs], rec_tail, h_tail)

    tq = ATT_TILE
    y_att = pl.pallas_call(
        _attn_kernel,
        grid=(bsz, MLA_HEADS // 2, seq // tq),
        in_specs=[pl.BlockSpec((1, tq, 2 * LANE), lambda b, j, i: (b, i, j)),
                  pl.BlockSpec((1, seq, 2 * LANE), lambda b, j, i: (b, 0, j)),
                  pl.BlockSpec((1, seq, LANE), lambda b, j, i: (b, 0, j)),
                  pl.BlockSpec((LANE, 2 * LANE), lambda b, j, i: (0, j)),
                  pl.BlockSpec((LANE, LANE), lambda b, j, i: (0, j))],
        out_specs=pl.BlockSpec((1, tq, LANE), lambda b, j, i: (b, i, j)),
        out_shape=jax.ShapeDtypeStruct((bsz, seq, MLA_HEADS * MLA_V), BF16),
        scratch_shapes=[pltpu.VMEM((2, tq, 1), F32), pltpu.VMEM((2, tq, 1), F32),
                        pltpu.VMEM((2, tq, LANE), F32)],
        compiler_params=_params("parallel", "parallel", "arbitrary"),
        name="mla_attention",
    )(q0, k0, v0, k_meta, v_meta)

    m = bsz * seq
    h1 = _mix_mlp_call("layer0_out_mlp", x.reshape(m, D_MODEL),
                       [y_rec.reshape(m, -1), y_att.reshape(m, -1)], w_out_ev,
                       ln[0][0], ln[0][1], w1[0], w2[0], ln[0][2], ln[0][3])

    tm = ROW_TILE
    per_seq = seq // tm
    row_spec = lambda w: pl.BlockSpec((tm, w), lambda i: (i, 0))
    rope_spec = pl.BlockSpec((tm, RET_QK_DIM // 2), lambda i: (i % per_seq, 0))
    mixw = RET_HEADS * RET_V_DIM
    q1, k1, v1, g1 = pl.pallas_call(
        _ret_proj_kernel,
        grid=(m // tm,),
        in_specs=[row_spec(D_MODEL), _const_spec(w_in_od.shape), rope_spec, rope_spec],
        out_specs=[row_spec(qk), row_spec(qk), row_spec(mixw), row_spec(mixw)],
        out_shape=(jax.ShapeDtypeStruct((m, qk), BF16), jax.ShapeDtypeStruct((m, qk), BF16),
                   jax.ShapeDtypeStruct((m, mixw), BF16), jax.ShapeDtypeStruct((m, mixw), BF16)),
        compiler_params=_params("parallel"),
        name="layer1_in_proj",
    )(h1, w_in_od, cos1[N_META:], sin1[N_META:])

    head_spec = lambda w: pl.BlockSpec((1, seq, w), lambda b, h: (b, 0, h))
    y_ret = pl.pallas_call(
        _retention_kernel,
        grid=(bsz, RET_HEADS),
        in_specs=[head_spec(RET_QK_DIM), head_spec(RET_QK_DIM), head_spec(RET_V_DIM), head_spec(RET_V_DIM),
                  pl.BlockSpec((1, RET_QK_DIM, RET_V_DIM), lambda b, h: (h, 0, 0)),
                  pl.BlockSpec((1, 1, LANE), lambda b, h: (h, 0, 0))],
        out_specs=head_spec(RET_V_DIM),
        out_shape=jax.ShapeDtypeStruct((bsz, seq, mixw), BF16),
        scratch_shapes=[pltpu.VMEM((RET_QK_DIM, RET_V_DIM), F32)],
        compiler_params=_params("parallel", "parallel"),
        name="retention",
    )(q1.reshape(bsz, seq, qk), k1.reshape(bsz, seq, qk), v1.reshape(bsz, seq, mixw),
      g1.reshape(bsz, seq, mixw), s_meta, log_gamma)

    out = _mix_mlp_call("layer1_out_mlp", h1, [y_ret.reshape(m, mixw)], w_out_od,
                        ln[1][0], ln[1][1], w1[1], w2[1], ln[1][2], ln[1][3])
    return out.reshape(bsz, seq, D_MODEL)
```

```python
import functools
import math

import jax
import jax.numpy as jnp
import numpy as np
from jax import lax
from jax.experimental import pallas as pl
from jax.experimental.pallas import tpu as pltpu

D_MODEL = 1024
N_META = 16
LRU_WIDTH = 512
LRU_HEADS = 4
LRU_HEAD_DIM = 128
CONV_WIDTH = 4
LRU_C = 8.0
MLA_HEADS = 8
MLA_NOPE = 64
MLA_ROPE = 32
MLA_V = 64
MLA_Q_RANK = 256
MLA_KV_RANK = 128
RET_HEADS = 4
RET_QK_DIM = 256
RET_V_DIM = 512
D_FF = 4096
ROPE_BASE = 10000.0
DN_ALPHA = 4.0 ** 0.25
EPS = 1e-5
NEG_INF = -1e30

LANE = 128
SUBLANE = 8
VMEM_LIMIT = 56 * 1024 * 1024

BF16 = jnp.bfloat16
F32 = jnp.float32

SEQ_TILE = 512
ATT_TILE = 512
ROW_TILE = 512
FF_TILE = 1024
RET_CHUNK = 128


def _dot(a, b):
    return jnp.dot(a, b, preferred_element_type=F32)


def _dot_nt(a, b):
    return lax.dot_general(a, b, (((1,), (1,)), ((), ())), preferred_element_type=F32)


def _dot_tn(a, b):
    return lax.dot_general(a, b, (((0,), (0,)), ((), ())), preferred_element_type=F32)


def _layernorm(x, g, b):
    mu = jnp.mean(x, axis=-1, keepdims=True)
    xc = x - mu
    var = jnp.mean(xc * xc, axis=-1, keepdims=True)
    return xc * lax.rsqrt(var + EPS) * g + b


def _rmsnorm(x, g):
    return x * lax.rsqrt(jnp.mean(x * x, axis=-1, keepdims=True) + EPS) * g


def _rope_mla(x, c, s_up, s_dn):
    return x * c + pltpu.roll(x, MLA_ROPE // 2, 1) * s_up + pltpu.roll(x, LANE - MLA_ROPE // 2, 1) * s_dn


def _lru_gates(xc, gate_w_ref, b_a, b_x, sp_lambda):
    rs, is_ = [], []
    for h in range(LRU_HEADS):
        g = _dot(xc[:, h * LRU_HEAD_DIM:(h + 1) * LRU_HEAD_DIM].astype(BF16), gate_w_ref[h])
        rs.append(g[:, :LRU_HEAD_DIM])
        is_.append(g[:, LRU_HEAD_DIM:])
    r = jax.nn.sigmoid(jnp.concatenate(rs, axis=1) + b_a)
    i = jax.nn.sigmoid(jnp.concatenate(is_, axis=1) + b_x)
    log_a = -LRU_C * r * sp_lambda
    a = jnp.exp(log_a)
    mult = jnp.sqrt(1.0 - a * a)
    return a, mult * (i * xc)


def _scan8(a, b):
    row = lax.broadcasted_iota(jnp.int32, a.shape, 0)
    for k in (1, 2, 4):
        keep = row >= k
        a_prev = jnp.where(keep, pltpu.roll(a, k, 0), 1.0)
        b_prev = jnp.where(keep, pltpu.roll(b, k, 0), 0.0)
        b = a * b_prev + b
        a = a * a_prev
    return a, b


def _lru_scan(a_ref, b_ref, h0, rows):
    def body(g, h_prev):
        sl = pl.ds(pl.multiple_of(g * SUBLANE, SUBLANE), SUBLANE)
        a_c, b_c = _scan8(a_ref[sl, :], b_ref[sl, :])
        h = a_c * h_prev + b_c
        b_ref[sl, :] = h
        return h[SUBLANE - 1:SUBLANE, :]
    return lax.fori_loop(0, rows // SUBLANE, body, h0, unroll=4)


def _mla_project(qlat, kvlat, kpe, q_g, kv_g, w_uq_ref, w_ukv_ref, rope_c, rope_up, rope_dn):
    scale = (MLA_NOPE + MLA_ROPE) ** -0.5
    q_all = _dot(_rmsnorm(qlat, q_g).astype(BF16), w_uq_ref[...])
    kv_all = _dot(_rmsnorm(kvlat, kv_g).astype(BF16), w_ukv_ref[...])
    kpe_r = _rope_mla(kpe, rope_c, rope_up, rope_dn)
    qs, ks = [], []
    for h in range(MLA_HEADS):
        sl = slice(h * LANE, (h + 1) * LANE)
        qs.append((_rope_mla(q_all[:, sl], rope_c, rope_up, rope_dn) * scale).astype(BF16))
        ks.append((kv_all[:, sl] + kpe_r).astype(BF16))
    v = kv_all[:, MLA_HEADS * LANE:].astype(BF16)
    return jnp.concatenate(qs, axis=1), jnp.concatenate(ks, axis=1), v


def _mlp_block(h_in, mix, ln1_g, ln1_b, w1_ref, w2_ref, ln2_g, ln2_b):
    h1 = _layernorm(DN_ALPHA * h_in + mix, ln1_g, ln1_b)
    h1b = h1.astype(BF16)
    f = None
    for c in range(D_FF // FF_TILE):
        a = _dot(h1b, w1_ref[:, c * FF_TILE:(c + 1) * FF_TILE])
        a = jnp.maximum(a, 0.0)
        part = _dot((a * a).astype(BF16), w2_ref[c * FF_TILE:(c + 1) * FF_TILE, :])
        f = part if f is None else f + part
    return _layernorm(DN_ALPHA * h1 + f, ln2_g, ln2_b)


def _meta_kernel(meta_ref, w_in_ref, conv_w_ref, conv_b_ref, gate_w_ref, b_a_ref, b_x_ref, lam_ref,
                 q_g_ref, kv_g_ref, w_uq_ref, w_ukv_ref, w_uvt_ref, rope_c_ref, rope_up_ref, rope_dn_ref,
                 w_out_ref, ln1_g_ref, ln1_b_ref, w1_ref, w2_ref, ln2_g_ref, ln2_b_ref,
                 w_k_ref, w_v_ref, cos1_ref, sin1_ref, lg_ref,
                 rec_tail_ref, h_tail_ref, k_meta_ref, vt_meta_ref, s_meta_ref,
                 conv_scr, a_scr, b_scr):
    n = N_META
    x = meta_ref[...]
    p = _dot(x.astype(BF16), w_in_ref[...])
    gate, rec = p[:, :LRU_WIDTH], p[:, LRU_WIDTH:2 * LRU_WIDTH]
    conv_scr[0:SUBLANE, :] = jnp.zeros((SUBLANE, LRU_WIDTH), F32)
    conv_scr[SUBLANE:SUBLANE + n, :] = rec
    cw = conv_w_ref[...]
    xc = conv_b_ref[...] + cw[3:4, :] * rec
    for j in range(CONV_WIDTH - 1):
        off = SUBLANE - (CONV_WIDTH - 1) + j
        xc = xc + cw[j:j + 1, :] * conv_scr[off:off + n, :]
    sp_lambda = jax.nn.softplus(-lam_ref[...])
    a, b = _lru_gates(xc, gate_w_ref, b_a_ref[...], b_x_ref[...], sp_lambda)
    a_scr[...] = a
    b_scr[...] = b
    _lru_scan(a_scr, b_scr, jnp.zeros((1, LRU_WIDTH), F32), n)
    h = b_scr[...]
    y_rec = (h * jax.nn.gelu(gate)).astype(BF16)
    rec_tail_ref[...] = rec[n - SUBLANE:, :]
    h_tail_ref[...] = h[n - SUBLANE:, :]

    off = 2 * LRU_WIDTH
    q, k, v = _mla_project(p[:, off:off + MLA_Q_RANK],
                           p[:, off + MLA_Q_RANK:off + MLA_Q_RANK + MLA_KV_RANK],
                           p[:, off + MLA_Q_RANK + MLA_KV_RANK:],
                           q_g_ref[...], kv_g_ref[...], w_uq_ref, w_ukv_ref,
                           rope_c_ref[...], rope_up_ref[...], rope_dn_ref[...])
    k_meta_ref[...] = k
    kvn = _rmsnorm(p[:, off + MLA_Q_RANK:off + MLA_Q_RANK + MLA_KV_RANK], kv_g_ref[...]).astype(BF16)
    vt_meta_ref[...] = _dot_nt(w_uvt_ref[...], kvn).astype(BF16)
    causal = (lax.broadcasted_iota(jnp.int32, (n, n), 1) <= lax.broadcasted_iota(jnp.int32, (n, n), 0))
    outs = []
    for hh in range(MLA_HEADS):
        sl = slice(hh * LANE, (hh + 1) * LANE)
        s = jnp.where(causal, _dot_nt(q[:, sl], k[:, sl]), NEG_INF)
        e = jnp.exp(s - jnp.max(s, axis=-1, keepdims=True))
        pr = e / jnp.sum(e, axis=-1, keepdims=True)
        outs.append(_dot(pr.astype(BF16), v[:, hh * MLA_V:(hh + 1) * MLA_V]))
    y_att = jnp.concatenate(outs, axis=1).astype(BF16)
    mix = _dot(y_rec, w_out_ref[0:LRU_WIDTH, :]) + _dot(y_att, w_out_ref[LRU_WIDTH:, :])
    h2 = _mlp_block(x, mix, ln1_g_ref[...], ln1_b_ref[...], w1_ref, w2_ref, ln2_g_ref[...], ln2_b_ref[...])

    h2b = h2.astype(BF16)
    kk = _dot(h2b, w_k_ref[...])
    vv = _dot(h2b, w_v_ref[...]).astype(BF16)
    cos, sin = cos1_ref[...], sin1_ref[...]
    idx = lax.broadcasted_iota(jnp.int32, (n, 1), 0).astype(F32)
    half = RET_QK_DIM // 2
    for hh in range(RET_HEADS):
        log_gamma = lg_ref[hh][:, 0:1]
        k1 = kk[:, hh * RET_QK_DIM:hh * RET_QK_DIM + half]
        k2 = kk[:, hh * RET_QK_DIM + half:(hh + 1) * RET_QK_DIM]
        kr = jnp.concatenate([k1 * cos - k2 * sin, k1 * sin + k2 * cos], axis=1).astype(BF16)
        k_dec = jnp.exp(log_gamma * (n - 1.0 - idx))
        kd = (kr.astype(F32) * k_dec).astype(BF16)
        s_meta_ref[hh] = _dot_tn(kd, vv[:, hh * RET_V_DIM:(hh + 1) * RET_V_DIM])


def _seq0_kernel(x_ref, w_in_ref, conv_w_ref, conv_b_ref, gate_w_ref, b_a_ref, b_x_ref, lam_ref,
                 q_g_ref, kv_g_ref, w_uqt_ref, w_uk_ref, w_uvt_ref, rope_c_ref, rope_up_ref, rope_dn_ref,
                 cos_t_ref, sin_t_ref, rec_tail_ref, h_tail_ref,
                 y_rec_ref, qt_ref, k_ref, vt_ref,
                 conv_scr, a_scr, b_scr, h_scr):
    ts = SEQ_TILE
    t = pl.program_id(1)

    @pl.when(t == 0)
    def _():
        conv_scr[0:SUBLANE, :] = rec_tail_ref[...]
        h_scr[...] = h_tail_ref[...]

    p = _dot(x_ref[0].astype(BF16), w_in_ref[...])
    gate, rec = p[:, :LRU_WIDTH], p[:, LRU_WIDTH:2 * LRU_WIDTH]
    conv_scr[SUBLANE:SUBLANE + ts, :] = rec
    cw = conv_w_ref[...]
    xc = conv_b_ref[...] + cw[3:4, :] * rec
    for j in range(CONV_WIDTH - 1):
        off = SUBLANE - (CONV_WIDTH - 1) + j
        xc = xc + cw[j:j + 1, :] * conv_scr[off:off + ts, :]
    conv_scr[0:SUBLANE, :] = rec[ts - SUBLANE:, :]
    sp_lambda = jax.nn.softplus(-lam_ref[...])
    a, b = _lru_gates(xc, gate_w_ref, b_a_ref[...], b_x_ref[...], sp_lambda)
    a_scr[...] = a
    b_scr[...] = b
    h_last = _lru_scan(a_scr, b_scr, h_scr[SUBLANE - 1:SUBLANE, :], ts)
    h_scr[SUBLANE - 1:SUBLANE, :] = h_last
    y_rec_ref[0] = (b_scr[...] * jax.nn.gelu(gate)).astype(BF16)

    off = 2 * LRU_WIDTH
    scale = (MLA_NOPE + MLA_ROPE) ** -0.5
    hr = MLA_ROPE // 2
    qn = _rmsnorm(p[:, off:off + MLA_Q_RANK], q_g_ref[...]).astype(BF16)
    q_t = _dot_nt(w_uqt_ref[...], qn)
    cos_t, sin_t = cos_t_ref[...] * scale, sin_t_ref[...] * scale
    for h in range(MLA_HEADS):
        base = h * LANE
        x1 = q_t[base + MLA_NOPE:base + MLA_NOPE + hr, :]
        x2 = q_t[base + MLA_NOPE + hr:base + MLA_NOPE + MLA_ROPE, :]
        qt_ref[0, 0, base:base + MLA_NOPE, :] = (q_t[base:base + MLA_NOPE, :] * scale).astype(BF16)
        qt_ref[0, 0, base + MLA_NOPE:base + MLA_NOPE + hr, :] = (x1 * cos_t - x2 * sin_t).astype(BF16)
        qt_ref[0, 0, base + MLA_NOPE + hr:base + MLA_NOPE + MLA_ROPE, :] = (x1 * sin_t + x2 * cos_t).astype(BF16)
        qt_ref[0, 0, base + MLA_NOPE + MLA_ROPE:base + LANE, :] = jnp.zeros(
            (LANE - MLA_NOPE - MLA_ROPE, ts), BF16)
    kvn = _rmsnorm(p[:, off + MLA_Q_RANK:off + MLA_Q_RANK + MLA_KV_RANK], kv_g_ref[...]).astype(BF16)
    k_nope = _dot(kvn, w_uk_ref[...])
    kpe_r = _rope_mla(p[:, off + MLA_Q_RANK + MLA_KV_RANK:], rope_c_ref[...], rope_up_ref[...], rope_dn_ref[...])
    for h in range(MLA_HEADS):
        sl = slice(h * LANE, (h + 1) * LANE)
        k_ref[0, :, sl] = (k_nope[:, sl] + kpe_r).astype(BF16)
    vt_ref[0, 0] = _dot_nt(w_uvt_ref[...], kvn).astype(BF16)


def _attn_kernel(qt_ref, k_ref, vt_ref, k_meta_ref, vt_meta_ref, o_ref, m_scr, l_scr, acc_scr):
    tq = ATT_TILE
    qi = pl.program_id(2)
    heads = [(hh, slice(hh * LANE, (hh + 1) * LANE), slice(hh * MLA_V, (hh + 1) * MLA_V)) for hh in range(2)]
    for hh, sl, vrows in heads:
        s = _dot(k_meta_ref[:, sl], qt_ref[0, 0, sl, :])
        m = jnp.max(s, axis=0, keepdims=True)
        e = jnp.exp(s - m)
        m_scr[hh] = m
        l_scr[hh] = jnp.sum(e, axis=0, keepdims=True)
        acc_scr[hh] = _dot(vt_meta_ref[vrows, :], e.astype(BF16))

    def step(kj, masked):
        rows = pl.ds(pl.multiple_of(kj * tq, tq), tq)
        scores = [_dot(k_ref[0, rows, sl], qt_ref[0, 0, sl, :]) for _, sl, _ in heads]
        for (hh, sl, vrows), s in zip(heads, scores):
            if masked:
                key = lax.broadcasted_iota(jnp.int32, (tq, tq), 0)
                qry = lax.broadcasted_iota(jnp.int32, (tq, tq), 1)
                s = jnp.where(key <= qry, s, NEG_INF)
            m_old = m_scr[hh]
            m_new = jnp.maximum(m_old, jnp.max(s, axis=0, keepdims=True))
            alpha = jnp.exp(m_old - m_new)
            e = jnp.exp(s - m_new)
            m_scr[hh] = m_new
            l_scr[hh] = alpha * l_scr[hh] + jnp.sum(e, axis=0, keepdims=True)
            acc_scr[hh] = alpha * acc_scr[hh] + _dot(vt_ref[0, kj, vrows, :], e.astype(BF16))

    def body(kj, c):
        step(kj, False)
        return c

    lax.fori_loop(0, qi, body, 0)
    step(qi, True)
    out_t = jnp.concatenate([acc_scr[0] / l_scr[0], acc_scr[1] / l_scr[1]], axis=0)
    o_ref[0] = out_t.T.astype(BF16)


def _mix_mlp_kernel(*refs, n_mix):
    h_ref = refs[0]
    y_refs = refs[1:1 + n_mix]
    w_out_ref, ln1_g, ln1_b, w1_ref, w2_ref, ln2_g, ln2_b, o_ref = refs[1 + n_mix:]
    mix = None
    row = 0
    for y_ref in y_refs:
        width = y_ref.shape[-1]
        part = _dot(y_ref[...], w_out_ref[row:row + width, :])
        mix = part if mix is None else mix + part
        row += width
    o_ref[...] = _mlp_block(h_ref[...], mix, ln1_g[...], ln1_b[...], w1_ref, w2_ref, ln2_g[...], ln2_b[...])


def _ret_proj_kernel(h_ref, w_ref, cos_ref, sin_ref, q_ref, k_ref, v_ref, g_ref):
    hb = h_ref[...].astype(BF16)
    qk = RET_HEADS * RET_QK_DIM
    half = RET_QK_DIM // 2
    k_scale = RET_QK_DIM ** -0.5
    for out_ref, base, scale in ((q_ref, 0, None), (k_ref, qk, k_scale)):
        cos, sin = cos_ref[...], sin_ref[...]
        if scale is not None:
            cos, sin = cos * scale, sin * scale
        pr = _dot(hb, w_ref[:, base:base + qk])
        parts = []
        for hh in range(RET_HEADS):
            x1 = pr[:, hh * RET_QK_DIM:hh * RET_QK_DIM + half]
            x2 = pr[:, hh * RET_QK_DIM + half:(hh + 1) * RET_QK_DIM]
            parts.append((x1 * cos - x2 * sin).astype(BF16))
            parts.append((x1 * sin + x2 * cos).astype(BF16))
        out_ref[...] = jnp.concatenate(parts, axis=1)
    mixw = RET_HEADS * RET_V_DIM
    v_ref[...] = _dot(hb, w_ref[:, 2 * qk:2 * qk + mixw]).astype(BF16)
    g_ref[...] = _dot(hb, w_ref[:, 2 * qk + mixw:]).astype(BF16)


def _retention_kernel(q_ref, k_ref, v_ref, g_ref, s0_ref, lg_ref, y_ref, s_scr):
    c = RET_CHUNK
    log_gamma = lg_ref[0][:, 0:1]
    ii = lax.broadcasted_iota(jnp.int32, (c, c), 0)
    jj = lax.broadcasted_iota(jnp.int32, (c, c), 1)
    diff = (ii - jj).astype(F32)
    decay = jnp.where(diff >= 0, jnp.exp(log_gamma * jnp.maximum(diff, 0.0)), 0.0)
    idx = lax.broadcasted_iota(jnp.int32, (c, 1), 0).astype(F32)
    q_decay = jnp.exp(log_gamma * (idx + 1.0))
    k_decay = jnp.exp(log_gamma * (c - 1.0 - idx))
    chunk_decay = jnp.exp(log_gamma * c)
    s_scr[...] = s0_ref[0]

    def body(ci, _):
        rows = pl.ds(pl.multiple_of(ci * c, c), c)
        q = q_ref[0, rows, :]
        k = k_ref[0, rows, :]
        v = v_ref[0, rows, :]
        s_prev = s_scr[...]
        scores = _dot_nt(q, k) * decay
        o = _dot(scores.astype(BF16), v) + q_decay * _dot(q, s_prev.astype(BF16))
        kd = (k.astype(F32) * k_decay).astype(BF16)
        s_scr[...] = chunk_decay * s_prev + _dot_tn(kd, v)
        o = o * lax.rsqrt(jnp.mean(o * o, axis=-1, keepdims=True) + EPS)
        y_ref[0, rows, :] = (jax.nn.silu(g_ref[0, rows, :].astype(F32)) * o).astype(BF16)
        return 0

    lax.fori_loop(0, q_ref.shape[1] // c, body, 0)


def _const_spec(shape):
    zeros = (0,) * len(shape)
    return pl.BlockSpec(shape, lambda *_: zeros, pipeline_mode=pl.Buffered(1))


def _params(*semantics):
    return pltpu.CompilerParams(dimension_semantics=semantics, vmem_limit_bytes=VMEM_LIMIT)


def _rope_tables(positions, half):
    inv = ROPE_BASE ** (-jnp.arange(half, dtype=F32) / half)
    ang = positions.astype(F32)[:, None] * inv[None, :]
    return jnp.cos(ang), jnp.sin(ang)


def _mla_rope_tables(positions):
    cos, sin = _rope_tables(positions, MLA_ROPE // 2)
    n = positions.shape[0]
    h = MLA_ROPE // 2
    ones = jnp.ones((n, MLA_NOPE), F32)
    zeros = lambda w: jnp.zeros((n, w), F32)
    c = jnp.concatenate([ones, cos, cos, zeros(LANE - MLA_NOPE - MLA_ROPE)], axis=1)
    s_up = jnp.concatenate([zeros(MLA_NOPE + h), sin, zeros(LANE - MLA_NOPE - MLA_ROPE)], axis=1)
    s_dn = jnp.concatenate([zeros(MLA_NOPE), -sin, zeros(LANE - MLA_NOPE - h)], axis=1)
    return c, s_up, s_dn


def _mix_mlp_call(name, h, ys, w_out, ln1_g, ln1_b, w1, w2, ln2_g, ln2_b):
    m = h.shape[0]
    tm = ROW_TILE
    row_spec = lambda w: pl.BlockSpec((tm, w), lambda i: (i, 0))
    vec = _const_spec((1, D_MODEL))
    return pl.pallas_call(
        functools.partial(_mix_mlp_kernel, n_mix=len(ys)),
        grid=(m // tm,),
        in_specs=[row_spec(D_MODEL)] + [row_spec(y.shape[1]) for y in ys] + [
            _const_spec(w_out.shape), vec, vec, _const_spec(w1.shape), _const_spec(w2.shape), vec, vec],
        out_specs=row_spec(D_MODEL),
        out_shape=jax.ShapeDtypeStruct((m, D_MODEL), F32),
        compiler_params=_params("parallel"),
        name=name,
    )(h, *ys, w_out, ln1_g, ln1_b, w1, w2, ln2_g, ln2_b)


def kernel(x, meta_tokens, ev_w_in, ev_conv_w, ev_conv_b, ev_w_rg_a, ev_b_rg_a, ev_w_rg_x, ev_b_rg_x,
           ev_lru_lambda, ev_q_norm_g, ev_w_uq, ev_kv_norm_g, ev_w_ukv, ev_w_out, od_w_in, od_w_out,
           ln_mix_g, ln_mix_b, mlp_w1, mlp_w2, ln_mlp_g, ln_mlp_b):
    bsz, seq, _ = x.shape
    row = lambda v: v.reshape(1, -1).astype(F32)

    w_in0 = ev_w_in[0]
    lat0 = 2 * LRU_WIDTH
    kpe0 = lat0 + MLA_Q_RANK + MLA_KV_RANK
    w_kpe = jnp.zeros((D_MODEL, LANE), F32).at[:, MLA_NOPE:MLA_NOPE + MLA_ROPE].set(w_in0[:, kpe0:])
    w_in_ev = jnp.concatenate([w_in0[:, :kpe0], w_kpe], axis=1).astype(BF16)
    gate_w = jnp.concatenate([ev_w_rg_a[0], ev_w_rg_x[0]], axis=2).astype(BF16)
    w_uq = ev_w_uq[0].reshape(MLA_Q_RANK, MLA_HEADS, MLA_NOPE + MLA_ROPE)
    w_uq = jnp.pad(w_uq, ((0, 0), (0, 0), (0, LANE - MLA_NOPE - MLA_ROPE)))
    w_uq = w_uq.reshape(MLA_Q_RANK, MLA_HEADS * LANE).astype(BF16)
    w_ukv = ev_w_ukv[0].reshape(MLA_KV_RANK, MLA_HEADS, MLA_NOPE + MLA_V)
    w_uk = jnp.pad(w_ukv[:, :, :MLA_NOPE], ((0, 0), (0, 0), (0, LANE - MLA_NOPE)))
    w_uk = w_uk.reshape(MLA_KV_RANK, MLA_HEADS * LANE)
    w_uv = w_ukv[:, :, MLA_NOPE:].reshape(MLA_KV_RANK, MLA_HEADS * MLA_V)
    w_ukv_p = jnp.concatenate([w_uk, w_uv], axis=1).astype(BF16)
    w_uq_t = w_uq.T
    w_uk = w_uk.astype(BF16)
    w_uv_t = w_uv.T.astype(BF16)
    w_out_ev = ev_w_out[0].astype(BF16)
    w_in_od = od_w_in[0].astype(BF16)
    w_out_od = od_w_out[0].astype(BF16)
    w1 = mlp_w1.astype(BF16)
    w2 = mlp_w2.astype(BF16)
    conv_w = ev_conv_w[0].astype(F32)
    conv_b, b_a, b_x, lam = row(ev_conv_b[0]), row(ev_b_rg_a[0]), row(ev_b_rg_x[0]), row(ev_lru_lambda[0])
    q_g, kv_g = row(ev_q_norm_g[0]), row(ev_kv_norm_g[0])
    ln = [(row(ln_mix_g[l]), row(ln_mix_b[l]), row(ln_mlp_g[l]), row(ln_mlp_b[l])) for l in range(2)]

    pos = jnp.arange(N_META + seq, dtype=jnp.int32)
    mla_tabs = _mla_rope_tables(pos)
    cos0, sin0 = _rope_tables(pos, MLA_ROPE // 2)
    cos0_t, sin0_t = cos0.T, sin0.T
    cos1, sin1 = _rope_tables(pos, RET_QK_DIM // 2)
    k_scale = RET_QK_DIM ** -0.5
    qk = RET_HEADS * RET_QK_DIM
    log_gamma = jnp.log(1.0 - 2.0 ** (-5.0 - jnp.arange(RET_HEADS, dtype=F32)))
    log_gamma = jnp.broadcast_to(log_gamma[:, None, None], (RET_HEADS, 1, LANE))

    meta_out = pl.pallas_call(
        _meta_kernel,
        out_shape=(jax.ShapeDtypeStruct((SUBLANE, LRU_WIDTH), F32),
                   jax.ShapeDtypeStruct((SUBLANE, LRU_WIDTH), F32),
                   jax.ShapeDtypeStruct((N_META, MLA_HEADS * LANE), BF16),
                   jax.ShapeDtypeStruct((MLA_HEADS * MLA_V, N_META), BF16),
                   jax.ShapeDtypeStruct((RET_HEADS, RET_QK_DIM, RET_V_DIM), F32)),
        scratch_shapes=[pltpu.VMEM((SUBLANE + N_META, LRU_WIDTH), F32),
                        pltpu.VMEM((N_META, LRU_WIDTH), F32),
                        pltpu.VMEM((N_META, LRU_WIDTH), F32)],
        compiler_params=pltpu.CompilerParams(vmem_limit_bytes=VMEM_LIMIT),
        name="meta_tokens",
    )(meta_tokens.astype(F32), w_in_ev, conv_w, conv_b, gate_w, b_a, b_x, lam, q_g, kv_g, w_uq, w_ukv_p, w_uv_t,
      *[t[:N_META] for t in mla_tabs], w_out_ev, *ln[0][:2], w1[0], w2[0], *ln[0][2:],
      w_in_od[:, qk:2 * qk], w_in_od[:, 2 * qk:2 * qk + RET_HEADS * RET_V_DIM],
      cos1[:N_META] * k_scale, sin1[:N_META] * k_scale, log_gamma)
    rec_tail, h_tail, k_meta, vt_meta, s_meta = meta_out

    ts = SEQ_TILE
    nt = seq // ts
    tab_spec = pl.BlockSpec((ts, LANE), lambda b, t: (t, 0))
    tab_t_spec = pl.BlockSpec((MLA_ROPE // 2, ts), lambda b, t: (0, t))
    seq_spec = lambda w: pl.BlockSpec((1, ts, w), lambda b, t: (b, t, 0))
    seq_t_spec = lambda w: pl.BlockSpec((1, 1, w, ts), lambda b, t: (b, t, 0, 0))
    y_rec, qt0, k0, vt0 = pl.pallas_call(
        _seq0_kernel,
        grid=(bsz, nt),
        in_specs=[seq_spec(D_MODEL), _const_spec(w_in_ev.shape), _const_spec(conv_w.shape),
                  _const_spec(conv_b.shape), _const_spec(gate_w.shape), _const_spec(b_a.shape),
                  _const_spec(b_x.shape), _const_spec(lam.shape), _const_spec(q_g.shape),
                  _const_spec(kv_g.shape), _const_spec(w_uq_t.shape), _const_spec(w_uk.shape),
                  _const_spec(w_uv_t.shape), tab_spec, tab_spec, tab_spec, tab_t_spec, tab_t_spec,
                  _const_spec(rec_tail.shape), _const_spec(h_tail.shape)],
        out_specs=[seq_spec(LRU_WIDTH), seq_t_spec(MLA_HEADS * LANE), seq_spec(MLA_HEADS * LANE),
                   seq_t_spec(MLA_HEADS * MLA_V)],
        out_shape=(jax.ShapeDtypeStruct((bsz, seq, LRU_WIDTH), BF16),
                   jax.ShapeDtypeStruct((bsz, nt, MLA_HEADS * LANE, ts), BF16),
                   jax.ShapeDtypeStruct((bsz, seq, MLA_HEADS * LANE), BF16),
                   jax.ShapeDtypeStruct((bsz, nt, MLA_HEADS * MLA_V, ts), BF16)),
        scratch_shapes=[pltpu.VMEM((SUBLANE + ts, LRU_WIDTH), F32),
                        pltpu.VMEM((ts, LRU_WIDTH), F32),
                        pltpu.VMEM((ts, LRU_WIDTH), F32),
                        pltpu.VMEM((SUBLANE, LRU_WIDTH), F32)],
        compiler_params=_params("parallel", "arbitrary"),
        name="seq0_mixer_proj",
    )(x, w_in_ev, conv_w, conv_b, gate_w, b_a, b_x, lam, q_g, kv_g, w_uq_t, w_uk, w_uv_t,
      *[t[N_META:] for t in mla_tabs], cos0_t[:, N_META:], sin0_t[:, N_META:], rec_tail, h_tail)

    tq = ATT_TILE
    assert tq == ts
    y_att = pl.pallas_call(
        _attn_kernel,
        grid=(bsz, MLA_HEADS // 2, seq // tq),
        in_specs=[pl.BlockSpec((1, 1, 2 * LANE, tq), lambda b, j, i: (b, i, j, 0)),
                  pl.BlockSpec((1, seq, 2 * LANE), lambda b, j, i: (b, 0, j)),
                  pl.BlockSpec((1, nt, LANE, tq), lambda b, j, i: (b, 0, j, 0)),
                  pl.BlockSpec((N_META, 2 * LANE), lambda b, j, i: (0, j)),
                  pl.BlockSpec((LANE, N_META), lambda b, j, i: (j, 0))],
        out_specs=pl.BlockSpec((1, tq, LANE), lambda b, j, i: (b, i, j)),
        out_shape=jax.ShapeDtypeStruct((bsz, seq, MLA_HEADS * MLA_V), BF16),
        scratch_shapes=[pltpu.VMEM((2, 1, tq), F32), pltpu.VMEM((2, 1, tq), F32),
                        pltpu.VMEM((2, MLA_V, tq), F32)],
        compiler_params=_params("parallel", "parallel", "arbitrary"),
        name="mla_attention",
    )(qt0, k0, vt0, k_meta, vt_meta)

    m = bsz * seq
    h1 = _mix_mlp_call("layer0_out_mlp", x.reshape(m, D_MODEL),
                       [y_rec.reshape(m, -1), y_att.reshape(m, -1)], w_out_ev,
                       ln[0][0], ln[0][1], w1[0], w2[0], ln[0][2], ln[0][3])

    tm = ROW_TILE
    per_seq = seq // tm
    row_spec = lambda w: pl.BlockSpec((tm, w), lambda i: (i, 0))
    rope_spec = pl.BlockSpec((tm, RET_QK_DIM // 2), lambda i: (i % per_seq, 0))
    mixw = RET_HEADS * RET_V_DIM
    q1, k1, v1, g1 = pl.pallas_call(
        _ret_proj_kernel,
        grid=(m // tm,),
        in_specs=[row_spec(D_MODEL), _const_spec(w_in_od.shape), rope_spec, rope_spec],
        out_specs=[row_spec(qk), row_spec(qk), row_spec(mixw), row_spec(mixw)],
        out_shape=(jax.ShapeDtypeStruct((m, qk), BF16), jax.ShapeDtypeStruct((m, qk), BF16),
                   jax.ShapeDtypeStruct((m, mixw), BF16), jax.ShapeDtypeStruct((m, mixw), BF16)),
        compiler_params=_params("parallel"),
        name="layer1_in_proj",
    )(h1, w_in_od, cos1[N_META:], sin1[N_META:])

    head_spec = lambda w: pl.BlockSpec((1, seq, w), lambda b, h: (b, 0, h))
    y_ret = pl.pallas_call(
        _retention_kernel,
        grid=(bsz, RET_HEADS),
        in_specs=[head_spec(RET_QK_DIM), head_spec(RET_QK_DIM), head_spec(RET_V_DIM), head_spec(RET_V_DIM),
                  pl.BlockSpec((1, RET_QK_DIM, RET_V_DIM), lambda b, h: (h, 0, 0)),
                  pl.BlockSpec((1, 1, LANE), lambda b, h: (h, 0, 0))],
        out_specs=head_spec(RET_V_DIM),
        out_shape=jax.ShapeDtypeStruct((bsz, seq, mixw), BF16),
        scratch_shapes=[pltpu.VMEM((RET_QK_DIM, RET_V_DIM), F32)],
        compiler_params=_params("parallel", "parallel"),
        name="retention",
    )(q1.reshape(bsz, seq, qk), k1.reshape(bsz, seq, qk), v1.reshape(bsz, seq, mixw),
      g1.reshape(bsz, seq, mixw), s_meta, log_gamma)

    out = _mix_mlp_call("layer1_out_mlp", h1, [y_ret.reshape(m, mixw)], w_out_od,
                        ln[1][0], ln[1][1], w1[1], w2[1], ln[1][2], ln[1][3])
    return out.reshape(bsz, seq, D_MODEL)
```

```python
import functools
import math

import jax
import jax.numpy as jnp
import numpy as np
from jax import lax
from jax.experimental import pallas as pl
from jax.experimental.pallas import tpu as pltpu

D_MODEL = 1024
N_META = 16
LRU_WIDTH = 512
LRU_HEADS = 4
LRU_HEAD_DIM = 128
CONV_WIDTH = 4
LRU_C = 8.0
MLA_HEADS = 8
MLA_NOPE = 64
MLA_ROPE = 32
MLA_V = 64
MLA_Q_RANK = 256
MLA_KV_RANK = 128
RET_HEADS = 4
RET_QK_DIM = 256
RET_V_DIM = 512
D_FF = 4096
ROPE_BASE = 10000.0
DN_ALPHA = 4.0 ** 0.25
EPS = 1e-5
NEG_INF = -1e30

LANE = 128
SUBLANE = 8
VMEM_LIMIT = 56 * 1024 * 1024

BF16 = jnp.bfloat16
F32 = jnp.float32

SEQ_TILE = 512
SEQ_SPLIT = 2
ATT_TILE = 512
ATT_SPLIT = 2
ROW_TILE = 512
FF_TILE = 1024
RET_CHUNK = 256


def _dot(a, b):
    return jnp.dot(a, b, preferred_element_type=F32)


def _dot_nt(a, b):
    return lax.dot_general(a, b, (((1,), (1,)), ((), ())), preferred_element_type=F32)


def _dot_tn(a, b):
    return lax.dot_general(a, b, (((0,), (0,)), ((), ())), preferred_element_type=F32)


def _layernorm(x, g, b):
    mu = jnp.mean(x, axis=-1, keepdims=True)
    xc = x - mu
    var = jnp.mean(xc * xc, axis=-1, keepdims=True)
    return xc * lax.rsqrt(var + EPS) * g + b


def _rmsnorm(x, g):
    return x * lax.rsqrt(jnp.mean(x * x, axis=-1, keepdims=True) + EPS) * g


def _rope_mla(x, c, s_up, s_dn):
    return x * c + pltpu.roll(x, MLA_ROPE // 2, 1) * s_up + pltpu.roll(x, LANE - MLA_ROPE // 2, 1) * s_dn


def _lru_gates(xc, gate_w_ref, b_a, b_x, sp_lambda):
    rs, is_ = [], []
    for h in range(LRU_HEADS):
        g = _dot(xc[:, h * LRU_HEAD_DIM:(h + 1) * LRU_HEAD_DIM].astype(BF16), gate_w_ref[h])
        rs.append(g[:, :LRU_HEAD_DIM])
        is_.append(g[:, LRU_HEAD_DIM:])
    r = jax.nn.sigmoid(jnp.concatenate(rs, axis=1) + b_a)
    i = jax.nn.sigmoid(jnp.concatenate(is_, axis=1) + b_x)
    log_a = -LRU_C * r * sp_lambda
    a = jnp.exp(log_a)
    y = 1.0 - a * a
    mult = jnp.where(y > 0.0, y * lax.rsqrt(y), 0.0)
    return a, mult * (i * xc)


def _scan8(a, b):
    row = lax.broadcasted_iota(jnp.int32, a.shape, 0)
    for k in (1, 2, 4):
        keep = row >= k
        a_prev = jnp.where(keep, pltpu.roll(a, k, 0), 1.0)
        b_prev = jnp.where(keep, pltpu.roll(b, k, 0), 0.0)
        b = a * b_prev + b
        a = a * a_prev
    return a, b


def _lru_scan(a_ref, b_ref, h0, rows):
    def body(g, h_prev):
        sl = pl.ds(pl.multiple_of(g * SUBLANE, SUBLANE), SUBLANE)
        a_c, b_c = _scan8(a_ref[sl, :], b_ref[sl, :])
        h = a_c * h_prev + b_c
        b_ref[sl, :] = h
        return h[SUBLANE - 1:SUBLANE, :]
    return lax.fori_loop(0, rows // SUBLANE, body, h0, unroll=4)


def _mla_project(qlat, kvlat, kpe, q_g, kv_g, w_uq_ref, w_ukv_ref, rope_c, rope_up, rope_dn):
    scale = (MLA_NOPE + MLA_ROPE) ** -0.5
    q_all = _dot(_rmsnorm(qlat, q_g).astype(BF16), w_uq_ref[...])
    kv_all = _dot(_rmsnorm(kvlat, kv_g).astype(BF16), w_ukv_ref[...])
    kpe_r = _rope_mla(kpe, rope_c, rope_up, rope_dn)
    qs, ks = [], []
    for h in range(MLA_HEADS):
        sl = slice(h * LANE, (h + 1) * LANE)
        qs.append((_rope_mla(q_all[:, sl], rope_c, rope_up, rope_dn) * scale).astype(BF16))
        ks.append((kv_all[:, sl] + kpe_r).astype(BF16))
    v = kv_all[:, MLA_HEADS * LANE:].astype(BF16)
    return jnp.concatenate(qs, axis=1), jnp.concatenate(ks, axis=1), v


def _mlp_block(h_in, mix, ln1_g, ln1_b, w1_ref, w2_ref, ln2_g, ln2_b):
    h1 = _layernorm(DN_ALPHA * h_in + mix, ln1_g, ln1_b)
    h1b = h1.astype(BF16)
    f = None
    for c in range(D_FF // FF_TILE):
        a = _dot(h1b, w1_ref[:, c * FF_TILE:(c + 1) * FF_TILE])
        a = jnp.maximum(a, 0.0)
        part = _dot((a * a).astype(BF16), w2_ref[c * FF_TILE:(c + 1) * FF_TILE, :])
        f = part if f is None else f + part
    return _layernorm(DN_ALPHA * h1 + f, ln2_g, ln2_b)


def _meta_kernel(meta_ref, w_in_ref, conv_w_ref, conv_b_ref, gate_w_ref, b_a_ref, b_x_ref, lam_ref,
                 q_g_ref, kv_g_ref, w_uq_ref, w_ukv_ref, w_uvt_ref, rope_c_ref, rope_up_ref, rope_dn_ref,
                 w_out_ref, ln1_g_ref, ln1_b_ref, w1_ref, w2_ref, ln2_g_ref, ln2_b_ref,
                 w_k_ref, w_v_ref, cos1_ref, sin1_ref, lg_ref,
                 rec_tail_ref, h_tail_ref, k_meta_ref, vt_meta_ref, s_meta_ref,
                 conv_scr, a_scr, b_scr):
    n = N_META
    x = meta_ref[...]
    p = _dot(x.astype(BF16), w_in_ref[...])
    gate, rec = p[:, :LRU_WIDTH], p[:, LRU_WIDTH:2 * LRU_WIDTH]
    conv_scr[0:SUBLANE, :] = jnp.zeros((SUBLANE, LRU_WIDTH), F32)
    conv_scr[SUBLANE:SUBLANE + n, :] = rec
    cw = conv_w_ref[...]
    xc = conv_b_ref[...] + cw[3:4, :] * rec
    for j in range(CONV_WIDTH - 1):
        off = SUBLANE - (CONV_WIDTH - 1) + j
        xc = xc + cw[j:j + 1, :] * conv_scr[off:off + n, :]
    sp_lambda = jax.nn.softplus(-lam_ref[...])
    a, b = _lru_gates(xc, gate_w_ref, b_a_ref[...], b_x_ref[...], sp_lambda)
    a_scr[...] = a
    b_scr[...] = b
    _lru_scan(a_scr, b_scr, jnp.zeros((1, LRU_WIDTH), F32), n)
    h = b_scr[...]
    y_rec = (h * jax.nn.gelu(gate)).astype(BF16)
    rec_tail_ref[...] = rec[n - SUBLANE:, :]
    h_tail_ref[...] = h[n - SUBLANE:, :]

    off = 2 * LRU_WIDTH
    q, k, v = _mla_project(p[:, off:off + MLA_Q_RANK],
                           p[:, off + MLA_Q_RANK:off + MLA_Q_RANK + MLA_KV_RANK],
                           p[:, off + MLA_Q_RANK + MLA_KV_RANK:],
                           q_g_ref[...], kv_g_ref[...], w_uq_ref, w_ukv_ref,
                           rope_c_ref[...], rope_up_ref[...], rope_dn_ref[...])
    k_meta_ref[...] = k
    kvn = _rmsnorm(p[:, off + MLA_Q_RANK:off + MLA_Q_RANK + MLA_KV_RANK], kv_g_ref[...]).astype(BF16)
    vt_meta_ref[...] = _dot_nt(w_uvt_ref[...], kvn).astype(BF16)
    causal = (lax.broadcasted_iota(jnp.int32, (n, n), 1) <= lax.broadcasted_iota(jnp.int32, (n, n), 0))
    outs = []
    for hh in range(MLA_HEADS):
        sl = slice(hh * LANE, (hh + 1) * LANE)
        s = jnp.where(causal, _dot_nt(q[:, sl], k[:, sl]), NEG_INF)
        e = jnp.exp(s - jnp.max(s, axis=-1, keepdims=True))
        pr = e / jnp.sum(e, axis=-1, keepdims=True)
        outs.append(_dot(pr.astype(BF16), v[:, hh * MLA_V:(hh + 1) * MLA_V]))
    y_att = jnp.concatenate(outs, axis=1).astype(BF16)
    mix = _dot(y_rec, w_out_ref[0:LRU_WIDTH, :]) + _dot(y_att, w_out_ref[LRU_WIDTH:, :])
    h2 = _mlp_block(x, mix, ln1_g_ref[...], ln1_b_ref[...], w1_ref, w2_ref, ln2_g_ref[...], ln2_b_ref[...])

    h2b = h2.astype(BF16)
    kk = _dot(h2b, w_k_ref[...])
    vv = _dot(h2b, w_v_ref[...]).astype(BF16)
    cos, sin = cos1_ref[...], sin1_ref[...]
    idx = lax.broadcasted_iota(jnp.int32, (n, 1), 0).astype(F32)
    half = RET_QK_DIM // 2
    for hh in range(RET_HEADS):
        log_gamma = lg_ref[hh][:, 0:1]
        k1 = kk[:, hh * RET_QK_DIM:hh * RET_QK_DIM + half]
        k2 = kk[:, hh * RET_QK_DIM + half:(hh + 1) * RET_QK_DIM]
        kr = jnp.concatenate([k1 * cos - k2 * sin, k1 * sin + k2 * cos], axis=1).astype(BF16)
        k_dec = jnp.exp(log_gamma * (n - 1.0 - idx))
        kd = (kr.astype(F32) * k_dec).astype(BF16)
        s_meta_ref[hh] = _dot_tn(kd, vv[:, hh * RET_V_DIM:(hh + 1) * RET_V_DIM])


def _seq0_kernel(x_ref, w_in_ref, conv_w_ref, conv_b_ref, gate_w_ref, b_a_ref, b_x_ref, lam_ref,
                 q_g_ref, kv_g_ref, w_uqt_ref, w_uk_ref, w_uvt_ref, rope_c_ref, rope_up_ref, rope_dn_ref,
                 cos_t_ref, sin_t_ref, rec_tail_ref, h_tail_ref,
                 y_rec_ref, qt_ref, k_ref, vt_ref,
                 conv_scr, a_scr, b_scr, g_scr, h_scr):
    ts = SEQ_TILE
    t = pl.program_id(1)

    @pl.when(t == 0)
    def _():
        conv_scr[0:SUBLANE, :] = rec_tail_ref[...]
        h_scr[...] = h_tail_ref[...]

    sub = ts // SEQ_SPLIT
    spans = [slice(i * sub, (i + 1) * sub) for i in range(SEQ_SPLIT)]
    ps = [_dot(x_ref[0, rows, :].astype(BF16), w_in_ref[...]) for rows in spans]
    cw = conv_w_ref[...]
    sp_lambda = jax.nn.softplus(-lam_ref[...])
    scale = (MLA_NOPE + MLA_ROPE) ** -0.5 * math.log2(math.e)
    hr = MLA_ROPE // 2
    off = 2 * LRU_WIDTH
    for rows, p in zip(spans, ps):
        gate, rec = p[:, :LRU_WIDTH], p[:, LRU_WIDTH:2 * LRU_WIDTH]
        conv_scr[SUBLANE:SUBLANE + sub, :] = rec
        xc = conv_b_ref[...] + cw[3:4, :] * rec
        for j in range(CONV_WIDTH - 1):
            o = SUBLANE - (CONV_WIDTH - 1) + j
            xc = xc + cw[j:j + 1, :] * conv_scr[o:o + sub, :]
        conv_scr[0:SUBLANE, :] = rec[sub - SUBLANE:, :]
        a, b = _lru_gates(xc, gate_w_ref, b_a_ref[...], b_x_ref[...], sp_lambda)
        a_scr[rows, :] = a
        b_scr[rows, :] = b
        g_scr[rows, :] = jax.nn.gelu(gate)

        qn = _rmsnorm(p[:, off:off + MLA_Q_RANK], q_g_ref[...]).astype(BF16)
        q_t = _dot_nt(w_uqt_ref[...], qn)
        cos_t, sin_t = cos_t_ref[:, rows] * scale, sin_t_ref[:, rows] * scale
        for h in range(MLA_HEADS):
            base = h * LANE
            x1 = q_t[base + MLA_NOPE:base + MLA_NOPE + hr, :]
            x2 = q_t[base + MLA_NOPE + hr:base + MLA_NOPE + MLA_ROPE, :]
            qt_ref[0, 0, base:base + MLA_NOPE, rows] = (q_t[base:base + MLA_NOPE, :] * scale).astype(BF16)
            qt_ref[0, 0, base + MLA_NOPE:base + MLA_NOPE + hr, rows] = (x1 * cos_t - x2 * sin_t).astype(BF16)
            qt_ref[0, 0, base + MLA_NOPE + hr:base + MLA_NOPE + MLA_ROPE, rows] = (
                x1 * sin_t + x2 * cos_t).astype(BF16)
            qt_ref[0, 0, base + MLA_NOPE + MLA_ROPE:base + LANE, rows] = jnp.zeros(
                (LANE - MLA_NOPE - MLA_ROPE, sub), BF16)
        kvn = _rmsnorm(p[:, off + MLA_Q_RANK:off + MLA_Q_RANK + MLA_KV_RANK], kv_g_ref[...]).astype(BF16)
        k_nope = _dot(kvn, w_uk_ref[...])
        kpe_r = _rope_mla(p[:, off + MLA_Q_RANK + MLA_KV_RANK:],
                          rope_c_ref[rows, :], rope_up_ref[rows, :], rope_dn_ref[rows, :])
        for h in range(MLA_HEADS):
            sl = slice(h * LANE, (h + 1) * LANE)
            k_ref[0, rows, sl] = (k_nope[:, sl] + kpe_r).astype(BF16)
        vt_ref[0, 0, :, rows] = _dot_nt(w_uvt_ref[...], kvn).astype(BF16)

    h_last = _lru_scan(a_scr, b_scr, h_scr[SUBLANE - 1:SUBLANE, :], ts)
    h_scr[SUBLANE - 1:SUBLANE, :] = h_last
    y_rec_ref[0] = (b_scr[...] * g_scr[...]).astype(BF16)


def _attn_kernel(qt_ref, k_ref, vt_ref, k_meta_ref, vt_meta_ref, o_ref, m_scr, l_scr, acc_scr):
    tq = ATT_TILE
    tw = tq // ATT_SPLIT
    qi = pl.program_id(2)
    chains = [(hh, slice(hh * LANE, (hh + 1) * LANE), slice(hh * MLA_V, (hh + 1) * MLA_V),
               part, slice(part * tw, (part + 1) * tw))
              for hh in range(2) for part in range(ATT_SPLIT)]
    def meta_scores():
        return tuple(_dot(k_meta_ref[:, sl], qt_ref[0, 0, sl, cols]) for _, sl, _, _, cols in chains)

    def meta_softmax_pv(scores):
        for (hh, sl, vrows, part, cols), s in zip(chains, scores):
            m = jnp.max(s, axis=0, keepdims=True)
            e = jnp.exp2(s - m)
            m_scr[hh, :, cols] = m
            l_scr[hh, :, cols] = jnp.sum(e, axis=0, keepdims=True)
            acc_scr[hh, :, cols] = _dot(vt_meta_ref[vrows, :], e.astype(BF16))

    def n_keys(part, diagonal):
        return (part + 1) * tw if diagonal else tq

    def scores_for(kj, diagonal):
        return tuple(_dot(k_ref[0, kj * tq:kj * tq + n_keys(part, diagonal), sl], qt_ref[0, 0, sl, cols])
                     for _, sl, _, part, cols in chains)

    def softmax_pv(kj, scores, diagonal):
        for (hh, sl, vrows, part, cols), s in zip(chains, scores):
            nk = n_keys(part, diagonal)
            if diagonal:
                key = lax.broadcasted_iota(jnp.int32, (nk, tw), 0)
                qry = lax.broadcasted_iota(jnp.int32, (nk, tw), 1) + part * tw
                s = jnp.where(key <= qry, s, NEG_INF)
            m_old = m_scr[hh, :, cols]
            m_new = jnp.maximum(m_old, jnp.max(s, axis=0, keepdims=True))
            alpha = jnp.exp2(m_old - m_new)
            e = jnp.exp2(s - m_new)
            m_scr[hh, :, cols] = m_new
            l_scr[hh, :, cols] = alpha * l_scr[hh, :, cols] + jnp.sum(e, axis=0, keepdims=True)
            acc_scr[hh, :, cols] = alpha * acc_scr[hh, :, cols] + _dot(vt_ref[0, kj, vrows, 0:nk], e.astype(BF16))

    def run(n_full):
        s_meta = meta_scores()
        scores = scores_for(0, n_full == 0)
        meta_softmax_pv(s_meta)
        for kj in range(n_full):
            nxt = scores_for(kj + 1, kj + 1 == n_full)
            softmax_pv(kj, scores, False)
            scores = nxt
        softmax_pv(n_full, scores, True)

    for n_full in range(k_ref.shape[1] // tq):
        pl.when(qi == n_full)(functools.partial(run, n_full))
    out_t = jnp.concatenate([acc_scr[0] / l_scr[0], acc_scr[1] / l_scr[1]], axis=0)
    o_ref[0] = out_t.T.astype(BF16)


def _mix_mlp_kernel(*refs, n_mix):
    h_ref = refs[0]
    y_refs = refs[1:1 + n_mix]
    w_out_ref, ln1_g, ln1_b, w1_ref, w2_ref, ln2_g, ln2_b, o_ref = refs[1 + n_mix:]
    mix = None
    row = 0
    for y_ref in y_refs:
        width = y_ref.shape[-1]
        part = _dot(y_ref[...], w_out_ref[row:row + width, :])
        mix = part if mix is None else mix + part
        row += width
    o_ref[...] = _mlp_block(h_ref[...], mix, ln1_g[...], ln1_b[...], w1_ref, w2_ref, ln2_g[...], ln2_b[...])


def _ret_proj_kernel(h_ref, w_ref, cos_ref, sin_ref, q_ref, k_ref, v_ref, g_ref):
    hb = h_ref[...].astype(BF16)
    qk = RET_HEADS * RET_QK_DIM
    half = RET_QK_DIM // 2
    k_scale = RET_QK_DIM ** -0.5
    for out_ref, base, scale in ((q_ref, 0, None), (k_ref, qk, k_scale)):
        cos, sin = cos_ref[...], sin_ref[...]
        if scale is not None:
            cos, sin = cos * scale, sin * scale
        pr = _dot(hb, w_ref[:, base:base + qk])
        parts = []
        for hh in range(RET_HEADS):
            x1 = pr[:, hh * RET_QK_DIM:hh * RET_QK_DIM + half]
            x2 = pr[:, hh * RET_QK_DIM + half:(hh + 1) * RET_QK_DIM]
            parts.append((x1 * cos - x2 * sin).astype(BF16))
            parts.append((x1 * sin + x2 * cos).astype(BF16))
        out_ref[...] = jnp.concatenate(parts, axis=1)
    mixw = RET_HEADS * RET_V_DIM
    v_ref[...] = _dot(hb, w_ref[:, 2 * qk:2 * qk + mixw]).astype(BF16)
    g_ref[...] = _dot(hb, w_ref[:, 2 * qk + mixw:]).astype(BF16)


def _retention_kernel(q_ref, k_ref, v_ref, g_ref, s0_ref, lg_ref, y_ref, s_scr):
    c = RET_CHUNK
    log_gamma = lg_ref[0][:, 0:1]
    ii = lax.broadcasted_iota(jnp.int32, (c, c), 0)
    jj = lax.broadcasted_iota(jnp.int32, (c, c), 1)
    diff = (ii - jj).astype(F32)
    decay = jnp.where(diff >= 0, jnp.exp(log_gamma * jnp.maximum(diff, 0.0)), 0.0)
    idx = lax.broadcasted_iota(jnp.int32, (c, 1), 0).astype(F32)
    q_decay = jnp.exp(log_gamma * (idx + 1.0))
    k_decay = jnp.exp(log_gamma * (c - 1.0 - idx))
    chunk_decay = jnp.exp(log_gamma * c)
    s_scr[...] = s0_ref[0]

    def recur(ci):
        rows = slice(ci * c, (ci + 1) * c)
        q = q_ref[0, rows, :]
        k = k_ref[0, rows, :]
        v = v_ref[0, rows, :]
        s_prev = s_scr[...]
        scores = _dot_nt(q, k) * decay
        o = _dot(scores.astype(BF16), v) + q_decay * _dot(q, s_prev.astype(BF16))
        kd = (k.astype(F32) * k_decay).astype(BF16)
        s_scr[...] = chunk_decay * s_prev + _dot_tn(kd, v)
        return o

    def finish(ci, o):
        rows = slice(ci * c, (ci + 1) * c)
        o = o * lax.rsqrt(jnp.mean(o * o, axis=-1, keepdims=True) + EPS)
        y_ref[0, rows, :] = (jax.nn.silu(g_ref[0, rows, :].astype(F32)) * o).astype(BF16)

    o_prev = None
    for ci in range(q_ref.shape[1] // c):
        o = recur(ci)
        if o_prev is not None:
            finish(ci - 1, o_prev)
        o_prev = o
    finish(q_ref.shape[1] // c - 1, o_prev)


def _const_spec(shape):
    zeros = (0,) * len(shape)
    return pl.BlockSpec(shape, lambda *_: zeros, pipeline_mode=pl.Buffered(1))


def _params(*semantics):
    return pltpu.CompilerParams(dimension_semantics=semantics, vmem_limit_bytes=VMEM_LIMIT)


def _rope_tables(positions, half):
    inv = ROPE_BASE ** (-jnp.arange(half, dtype=F32) / half)
    ang = positions.astype(F32)[:, None] * inv[None, :]
    return jnp.cos(ang), jnp.sin(ang)


def _mla_rope_tables(positions):
    cos, sin = _rope_tables(positions, MLA_ROPE // 2)
    n = positions.shape[0]
    h = MLA_ROPE // 2
    ones = jnp.ones((n, MLA_NOPE), F32)
    zeros = lambda w: jnp.zeros((n, w), F32)
    c = jnp.concatenate([ones, cos, cos, zeros(LANE - MLA_NOPE - MLA_ROPE)], axis=1)
    s_up = jnp.concatenate([zeros(MLA_NOPE + h), sin, zeros(LANE - MLA_NOPE - MLA_ROPE)], axis=1)
    s_dn = jnp.concatenate([zeros(MLA_NOPE), -sin, zeros(LANE - MLA_NOPE - h)], axis=1)
    return c, s_up, s_dn


def _mix_mlp_call(name, h, ys, w_out, ln1_g, ln1_b, w1, w2, ln2_g, ln2_b):
    m = h.shape[0]
    tm = ROW_TILE
    row_spec = lambda w: pl.BlockSpec((tm, w), lambda i: (i, 0))
    vec = _const_spec((1, D_MODEL))
    return pl.pallas_call(
        functools.partial(_mix_mlp_kernel, n_mix=len(ys)),
        grid=(m // tm,),
        in_specs=[row_spec(D_MODEL)] + [row_spec(y.shape[1]) for y in ys] + [
            _const_spec(w_out.shape), vec, vec, _const_spec(w1.shape), _const_spec(w2.shape), vec, vec],
        out_specs=row_spec(D_MODEL),
        out_shape=jax.ShapeDtypeStruct((m, D_MODEL), F32),
        compiler_params=_params("parallel"),
        name=name,
    )(h, *ys, w_out, ln1_g, ln1_b, w1, w2, ln2_g, ln2_b)


def kernel(x, meta_tokens, ev_w_in, ev_conv_w, ev_conv_b, ev_w_rg_a, ev_b_rg_a, ev_w_rg_x, ev_b_rg_x,
           ev_lru_lambda, ev_q_norm_g, ev_w_uq, ev_kv_norm_g, ev_w_ukv, ev_w_out, od_w_in, od_w_out,
           ln_mix_g, ln_mix_b, mlp_w1, mlp_w2, ln_mlp_g, ln_mlp_b):
    bsz, seq, _ = x.shape
    row = lambda v: v.reshape(1, -1).astype(F32)

    w_in0 = ev_w_in[0]
    lat0 = 2 * LRU_WIDTH
    kpe0 = lat0 + MLA_Q_RANK + MLA_KV_RANK
    w_kpe = jnp.zeros((D_MODEL, LANE), F32).at[:, MLA_NOPE:MLA_NOPE + MLA_ROPE].set(w_in0[:, kpe0:])
    w_in_ev = jnp.concatenate([w_in0[:, :kpe0], w_kpe], axis=1).astype(BF16)
    gate_w = jnp.concatenate([ev_w_rg_a[0], ev_w_rg_x[0]], axis=2).astype(BF16)
    w_uq = ev_w_uq[0].reshape(MLA_Q_RANK, MLA_HEADS, MLA_NOPE + MLA_ROPE)
    w_uq = jnp.pad(w_uq, ((0, 0), (0, 0), (0, LANE - MLA_NOPE - MLA_ROPE)))
    w_uq = w_uq.reshape(MLA_Q_RANK, MLA_HEADS * LANE).astype(BF16)
    w_ukv = ev_w_ukv[0].reshape(MLA_KV_RANK, MLA_HEADS, MLA_NOPE + MLA_V)
    w_uk = jnp.pad(w_ukv[:, :, :MLA_NOPE], ((0, 0), (0, 0), (0, LANE - MLA_NOPE)))
    w_uk = w_uk.reshape(MLA_KV_RANK, MLA_HEADS * LANE)
    w_uv = w_ukv[:, :, MLA_NOPE:].reshape(MLA_KV_RANK, MLA_HEADS * MLA_V)
    w_ukv_p = jnp.concatenate([w_uk, w_uv], axis=1).astype(BF16)
    w_uq_t = w_uq.T
    w_uk = w_uk.astype(BF16)
    w_uv_t = w_uv.T.astype(BF16)
    w_out_ev = ev_w_out[0].astype(BF16)
    w_in_od = od_w_in[0].astype(BF16)
    w_out_od = od_w_out[0].astype(BF16)
    w1 = mlp_w1.astype(BF16)
    w2 = mlp_w2.astype(BF16)
    conv_w = ev_conv_w[0].astype(F32)
    conv_b, b_a, b_x, lam = row(ev_conv_b[0]), row(ev_b_rg_a[0]), row(ev_b_rg_x[0]), row(ev_lru_lambda[0])
    q_g, kv_g = row(ev_q_norm_g[0]), row(ev_kv_norm_g[0])
    ln = [(row(ln_mix_g[l]), row(ln_mix_b[l]), row(ln_mlp_g[l]), row(ln_mlp_b[l])) for l in range(2)]

    pos = jnp.arange(N_META + seq, dtype=jnp.int32)
    mla_tabs = _mla_rope_tables(pos)
    cos0, sin0 = _rope_tables(pos, MLA_ROPE // 2)
    cos0_t, sin0_t = cos0.T, sin0.T
    cos1, sin1 = _rope_tables(pos, RET_QK_DIM // 2)
    k_scale = RET_QK_DIM ** -0.5
    qk = RET_HEADS * RET_QK_DIM
    log_gamma = jnp.log(1.0 - 2.0 ** (-5.0 - jnp.arange(RET_HEADS, dtype=F32)))
    log_gamma = jnp.broadcast_to(log_gamma[:, None, None], (RET_HEADS, 1, LANE))

    meta_out = pl.pallas_call(
        _meta_kernel,
        out_shape=(jax.ShapeDtypeStruct((SUBLANE, LRU_WIDTH), F32),
                   jax.ShapeDtypeStruct((SUBLANE, LRU_WIDTH), F32),
                   jax.ShapeDtypeStruct((N_META, MLA_HEADS * LANE), BF16),
                   jax.ShapeDtypeStruct((MLA_HEADS * MLA_V, N_META), BF16),
                   jax.ShapeDtypeStruct((RET_HEADS, RET_QK_DIM, RET_V_DIM), F32)),
        scratch_shapes=[pltpu.VMEM((SUBLANE + N_META, LRU_WIDTH), F32),
                        pltpu.VMEM((N_META, LRU_WIDTH), F32),
                        pltpu.VMEM((N_META, LRU_WIDTH), F32)],
        compiler_params=pltpu.CompilerParams(vmem_limit_bytes=VMEM_LIMIT),
        name="meta_tokens",
    )(meta_tokens.astype(F32), w_in_ev, conv_w, conv_b, gate_w, b_a, b_x, lam, q_g, kv_g, w_uq, w_ukv_p, w_uv_t,
      *[t[:N_META] for t in mla_tabs], w_out_ev, *ln[0][:2], w1[0], w2[0], *ln[0][2:],
      w_in_od[:, qk:2 * qk], w_in_od[:, 2 * qk:2 * qk + RET_HEADS * RET_V_DIM],
      cos1[:N_META] * k_scale, sin1[:N_META] * k_scale, log_gamma)
    rec_tail, h_tail, k_meta, vt_meta, s_meta = meta_out

    ts = SEQ_TILE
    nt = seq // ts
    tab_spec = pl.BlockSpec((ts, LANE), lambda b, t: (t, 0))
    tab_t_spec = pl.BlockSpec((MLA_ROPE // 2, ts), lambda b, t: (0, t))
    seq_spec = lambda w: pl.BlockSpec((1, ts, w), lambda b, t: (b, t, 0))
    seq_t_spec = lambda w: pl.BlockSpec((1, 1, w, ts), lambda b, t: (b, t, 0, 0))
    y_rec, qt0, k0, vt0 = pl.pallas_call(
        _seq0_kernel,
        grid=(bsz, nt),
        in_specs=[seq_spec(D_MODEL), _const_spec(w_in_ev.shape), _const_spec(conv_w.shape),
                  _const_spec(conv_b.shape), _const_spec(gate_w.shape), _const_spec(b_a.shape),
                  _const_spec(b_x.shape), _const_spec(lam.shape), _const_spec(q_g.shape),
                  _const_spec(kv_g.shape), _const_spec(w_uq_t.shape), _const_spec(w_uk.shape),
                  _const_spec(w_uv_t.shape), tab_spec, tab_spec, tab_spec, tab_t_spec, tab_t_spec,
                  _const_spec(rec_tail.shape), _const_spec(h_tail.shape)],
        out_specs=[seq_spec(LRU_WIDTH), seq_t_spec(MLA_HEADS * LANE), seq_spec(MLA_HEADS * LANE),
                   seq_t_spec(MLA_HEADS * MLA_V)],
        out_shape=(jax.ShapeDtypeStruct((bsz, seq, LRU_WIDTH), BF16),
                   jax.ShapeDtypeStruct((bsz, nt, MLA_HEADS * LANE, ts), BF16),
                   jax.ShapeDtypeStruct((bsz, seq, MLA_HEADS * LANE), BF16),
                   jax.ShapeDtypeStruct((bsz, nt, MLA_HEADS * MLA_V, ts), BF16)),
        scratch_shapes=[pltpu.VMEM((SUBLANE + ts // SEQ_SPLIT, LRU_WIDTH), F32),
                        pltpu.VMEM((ts, LRU_WIDTH), F32),
                        pltpu.VMEM((ts, LRU_WIDTH), F32),
                        pltpu.VMEM((ts, LRU_WIDTH), F32),
                        pltpu.VMEM((SUBLANE, LRU_WIDTH), F32)],
        compiler_params=_params("parallel", "arbitrary"),
        name="seq0_mixer_proj",
    )(x, w_in_ev, conv_w, conv_b, gate_w, b_a, b_x, lam, q_g, kv_g, w_uq_t, w_uk, w_uv_t,
      *[t[N_META:] for t in mla_tabs], cos0_t[:, N_META:], sin0_t[:, N_META:], rec_tail, h_tail)

    tq = ATT_TILE
    assert tq == ts
    y_att = pl.pallas_call(
        _attn_kernel,
        grid=(bsz, MLA_HEADS // 2, seq // tq),
        in_specs=[pl.BlockSpec((1, 1, 2 * LANE, tq), lambda b, j, i: (b, i, j, 0)),
                  pl.BlockSpec((1, seq, 2 * LANE), lambda b, j, i: (b, 0, j)),
                  pl.BlockSpec((1, nt, LANE, tq), lambda b, j, i: (b, 0, j, 0)),
                  pl.BlockSpec((N_META, 2 * LANE), lambda b, j, i: (0, j)),
                  pl.BlockSpec((LANE, N_META), lambda b, j, i: (j, 0))],
        out_specs=pl.BlockSpec((1, tq, LANE), lambda b, j, i: (b, i, j)),
        out_shape=jax.ShapeDtypeStruct((bsz, seq, MLA_HEADS * MLA_V), BF16),
        scratch_shapes=[pltpu.VMEM((2, 1, tq), F32), pltpu.VMEM((2, 1, tq), F32),
                        pltpu.VMEM((2, MLA_V, tq), F32)],
        compiler_params=_params("parallel", "parallel", "arbitrary"),
        name="mla_attention",
    )(qt0, k0, vt0, k_meta, vt_meta)

    m = bsz * seq
    h1 = _mix_mlp_call("layer0_out_mlp", x.reshape(m, D_MODEL),
                       [y_rec.reshape(m, -1), y_att.reshape(m, -1)], w_out_ev,
                       ln[0][0], ln[0][1], w1[0], w2[0], ln[0][2], ln[0][3])

    tm = ROW_TILE
    per_seq = seq // tm
    row_spec = lambda w: pl.BlockSpec((tm, w), lambda i: (i, 0))
    rope_spec = pl.BlockSpec((tm, RET_QK_DIM // 2), lambda i: (i % per_seq, 0))
    mixw = RET_HEADS * RET_V_DIM
    q1, k1, v1, g1 = pl.pallas_call(
        _ret_proj_kernel,
        grid=(m // tm,),
        in_specs=[row_spec(D_MODEL), _const_spec(w_in_od.shape), rope_spec, rope_spec],
        out_specs=[row_spec(qk), row_spec(qk), row_spec(mixw), row_spec(mixw)],
        out_shape=(jax.ShapeDtypeStruct((m, qk), BF16), jax.ShapeDtypeStruct((m, qk), BF16),
                   jax.ShapeDtypeStruct((m, mixw), BF16), jax.ShapeDtypeStruct((m, mixw), BF16)),
        compiler_params=_params("parallel"),
        name="layer1_in_proj",
    )(h1, w_in_od, cos1[N_META:], sin1[N_META:])

    head_spec = lambda w: pl.BlockSpec((1, seq, w), lambda b, h: (b, 0, h))
    y_ret = pl.pallas_call(
        _retention_kernel,
        grid=(bsz, RET_HEADS),
        in_specs=[head_spec(RET_QK_DIM), head_spec(RET_QK_DIM), head_spec(RET_V_DIM), head_spec(RET_V_DIM),
                  pl.BlockSpec((1, RET_QK_DIM, RET_V_DIM), lambda b, h: (h, 0, 0)),
                  pl.BlockSpec((1, 1, LANE), lambda b, h: (h, 0, 0))],
        out_specs=head_spec(RET_V_DIM),
        out_shape=jax.ShapeDtypeStruct((bsz, seq, mixw), BF16),
        scratch_shapes=[pltpu.VMEM((RET_QK_DIM, RET_V_DIM), F32)],
        compiler_params=_params("parallel", "parallel"),
        name="retention",
    )(q1.reshape(bsz, seq, qk), k1.reshape(bsz, seq, qk), v1.reshape(bsz, seq, mixw),
      g1.reshape(bsz, seq, mixw), s_meta, log_gamma)

    out = _mix_mlp_call("layer1_out_mlp", h1, [y_ret.reshape(m, mixw)], w_out_od,
                        ln[1][0], ln[1][1], w1[1], w2[1], ln[1][2], ln[1][3])
    return out.reshape(bsz, seq, D_MODEL)
```

```python
import functools
import math

import jax
import jax.numpy as jnp
import numpy as np
from jax import lax
from jax.experimental import pallas as pl
from jax.experimental.pallas import tpu as pltpu

D_MODEL = 1024
N_META = 16
LRU_WIDTH = 512
LRU_HEADS = 4
LRU_HEAD_DIM = 128
CONV_WIDTH = 4
LRU_C = 8.0
MLA_HEADS = 8
MLA_NOPE = 64
MLA_ROPE = 32
MLA_V = 64
MLA_Q_RANK = 256
MLA_KV_RANK = 128
RET_HEADS = 4
RET_QK_DIM = 256
RET_V_DIM = 512
D_FF = 4096
ROPE_BASE = 10000.0
DN_ALPHA = 4.0 ** 0.25
EPS = 1e-5
NEG_INF = -1e30

LANE = 128
SUBLANE = 8
VMEM_LIMIT = 56 * 1024 * 1024

BF16 = jnp.bfloat16
F32 = jnp.float32

SEQ_TILE = 512
SEQ_SPLIT = 2
ATT_TILE = 512
ATT_SPLIT = 2
ATT_ONES = 16
ROW_TILE = 512
ROW_SPLIT = 2
FF_TILE = 1024
RET_CHUNK = 256


def _dot(a, b):
    return jnp.dot(a, b, preferred_element_type=F32)


def _dot_nt(a, b):
    return lax.dot_general(a, b, (((1,), (1,)), ((), ())), preferred_element_type=F32)


def _dot_tn(a, b):
    return lax.dot_general(a, b, (((0,), (0,)), ((), ())), preferred_element_type=F32)


def _layernorm(x, g, b):
    mu = jnp.mean(x, axis=-1, keepdims=True)
    xc = x - mu
    var = jnp.mean(xc * xc, axis=-1, keepdims=True)
    return xc * lax.rsqrt(var + EPS) * g + b


def _rmsnorm(x, g):
    return x * lax.rsqrt(jnp.mean(x * x, axis=-1, keepdims=True) + EPS) * g


def _rope_mla(x, c, s_up, s_dn):
    return x * c + pltpu.roll(x, MLA_ROPE // 2, 1) * s_up + pltpu.roll(x, LANE - MLA_ROPE // 2, 1) * s_dn


def _lru_gates(xc, gate_w_ref, b_a, b_x, sp_lambda):
    rs, is_ = [], []
    for h in range(LRU_HEADS):
        g = _dot(xc[:, h * LRU_HEAD_DIM:(h + 1) * LRU_HEAD_DIM].astype(BF16), gate_w_ref[h])
        rs.append(g[:, :LRU_HEAD_DIM])
        is_.append(g[:, LRU_HEAD_DIM:])
    r = jax.nn.sigmoid(jnp.concatenate(rs, axis=1) + b_a)
    i = jax.nn.sigmoid(jnp.concatenate(is_, axis=1) + b_x)
    log_a = -LRU_C * r * sp_lambda
    a = jnp.exp(log_a)
    y = 1.0 - a * a
    mult = jnp.where(y > 0.0, y * lax.rsqrt(y), 0.0)
    return a, mult * (i * xc)


def _scan8(a, b):
    row = lax.broadcasted_iota(jnp.int32, a.shape, 0)
    for k in (1, 2, 4):
        keep = row >= k
        a_prev = jnp.where(keep, pltpu.roll(a, k, 0), 1.0)
        b_prev = jnp.where(keep, pltpu.roll(b, k, 0), 0.0)
        b = a * b_prev + b
        a = a * a_prev
    return a, b


def _lru_scan(a_ref, b_ref, h0, rows):
    def body(g, h_prev):
        sl = pl.ds(pl.multiple_of(g * SUBLANE, SUBLANE), SUBLANE)
        a_c, b_c = _scan8(a_ref[sl, :], b_ref[sl, :])
        h = a_c * h_prev + b_c
        b_ref[sl, :] = h
        return h[SUBLANE - 1:SUBLANE, :]
    return lax.fori_loop(0, rows // SUBLANE, body, h0, unroll=4)


def _mla_project(qlat, kvlat, kpe, q_g, kv_g, w_uq_ref, w_ukv_ref, rope_c, rope_up, rope_dn):
    scale = (MLA_NOPE + MLA_ROPE) ** -0.5
    q_all = _dot(_rmsnorm(qlat, q_g).astype(BF16), w_uq_ref[...])
    kv_all = _dot(_rmsnorm(kvlat, kv_g).astype(BF16), w_ukv_ref[...])
    kpe_r = _rope_mla(kpe, rope_c, rope_up, rope_dn)
    qs, ks = [], []
    for h in range(MLA_HEADS):
        sl = slice(h * LANE, (h + 1) * LANE)
        qs.append((_rope_mla(q_all[:, sl], rope_c, rope_up, rope_dn) * scale).astype(BF16))
        ks.append((kv_all[:, sl] + kpe_r).astype(BF16))
    v = kv_all[:, MLA_HEADS * LANE:].astype(BF16)
    return jnp.concatenate(qs, axis=1), jnp.concatenate(ks, axis=1), v


def _ffn(h1, w1_ref, w2_ref):
    h1b = h1.astype(BF16)
    f = None
    for c in range(D_FF // FF_TILE):
        a = _dot(h1b, w1_ref[:, c * FF_TILE:(c + 1) * FF_TILE])
        a = jnp.maximum(a, 0.0)
        part = _dot((a * a).astype(BF16), w2_ref[c * FF_TILE:(c + 1) * FF_TILE, :])
        f = part if f is None else f + part
    return f


def _mlp_block(h_in, mix, ln1_g, ln1_b, w1_ref, w2_ref, ln2_g, ln2_b):
    h1 = _layernorm(DN_ALPHA * h_in + mix, ln1_g, ln1_b)
    return _layernorm(DN_ALPHA * h1 + _ffn(h1, w1_ref, w2_ref), ln2_g, ln2_b)


def _meta_kernel(meta_ref, w_in_ref, conv_w_ref, conv_b_ref, gate_w_ref, b_a_ref, b_x_ref, lam_ref,
                 q_g_ref, kv_g_ref, w_uq_ref, w_ukv_ref, w_uvt_ref, rope_c_ref, rope_up_ref, rope_dn_ref,
                 w_out_ref, ln1_g_ref, ln1_b_ref, w1_ref, w2_ref, ln2_g_ref, ln2_b_ref,
                 w_k_ref, w_v_ref, cos1_ref, sin1_ref, lg_ref,
                 rec_tail_ref, h_tail_ref, k_meta_ref, vt_meta_ref, s_meta_ref,
                 conv_scr, a_scr, b_scr):
    n = N_META
    x = meta_ref[...]
    p = _dot(x.astype(BF16), w_in_ref[...])
    gate, rec = p[:, :LRU_WIDTH], p[:, LRU_WIDTH:2 * LRU_WIDTH]
    conv_scr[0:SUBLANE, :] = jnp.zeros((SUBLANE, LRU_WIDTH), F32)
    conv_scr[SUBLANE:SUBLANE + n, :] = rec
    cw = conv_w_ref[...]
    xc = conv_b_ref[...] + cw[3:4, :] * rec
    for j in range(CONV_WIDTH - 1):
        off = SUBLANE - (CONV_WIDTH - 1) + j
        xc = xc + cw[j:j + 1, :] * conv_scr[off:off + n, :]
    sp_lambda = jax.nn.softplus(-lam_ref[...])
    a, b = _lru_gates(xc, gate_w_ref, b_a_ref[...], b_x_ref[...], sp_lambda)
    a_scr[...] = a
    b_scr[...] = b
    _lru_scan(a_scr, b_scr, jnp.zeros((1, LRU_WIDTH), F32), n)
    h = b_scr[...]
    y_rec = (h * jax.nn.gelu(gate)).astype(BF16)
    rec_tail_ref[...] = rec[n - SUBLANE:, :]
    h_tail_ref[...] = h[n - SUBLANE:, :]

    off = 2 * LRU_WIDTH
    q, k, v = _mla_project(p[:, off:off + MLA_Q_RANK],
                           p[:, off + MLA_Q_RANK:off + MLA_Q_RANK + MLA_KV_RANK],
                           p[:, off + MLA_Q_RANK + MLA_KV_RANK:],
                           q_g_ref[...], kv_g_ref[...], w_uq_ref, w_ukv_ref,
                           rope_c_ref[...], rope_up_ref[...], rope_dn_ref[...])
    k_meta_ref[...] = k
    kvn = _rmsnorm(p[:, off + MLA_Q_RANK:off + MLA_Q_RANK + MLA_KV_RANK], kv_g_ref[...]).astype(BF16)
    vt_meta_ref[...] = _dot_nt(w_uvt_ref[...], kvn).astype(BF16)
    causal = (lax.broadcasted_iota(jnp.int32, (n, n), 1) <= lax.broadcasted_iota(jnp.int32, (n, n), 0))
    outs = []
    for hh in range(MLA_HEADS):
        sl = slice(hh * LANE, (hh + 1) * LANE)
        s = jnp.where(causal, _dot_nt(q[:, sl], k[:, sl]), NEG_INF)
        e = jnp.exp(s - jnp.max(s, axis=-1, keepdims=True))
        pr = e / jnp.sum(e, axis=-1, keepdims=True)
        outs.append(_dot(pr.astype(BF16), v[:, hh * MLA_V:(hh + 1) * MLA_V]))
    y_att = jnp.concatenate(outs, axis=1).astype(BF16)
    mix = _dot(y_rec, w_out_ref[0:LRU_WIDTH, :]) + _dot(y_att, w_out_ref[LRU_WIDTH:, :])
    h2 = _mlp_block(x, mix, ln1_g_ref[...], ln1_b_ref[...], w1_ref, w2_ref, ln2_g_ref[...], ln2_b_ref[...])

    h2b = h2.astype(BF16)
    kk = _dot(h2b, w_k_ref[...])
    vv = _dot(h2b, w_v_ref[...]).astype(BF16)
    cos, sin = cos1_ref[...], sin1_ref[...]
    idx = lax.broadcasted_iota(jnp.int32, (n, 1), 0).astype(F32)
    half = RET_QK_DIM // 2
    for hh in range(RET_HEADS):
        log_gamma = lg_ref[hh][:, 0:1]
        k1 = kk[:, hh * RET_QK_DIM:hh * RET_QK_DIM + half]
        k2 = kk[:, hh * RET_QK_DIM + half:(hh + 1) * RET_QK_DIM]
        kr = jnp.concatenate([k1 * cos - k2 * sin, k1 * sin + k2 * cos], axis=1).astype(BF16)
        k_dec = jnp.exp(log_gamma * (n - 1.0 - idx))
        kd = (kr.astype(F32) * k_dec).astype(BF16)
        s_meta_ref[hh] = _dot_tn(kd, vv[:, hh * RET_V_DIM:(hh + 1) * RET_V_DIM])


def _seq0_kernel(x_ref, w_in_ref, conv_w_ref, conv_b_ref, gate_w_ref, b_a_ref, b_x_ref, lam_ref,
                 q_g_ref, kv_g_ref, w_uqt_ref, w_uk_ref, w_uvt_ref, rope_c_ref, rope_up_ref, rope_dn_ref,
                 cos_t_ref, sin_t_ref, rec_tail_ref, h_tail_ref,
                 y_rec_ref, qt_ref, k_ref, vt_ref,
                 conv_scr, a_scr, b_scr, g_scr, h_scr):
    ts = SEQ_TILE
    t = pl.program_id(1)

    @pl.when(t == 0)
    def _():
        conv_scr[0:SUBLANE, :] = rec_tail_ref[...]
        h_scr[...] = h_tail_ref[...]

    sub = ts // SEQ_SPLIT
    spans = [slice(i * sub, (i + 1) * sub) for i in range(SEQ_SPLIT)]
    ps = [_dot(x_ref[0, rows, :].astype(BF16), w_in_ref[...]) for rows in spans]
    cw = conv_w_ref[...]
    sp_lambda = jax.nn.softplus(-lam_ref[...])
    scale = (MLA_NOPE + MLA_ROPE) ** -0.5 * math.log2(math.e)
    hr = MLA_ROPE // 2
    off = 2 * LRU_WIDTH
    for rows, p in zip(spans, ps):
        gate, rec = p[:, :LRU_WIDTH], p[:, LRU_WIDTH:2 * LRU_WIDTH]
        conv_scr[SUBLANE:SUBLANE + sub, :] = rec
        xc = conv_b_ref[...] + cw[3:4, :] * rec
        for j in range(CONV_WIDTH - 1):
            o = SUBLANE - (CONV_WIDTH - 1) + j
            xc = xc + cw[j:j + 1, :] * conv_scr[o:o + sub, :]
        conv_scr[0:SUBLANE, :] = rec[sub - SUBLANE:, :]
        a, b = _lru_gates(xc, gate_w_ref, b_a_ref[...], b_x_ref[...], sp_lambda)
        a_scr[rows, :] = a
        b_scr[rows, :] = b
        g_scr[rows, :] = jax.nn.gelu(gate)

        qn = _rmsnorm(p[:, off:off + MLA_Q_RANK], q_g_ref[...]).astype(BF16)
        q_t = _dot_nt(w_uqt_ref[...], qn)
        cos_t, sin_t = cos_t_ref[:, rows] * scale, sin_t_ref[:, rows] * scale
        for h in range(MLA_HEADS):
            base = h * LANE
            x1 = q_t[base + MLA_NOPE:base + MLA_NOPE + hr, :]
            x2 = q_t[base + MLA_NOPE + hr:base + MLA_NOPE + MLA_ROPE, :]
            qt_ref[0, 0, base:base + MLA_NOPE, rows] = (q_t[base:base + MLA_NOPE, :] * scale).astype(BF16)
            qt_ref[0, 0, base + MLA_NOPE:base + MLA_NOPE + hr, rows] = (x1 * cos_t - x2 * sin_t).astype(BF16)
            qt_ref[0, 0, base + MLA_NOPE + hr:base + MLA_NOPE + MLA_ROPE, rows] = (
                x1 * sin_t + x2 * cos_t).astype(BF16)
            qt_ref[0, 0, base + MLA_NOPE + MLA_ROPE:base + LANE, rows] = jnp.zeros(
                (LANE - MLA_NOPE - MLA_ROPE, sub), BF16)
        kvn = _rmsnorm(p[:, off + MLA_Q_RANK:off + MLA_Q_RANK + MLA_KV_RANK], kv_g_ref[...]).astype(BF16)
        k_nope = _dot(kvn, w_uk_ref[...])
        kpe_r = _rope_mla(p[:, off + MLA_Q_RANK + MLA_KV_RANK:],
                          rope_c_ref[rows, :], rope_up_ref[rows, :], rope_dn_ref[rows, :])
        for h in range(MLA_HEADS):
            sl = slice(h * LANE, (h + 1) * LANE)
            k_ref[0, rows, sl] = (k_nope[:, sl] + kpe_r).astype(BF16)
        vt_ref[0, 0, :, rows] = _dot_nt(w_uvt_ref[...], kvn).astype(BF16)

    h_last = _lru_scan(a_scr, b_scr, h_scr[SUBLANE - 1:SUBLANE, :], ts)
    h_scr[SUBLANE - 1:SUBLANE, :] = h_last
    y_rec_ref[0] = (b_scr[...] * g_scr[...]).astype(BF16)


def _attn_kernel(qt_ref, k_ref, vt_ref, k_meta_ref, vt_meta_ref, o_ref, m_scr, acc_scr):
    tq = ATT_TILE
    tw = tq // ATT_SPLIT
    qi = pl.program_id(2)
    chains = [(hh, slice(hh * LANE, (hh + 1) * LANE), slice(hh * MLA_V, (hh + 1) * MLA_V),
               part, slice(part * tw, (part + 1) * tw))
              for hh in range(2) for part in range(ATT_SPLIT)]
    def with_ones(v_t):
        return jnp.concatenate([v_t, jnp.ones((ATT_ONES, v_t.shape[1]), BF16)], axis=0)

    def meta_scores():
        return tuple(_dot(k_meta_ref[:, sl], qt_ref[0, 0, sl, cols]) for _, sl, _, _, cols in chains)

    def meta_softmax_pv(scores):
        for (hh, sl, vrows, part, cols), s in zip(chains, scores):
            m = jnp.max(s, axis=0, keepdims=True)
            e = jnp.exp2(s - m)
            m_scr[hh, :, cols] = m
            acc_scr[hh, :, cols] = _dot(with_ones(vt_meta_ref[vrows, :]), e.astype(BF16))

    def n_keys(part, diagonal):
        return (part + 1) * tw if diagonal else tq

    def scores_for(kj, diagonal):
        return tuple(_dot(k_ref[0, kj * tq:kj * tq + n_keys(part, diagonal), sl], qt_ref[0, 0, sl, cols])
                     for _, sl, _, part, cols in chains)

    def softmax_pv(kj, scores, diagonal):
        for (hh, sl, vrows, part, cols), s in zip(chains, scores):
            nk = n_keys(part, diagonal)
            if diagonal:
                key = lax.broadcasted_iota(jnp.int32, (nk, tw), 0)
                qry = lax.broadcasted_iota(jnp.int32, (nk, tw), 1) + part * tw
                s = jnp.where(key <= qry, s, NEG_INF)
            m_old = m_scr[hh, :, cols]
            m_new = jnp.maximum(m_old, jnp.max(s, axis=0, keepdims=True))
            alpha = jnp.exp2(m_old - m_new)
            e = jnp.exp2(s - m_new)
            m_scr[hh, :, cols] = m_new
            acc_scr[hh, :, cols] = alpha * acc_scr[hh, :, cols] + _dot(
                with_ones(vt_ref[0, kj, vrows, 0:nk]), e.astype(BF16))

    def run(n_full):
        s_meta = meta_scores()
        scores = scores_for(0, n_full == 0)
        meta_softmax_pv(s_meta)
        for kj in range(n_full):
            nxt = scores_for(kj + 1, kj + 1 == n_full)
            softmax_pv(kj, scores, False)
            scores = nxt
        softmax_pv(n_full, scores, True)

    for n_full in range(k_ref.shape[1] // tq):
        pl.when(qi == n_full)(functools.partial(run, n_full))
    out_t = jnp.concatenate([acc_scr[hh, 0:MLA_V, :] / acc_scr[hh, MLA_V:MLA_V + 1, :] for hh in range(2)],
                            axis=0)
    o_ref[0] = out_t.T.astype(BF16)


def _mix_mlp_kernel(*refs, n_mix):
    h_ref = refs[0]
    y_refs = refs[1:1 + n_mix]
    w_out_ref, ln1_g, ln1_b, w1_ref, w2_ref, ln2_g, ln2_b, o_ref = refs[1 + n_mix:]
    sub = h_ref.shape[0] // ROW_SPLIT
    spans = [slice(s * sub, (s + 1) * sub) for s in range(ROW_SPLIT)]
    mixes = []
    for rows in spans:
        mix = None
        row = 0
        for y_ref in y_refs:
            width = y_ref.shape[-1]
            part = _dot(y_ref[rows, :], w_out_ref[row:row + width, :])
            mix = part if mix is None else mix + part
            row += width
        mixes.append(mix)
    h1s = [_layernorm(DN_ALPHA * h_ref[rows, :] + mix, ln1_g[...], ln1_b[...]) for rows, mix in zip(spans, mixes)]
    fs = [_ffn(h1, w1_ref, w2_ref) for h1 in h1s]
    for rows, h1, f in zip(spans, h1s, fs):
        o_ref[rows, :] = _layernorm(DN_ALPHA * h1 + f, ln2_g[...], ln2_b[...])


def _ret_proj_kernel(h_ref, w_ref, cos_ref, sin_ref, q_ref, k_ref, v_ref, g_ref):
    hb = h_ref[...].astype(BF16)
    qk = RET_HEADS * RET_QK_DIM
    half = RET_QK_DIM // 2
    k_scale = RET_QK_DIM ** -0.5
    for out_ref, base, scale in ((q_ref, 0, None), (k_ref, qk, k_scale)):
        cos, sin = cos_ref[...], sin_ref[...]
        if scale is not None:
            cos, sin = cos * scale, sin * scale
        pr = _dot(hb, w_ref[:, base:base + qk])
        parts = []
        for hh in range(RET_HEADS):
            x1 = pr[:, hh * RET_QK_DIM:hh * RET_QK_DIM + half]
            x2 = pr[:, hh * RET_QK_DIM + half:(hh + 1) * RET_QK_DIM]
            parts.append((x1 * cos - x2 * sin).astype(BF16))
            parts.append((x1 * sin + x2 * cos).astype(BF16))
        out_ref[...] = jnp.concatenate(parts, axis=1)
    mixw = RET_HEADS * RET_V_DIM
    v_ref[...] = _dot(hb, w_ref[:, 2 * qk:2 * qk + mixw]).astype(BF16)
    g_ref[...] = _dot(hb, w_ref[:, 2 * qk + mixw:]).astype(BF16)


def _retention_kernel(q_ref, k_ref, v_ref, g_ref, s0_ref, lg_ref, y_ref, s_scr):
    c = RET_CHUNK
    log_gamma = lg_ref[0][:, 0:1]
    ii = lax.broadcasted_iota(jnp.int32, (c, c), 0)
    jj = lax.broadcasted_iota(jnp.int32, (c, c), 1)
    diff = (ii - jj).astype(F32)
    decay = jnp.where(diff >= 0, jnp.exp(log_gamma * jnp.maximum(diff, 0.0)), 0.0)
    idx = lax.broadcasted_iota(jnp.int32, (c, 1), 0).astype(F32)
    q_decay = jnp.exp(log_gamma * (idx + 1.0))
    k_decay = jnp.exp(log_gamma * (c - 1.0 - idx))
    chunk_decay = jnp.exp(log_gamma * c)
    s_scr[...] = s0_ref[0]

    def recur(ci):
        rows = slice(ci * c, (ci + 1) * c)
        q = q_ref[0, rows, :]
        k = k_ref[0, rows, :]
        v = v_ref[0, rows, :]
        s_prev = s_scr[...]
        scores = _dot_nt(q, k) * decay
        o = _dot(scores.astype(BF16), v) + q_decay * _dot(q, s_prev.astype(BF16))
        kd = (k.astype(F32) * k_decay).astype(BF16)
        s_scr[...] = chunk_decay * s_prev + _dot_tn(kd, v)
        return o

    def finish(ci, o):
        rows = slice(ci * c, (ci + 1) * c)
        o = o * lax.rsqrt(jnp.mean(o * o, axis=-1, keepdims=True) + EPS)
        y_ref[0, rows, :] = (jax.nn.silu(g_ref[0, rows, :].astype(F32)) * o).astype(BF16)

    o_prev = None
    for ci in range(q_ref.shape[1] // c):
        o = recur(ci)
        if o_prev is not None:
            finish(ci - 1, o_prev)
        o_prev = o
    finish(q_ref.shape[1] // c - 1, o_prev)


def _const_spec(shape):
    zeros = (0,) * len(shape)
    return pl.BlockSpec(shape, lambda *_: zeros, pipeline_mode=pl.Buffered(1))


def _params(*semantics):
    return pltpu.CompilerParams(dimension_semantics=semantics, vmem_limit_bytes=VMEM_LIMIT)


def _rope_tables(positions, half):
    inv = ROPE_BASE ** (-jnp.arange(half, dtype=F32) / half)
    ang = positions.astype(F32)[:, None] * inv[None, :]
    return jnp.cos(ang), jnp.sin(ang)


def _mla_rope_tables(positions):
    cos, sin = _rope_tables(positions, MLA_ROPE // 2)
    n = positions.shape[0]
    h = MLA_ROPE // 2
    ones = jnp.ones((n, MLA_NOPE), F32)
    zeros = lambda w: jnp.zeros((n, w), F32)
    c = jnp.concatenate([ones, cos, cos, zeros(LANE - MLA_NOPE - MLA_ROPE)], axis=1)
    s_up = jnp.concatenate([zeros(MLA_NOPE + h), sin, zeros(LANE - MLA_NOPE - MLA_ROPE)], axis=1)
    s_dn = jnp.concatenate([zeros(MLA_NOPE), -sin, zeros(LANE - MLA_NOPE - h)], axis=1)
    return c, s_up, s_dn


def _mix_mlp_call(name, h, ys, w_out, ln1_g, ln1_b, w1, w2, ln2_g, ln2_b):
    m = h.shape[0]
    tm = ROW_TILE
    row_spec = lambda w: pl.BlockSpec((tm, w), lambda i: (i, 0))
    vec = _const_spec((1, D_MODEL))
    return pl.pallas_call(
        functools.partial(_mix_mlp_kernel, n_mix=len(ys)),
        grid=(m // tm,),
        in_specs=[row_spec(D_MODEL)] + [row_spec(y.shape[1]) for y in ys] + [
            _const_spec(w_out.shape), vec, vec, _const_spec(w1.shape), _const_spec(w2.shape), vec, vec],
        out_specs=row_spec(D_MODEL),
        out_shape=jax.ShapeDtypeStruct((m, D_MODEL), F32),
        compiler_params=_params("parallel"),
        name=name,
    )(h, *ys, w_out, ln1_g, ln1_b, w1, w2, ln2_g, ln2_b)


def kernel(x, meta_tokens, ev_w_in, ev_conv_w, ev_conv_b, ev_w_rg_a, ev_b_rg_a, ev_w_rg_x, ev_b_rg_x,
           ev_lru_lambda, ev_q_norm_g, ev_w_uq, ev_kv_norm_g, ev_w_ukv, ev_w_out, od_w_in, od_w_out,
           ln_mix_g, ln_mix_b, mlp_w1, mlp_w2, ln_mlp_g, ln_mlp_b):
    bsz, seq, _ = x.shape
    row = lambda v: v.reshape(1, -1).astype(F32)

    w_in0 = ev_w_in[0]
    lat0 = 2 * LRU_WIDTH
    kpe0 = lat0 + MLA_Q_RANK + MLA_KV_RANK
    w_kpe = jnp.zeros((D_MODEL, LANE), F32).at[:, MLA_NOPE:MLA_NOPE + MLA_ROPE].set(w_in0[:, kpe0:])
    w_in_ev = jnp.concatenate([w_in0[:, :kpe0], w_kpe], axis=1).astype(BF16)
    gate_w = jnp.concatenate([ev_w_rg_a[0], ev_w_rg_x[0]], axis=2).astype(BF16)
    w_uq = ev_w_uq[0].reshape(MLA_Q_RANK, MLA_HEADS, MLA_NOPE + MLA_ROPE)
    w_uq = jnp.pad(w_uq, ((0, 0), (0, 0), (0, LANE - MLA_NOPE - MLA_ROPE)))
    w_uq = w_uq.reshape(MLA_Q_RANK, MLA_HEADS * LANE).astype(BF16)
    w_ukv = ev_w_ukv[0].reshape(MLA_KV_RANK, MLA_HEADS, MLA_NOPE + MLA_V)
    w_uk = jnp.pad(w_ukv[:, :, :MLA_NOPE], ((0, 0), (0, 0), (0, LANE - MLA_NOPE)))
    w_uk = w_uk.reshape(MLA_KV_RANK, MLA_HEADS * LANE)
    w_uv = w_ukv[:, :, MLA_NOPE:].reshape(MLA_KV_RANK, MLA_HEADS * MLA_V)
    w_ukv_p = jnp.concatenate([w_uk, w_uv], axis=1).astype(BF16)
    w_uq_t = w_uq.T
    w_uk = w_uk.astype(BF16)
    w_uv_t = w_uv.T.astype(BF16)
    w_out_ev = ev_w_out[0].astype(BF16)
    w_in_od = od_w_in[0].astype(BF16)
    w_out_od = od_w_out[0].astype(BF16)
    w1 = mlp_w1.astype(BF16)
    w2 = mlp_w2.astype(BF16)
    conv_w = ev_conv_w[0].astype(F32)
    conv_b, b_a, b_x, lam = row(ev_conv_b[0]), row(ev_b_rg_a[0]), row(ev_b_rg_x[0]), row(ev_lru_lambda[0])
    q_g, kv_g = row(ev_q_norm_g[0]), row(ev_kv_norm_g[0])
    ln = [(row(ln_mix_g[l]), row(ln_mix_b[l]), row(ln_mlp_g[l]), row(ln_mlp_b[l])) for l in range(2)]

    pos = jnp.arange(N_META + seq, dtype=jnp.int32)
    mla_tabs = _mla_rope_tables(pos)
    cos0, sin0 = _rope_tables(pos, MLA_ROPE // 2)
    cos0_t, sin0_t = cos0.T, sin0.T
    cos1, sin1 = _rope_tables(pos, RET_QK_DIM // 2)
    k_scale = RET_QK_DIM ** -0.5
    qk = RET_HEADS * RET_QK_DIM
    log_gamma = jnp.log(1.0 - 2.0 ** (-5.0 - jnp.arange(RET_HEADS, dtype=F32)))
    log_gamma = jnp.broadcast_to(log_gamma[:, None, None], (RET_HEADS, 1, LANE))

    meta_out = pl.pallas_call(
        _meta_kernel,
        out_shape=(jax.ShapeDtypeStruct((SUBLANE, LRU_WIDTH), F32),
                   jax.ShapeDtypeStruct((SUBLANE, LRU_WIDTH), F32),
                   jax.ShapeDtypeStruct((N_META, MLA_HEADS * LANE), BF16),
                   jax.ShapeDtypeStruct((MLA_HEADS * MLA_V, N_META), BF16),
                   jax.ShapeDtypeStruct((RET_HEADS, RET_QK_DIM, RET_V_DIM), F32)),
        scratch_shapes=[pltpu.VMEM((SUBLANE + N_META, LRU_WIDTH), F32),
                        pltpu.VMEM((N_META, LRU_WIDTH), F32),
                        pltpu.VMEM((N_META, LRU_WIDTH), F32)],
        compiler_params=pltpu.CompilerParams(vmem_limit_bytes=VMEM_LIMIT),
        name="meta_tokens",
    )(meta_tokens.astype(F32), w_in_ev, conv_w, conv_b, gate_w, b_a, b_x, lam, q_g, kv_g, w_uq, w_ukv_p, w_uv_t,
      *[t[:N_META] for t in mla_tabs], w_out_ev, *ln[0][:2], w1[0], w2[0], *ln[0][2:],
      w_in_od[:, qk:2 * qk], w_in_od[:, 2 * qk:2 * qk + RET_HEADS * RET_V_DIM],
      cos1[:N_META] * k_scale, sin1[:N_META] * k_scale, log_gamma)
    rec_tail, h_tail, k_meta, vt_meta, s_meta = meta_out

    ts = SEQ_TILE
    nt = seq // ts
    tab_spec = pl.BlockSpec((ts, LANE), lambda b, t: (t, 0))
    tab_t_spec = pl.BlockSpec((MLA_ROPE // 2, ts), lambda b, t: (0, t))
    seq_spec = lambda w: pl.BlockSpec((1, ts, w), lambda b, t: (b, t, 0))
    seq_t_spec = lambda w: pl.BlockSpec((1, 1, w, ts), lambda b, t: (b, t, 0, 0))
    y_rec, qt0, k0, vt0 = pl.pallas_call(
        _seq0_kernel,
        grid=(bsz, nt),
        in_specs=[seq_spec(D_MODEL), _const_spec(w_in_ev.shape), _const_spec(conv_w.shape),
                  _const_spec(conv_b.shape), _const_spec(gate_w.shape), _const_spec(b_a.shape),
                  _const_spec(b_x.shape), _const_spec(lam.shape), _const_spec(q_g.shape),
                  _const_spec(kv_g.shape), _const_spec(w_uq_t.shape), _const_spec(w_uk.shape),
                  _const_spec(w_uv_t.shape), tab_spec, tab_spec, tab_spec, tab_t_spec, tab_t_spec,
                  _const_spec(rec_tail.shape), _const_spec(h_tail.shape)],
        out_specs=[seq_spec(LRU_WIDTH), seq_t_spec(MLA_HEADS * LANE), seq_spec(MLA_HEADS * LANE),
                   seq_t_spec(MLA_HEADS * MLA_V)],
        out_shape=(jax.ShapeDtypeStruct((bsz, seq, LRU_WIDTH), BF16),
                   jax.ShapeDtypeStruct((bsz, nt, MLA_HEADS * LANE, ts), BF16),
                   jax.ShapeDtypeStruct((bsz, seq, MLA_HEADS * LANE), BF16),
                   jax.ShapeDtypeStruct((bsz, nt, MLA_HEADS * MLA_V, ts), BF16)),
        scratch_shapes=[pltpu.VMEM((SUBLANE + ts // SEQ_SPLIT, LRU_WIDTH), F32),
                        pltpu.VMEM((ts, LRU_WIDTH), F32),
                        pltpu.VMEM((ts, LRU_WIDTH), F32),
                        pltpu.VMEM((ts, LRU_WIDTH), F32),
                        pltpu.VMEM((SUBLANE, LRU_WIDTH), F32)],
        compiler_params=_params("parallel", "arbitrary"),
        name="seq0_mixer_proj",
    )(x, w_in_ev, conv_w, conv_b, gate_w, b_a, b_x, lam, q_g, kv_g, w_uq_t, w_uk, w_uv_t,
      *[t[N_META:] for t in mla_tabs], cos0_t[:, N_META:], sin0_t[:, N_META:], rec_tail, h_tail)

    tq = ATT_TILE
    assert tq == ts
    y_att = pl.pallas_call(
        _attn_kernel,
        grid=(bsz, MLA_HEADS // 2, seq // tq),
        in_specs=[pl.BlockSpec((1, 1, 2 * LANE, tq), lambda b, j, i: (b, i, j, 0)),
                  pl.BlockSpec((1, seq, 2 * LANE), lambda b, j, i: (b, 0, j)),
                  pl.BlockSpec((1, nt, LANE, tq), lambda b, j, i: (b, 0, j, 0)),
                  pl.BlockSpec((N_META, 2 * LANE), lambda b, j, i: (0, j)),
                  pl.BlockSpec((LANE, N_META), lambda b, j, i: (j, 0))],
        out_specs=pl.BlockSpec((1, tq, LANE), lambda b, j, i: (b, i, j)),
        out_shape=jax.ShapeDtypeStruct((bsz, seq, MLA_HEADS * MLA_V), BF16),
        scratch_shapes=[pltpu.VMEM((2, 1, tq), F32), pltpu.VMEM((2, MLA_V + ATT_ONES, tq), F32)],
        compiler_params=_params("parallel", "parallel", "arbitrary"),
        name="mla_attention",
    )(qt0, k0, vt0, k_meta, vt_meta)

    m = bsz * seq
    h1 = _mix_mlp_call("layer0_out_mlp", x.reshape(m, D_MODEL),
                       [y_rec.reshape(m, -1), y_att.reshape(m, -1)], w_out_ev,
                       ln[0][0], ln[0][1], w1[0], w2[0], ln[0][2], ln[0][3])

    tm = ROW_TILE
    per_seq = seq // tm
    row_spec = lambda w: pl.BlockSpec((tm, w), lambda i: (i, 0))
    rope_spec = pl.BlockSpec((tm, RET_QK_DIM // 2), lambda i: (i % per_seq, 0))
    mixw = RET_HEADS * RET_V_DIM
    q1, k1, v1, g1 = pl.pallas_call(
        _ret_proj_kernel,
        grid=(m // tm,),
        in_specs=[row_spec(D_MODEL), _const_spec(w_in_od.shape), rope_spec, rope_spec],
        out_specs=[row_spec(qk), row_spec(qk), row_spec(mixw), row_spec(mixw)],
        out_shape=(jax.ShapeDtypeStruct((m, qk), BF16), jax.ShapeDtypeStruct((m, qk), BF16),
                   jax.ShapeDtypeStruct((m, mixw), BF16), jax.ShapeDtypeStruct((m, mixw), BF16)),
        compiler_params=_params("parallel"),
        name="layer1_in_proj",
    )(h1, w_in_od, cos1[N_META:], sin1[N_META:])

    head_spec = lambda w: pl.BlockSpec((1, seq, w), lambda b, h: (b, 0, h))
    y_ret = pl.pallas_call(
        _retention_kernel,
        grid=(bsz, RET_HEADS),
        in_specs=[head_spec(RET_QK_DIM), head_spec(RET_QK_DIM), head_spec(RET_V_DIM), head_spec(RET_V_DIM),
                  pl.BlockSpec((1, RET_QK_DIM, RET_V_DIM), lambda b, h: (h, 0, 0)),
                  pl.BlockSpec((1, 1, LANE), lambda b, h: (h, 0, 0))],
        out_specs=head_spec(RET_V_DIM),
        out_shape=jax.ShapeDtypeStruct((bsz, seq, mixw), BF16),
        scratch_shapes=[pltpu.VMEM((RET_QK_DIM, RET_V_DIM), F32)],
        compiler_params=_params("parallel", "parallel"),
        name="retention",
    )(q1.reshape(bsz, seq, qk), k1.reshape(bsz, seq, qk), v1.reshape(bsz, seq, mixw),
      g1.reshape(bsz, seq, mixw), s_meta, log_gamma)

    out = _mix_mlp_call("layer1_out_mlp", h1, [y_ret.reshape(m, mixw)], w_out_od,
                        ln[1][0], ln[1][1], w1[1], w2[1], ln[1][2], ln[1][3])
    return out.reshape(bsz, seq, D_MODEL)
```

```python
import functools
import math

import jax
import jax.numpy as jnp
import numpy as np
from jax import lax
from jax.experimental import pallas as pl
from jax.experimental.pallas import tpu as pltpu

D_MODEL = 1024
N_META = 16
LRU_WIDTH = 512
LRU_HEADS = 4
LRU_HEAD_DIM = 128
CONV_WIDTH = 4
LRU_C = 8.0
MLA_HEADS = 8
MLA_NOPE = 64
MLA_ROPE = 32
MLA_V = 64
MLA_Q_RANK = 256
MLA_KV_RANK = 128
RET_HEADS = 4
RET_QK_DIM = 256
RET_V_DIM = 512
D_FF = 4096
ROPE_BASE = 10000.0
DN_ALPHA = 4.0 ** 0.25
EPS = 1e-5
NEG_INF = -1e30

LANE = 128
SUBLANE = 8
VMEM_LIMIT = 56 * 1024 * 1024

BF16 = jnp.bfloat16
F32 = jnp.float32

SEQ_TILE = 512
SEQ_SPLIT = 2
ATT_TILE = 512
ATT_SPLIT = 2
ATT_ONES = 16
ROW_TILE = 512
ROW_SPLIT = 2
FF_TILE = 1024
RET_CHUNK = 256


def _dot(a, b):
    return jnp.dot(a, b, preferred_element_type=F32)


def _dot_nt(a, b):
    return lax.dot_general(a, b, (((1,), (1,)), ((), ())), preferred_element_type=F32)


def _dot_tn(a, b):
    return lax.dot_general(a, b, (((0,), (0,)), ((), ())), preferred_element_type=F32)


def _layernorm(x, g, b):
    mu = jnp.mean(x, axis=-1, keepdims=True)
    xc = x - mu
    var = jnp.mean(xc * xc, axis=-1, keepdims=True)
    return xc * lax.rsqrt(var + EPS) * g + b


def _rmsnorm(x, g):
    return x * lax.rsqrt(jnp.mean(x * x, axis=-1, keepdims=True) + EPS) * g


def _rope_mla(x, c, s_up, s_dn):
    return x * c + pltpu.roll(x, MLA_ROPE // 2, 1) * s_up + pltpu.roll(x, LANE - MLA_ROPE // 2, 1) * s_dn


def _lru_gates(xc, gate_w_ref, b_a, b_x, sp_lambda):
    rs, is_ = [], []
    for h in range(LRU_HEADS):
        g = _dot(xc[:, h * LRU_HEAD_DIM:(h + 1) * LRU_HEAD_DIM].astype(BF16), gate_w_ref[h])
        rs.append(g[:, :LRU_HEAD_DIM])
        is_.append(g[:, LRU_HEAD_DIM:])
    r = jax.nn.sigmoid(jnp.concatenate(rs, axis=1) + b_a)
    i = jax.nn.sigmoid(jnp.concatenate(is_, axis=1) + b_x)
    log_a = -LRU_C * r * sp_lambda
    a = jnp.exp(log_a)
    y = 1.0 - a * a
    mult = jnp.where(y > 0.0, y * lax.rsqrt(y), 0.0)
    return a, mult * (i * xc)


def _scan8(a, b):
    row = lax.broadcasted_iota(jnp.int32, a.shape, 0)
    for k in (1, 2, 4):
        keep = row >= k
        a_prev = jnp.where(keep, pltpu.roll(a, k, 0), 1.0)
        b_prev = jnp.where(keep, pltpu.roll(b, k, 0), 0.0)
        b = a * b_prev + b
        a = a * a_prev
    return a, b


def _lru_scan(a_ref, b_ref, h0, rows):
    def body(g, h_prev):
        sl = pl.ds(pl.multiple_of(g * SUBLANE, SUBLANE), SUBLANE)
        a_c, b_c = _scan8(a_ref[sl, :], b_ref[sl, :])
        h = a_c * h_prev + b_c
        b_ref[sl, :] = h
        return h[SUBLANE - 1:SUBLANE, :]
    return lax.fori_loop(0, rows // SUBLANE, body, h0, unroll=4)


def _mla_project(qlat, kvlat, kpe, q_g, kv_g, w_uq_ref, w_ukv_ref, rope_c, rope_up, rope_dn):
    scale = (MLA_NOPE + MLA_ROPE) ** -0.5
    q_all = _dot(_rmsnorm(qlat, q_g).astype(BF16), w_uq_ref[...])
    kv_all = _dot(_rmsnorm(kvlat, kv_g).astype(BF16), w_ukv_ref[...])
    kpe_r = _rope_mla(kpe, rope_c, rope_up, rope_dn)
    qs, ks = [], []
    for h in range(MLA_HEADS):
        sl = slice(h * LANE, (h + 1) * LANE)
        qs.append((_rope_mla(q_all[:, sl], rope_c, rope_up, rope_dn) * scale).astype(BF16))
        ks.append((kv_all[:, sl] + kpe_r).astype(BF16))
    v = kv_all[:, MLA_HEADS * LANE:].astype(BF16)
    return jnp.concatenate(qs, axis=1), jnp.concatenate(ks, axis=1), v


def _ffn(h1, w1_ref, w2_ref):
    h1b = h1.astype(BF16)
    f = None
    for c in range(D_FF // FF_TILE):
        a = _dot(h1b, w1_ref[:, c * FF_TILE:(c + 1) * FF_TILE])
        a = jnp.maximum(a, 0.0)
        part = _dot((a * a).astype(BF16), w2_ref[c * FF_TILE:(c + 1) * FF_TILE, :])
        f = part if f is None else f + part
    return f


def _mlp_block(h_in, mix, ln1_g, ln1_b, w1_ref, w2_ref, ln2_g, ln2_b):
    h1 = _layernorm(DN_ALPHA * h_in + mix, ln1_g, ln1_b)
    return _layernorm(DN_ALPHA * h1 + _ffn(h1, w1_ref, w2_ref), ln2_g, ln2_b)


def _meta_kernel(meta_ref, w_in_ref, conv_w_ref, conv_b_ref, gate_w_ref, b_a_ref, b_x_ref, lam_ref,
                 q_g_ref, kv_g_ref, w_uq_ref, w_ukv_ref, w_uvt_ref, rope_c_ref, rope_up_ref, rope_dn_ref,
                 w_out_ref, ln1_g_ref, ln1_b_ref, w1_ref, w2_ref, ln2_g_ref, ln2_b_ref,
                 w_k_ref, w_v_ref, cos1_ref, sin1_ref, lg_ref,
                 rec_tail_ref, h_tail_ref, k_meta_ref, vt_meta_ref, s_meta_ref,
                 conv_scr, a_scr, b_scr):
    n = N_META
    x = meta_ref[...]
    p = _dot(x.astype(BF16), w_in_ref[...])
    gate, rec = p[:, :LRU_WIDTH], p[:, LRU_WIDTH:2 * LRU_WIDTH]
    conv_scr[0:SUBLANE, :] = jnp.zeros((SUBLANE, LRU_WIDTH), F32)
    conv_scr[SUBLANE:SUBLANE + n, :] = rec
    cw = conv_w_ref[...]
    xc = conv_b_ref[...] + cw[3:4, :] * rec
    for j in range(CONV_WIDTH - 1):
        off = SUBLANE - (CONV_WIDTH - 1) + j
        xc = xc + cw[j:j + 1, :] * conv_scr[off:off + n, :]
    sp_lambda = jax.nn.softplus(-lam_ref[...])
    a, b = _lru_gates(xc, gate_w_ref, b_a_ref[...], b_x_ref[...], sp_lambda)
    a_scr[...] = a
    b_scr[...] = b
    _lru_scan(a_scr, b_scr, jnp.zeros((1, LRU_WIDTH), F32), n)
    h = b_scr[...]
    y_rec = (h * jax.nn.gelu(gate)).astype(BF16)
    rec_tail_ref[...] = rec[n - SUBLANE:, :]
    h_tail_ref[...] = h[n - SUBLANE:, :]

    off = 2 * LRU_WIDTH
    q, k, v = _mla_project(p[:, off:off + MLA_Q_RANK],
                           p[:, off + MLA_Q_RANK:off + MLA_Q_RANK + MLA_KV_RANK],
                           p[:, off + MLA_Q_RANK + MLA_KV_RANK:],
                           q_g_ref[...], kv_g_ref[...], w_uq_ref, w_ukv_ref,
                           rope_c_ref[...], rope_up_ref[...], rope_dn_ref[...])
    k_meta_ref[...] = k
    kvn = _rmsnorm(p[:, off + MLA_Q_RANK:off + MLA_Q_RANK + MLA_KV_RANK], kv_g_ref[...]).astype(BF16)
    vt_meta_ref[...] = _dot_nt(w_uvt_ref[...], kvn).astype(BF16)
    causal = (lax.broadcasted_iota(jnp.int32, (n, n), 1) <= lax.broadcasted_iota(jnp.int32, (n, n), 0))
    outs = []
    for hh in range(MLA_HEADS):
        sl = slice(hh * LANE, (hh + 1) * LANE)
        s = jnp.where(causal, _dot_nt(q[:, sl], k[:, sl]), NEG_INF)
        e = jnp.exp(s - jnp.max(s, axis=-1, keepdims=True))
        pr = e / jnp.sum(e, axis=-1, keepdims=True)
        outs.append(_dot(pr.astype(BF16), v[:, hh * MLA_V:(hh + 1) * MLA_V]))
    y_att = jnp.concatenate(outs, axis=1).astype(BF16)
    mix = _dot(y_rec, w_out_ref[0:LRU_WIDTH, :]) + _dot(y_att, w_out_ref[LRU_WIDTH:, :])
    h2 = _mlp_block(x, mix, ln1_g_ref[...], ln1_b_ref[...], w1_ref, w2_ref, ln2_g_ref[...], ln2_b_ref[...])

    h2b = h2.astype(BF16)
    kk = _dot(h2b, w_k_ref[...])
    vv = _dot(h2b, w_v_ref[...]).astype(BF16)
    cos, sin = cos1_ref[...], sin1_ref[...]
    idx = lax.broadcasted_iota(jnp.int32, (n, 1), 0).astype(F32)
    half = RET_QK_DIM // 2
    for hh in range(RET_HEADS):
        log_gamma = lg_ref[hh][:, 0:1]
        k1 = kk[:, hh * RET_QK_DIM:hh * RET_QK_DIM + half]
        k2 = kk[:, hh * RET_QK_DIM + half:(hh + 1) * RET_QK_DIM]
        kr = jnp.concatenate([k1 * cos - k2 * sin, k1 * sin + k2 * cos], axis=1).astype(BF16)
        k_dec = jnp.exp(log_gamma * (n - 1.0 - idx))
        kd = (kr.astype(F32) * k_dec).astype(BF16)
        s_meta_ref[hh] = _dot_tn(kd, vv[:, hh * RET_V_DIM:(hh + 1) * RET_V_DIM])


def _seq0_kernel(x_ref, w_in_ref, conv_w_ref, conv_b_ref, gate_w_ref, b_a_ref, b_x_ref, lam_ref,
                 q_g_ref, kv_g_ref, w_uqt_ref, w_uk_ref, w_uvt_ref, rope_c_ref, rope_up_ref, rope_dn_ref,
                 cos_t_ref, sin_t_ref, rec_tail_ref, h_tail_ref,
                 y_rec_ref, qt_ref, k_ref, vt_ref,
                 conv_scr, a_scr, b_scr, g_scr, h_scr):
    ts = SEQ_TILE
    t = pl.program_id(1)

    @pl.when(t == 0)
    def _():
        conv_scr[0:SUBLANE, :] = rec_tail_ref[...]
        h_scr[...] = h_tail_ref[...]

    sub = ts // SEQ_SPLIT
    spans = [slice(i * sub, (i + 1) * sub) for i in range(SEQ_SPLIT)]
    ps = [_dot(x_ref[0, rows, :].astype(BF16), w_in_ref[...]) for rows in spans]
    cw = conv_w_ref[...]
    sp_lambda = jax.nn.softplus(-lam_ref[...])
    scale = (MLA_NOPE + MLA_ROPE) ** -0.5 * math.log2(math.e)
    hr = MLA_ROPE // 2
    off = 2 * LRU_WIDTH
    for rows, p in zip(spans, ps):
        gate, rec = p[:, :LRU_WIDTH], p[:, LRU_WIDTH:2 * LRU_WIDTH]
        conv_scr[SUBLANE:SUBLANE + sub, :] = rec
        xc = conv_b_ref[...] + cw[3:4, :] * rec
        for j in range(CONV_WIDTH - 1):
            o = SUBLANE - (CONV_WIDTH - 1) + j
            xc = xc + cw[j:j + 1, :] * conv_scr[o:o + sub, :]
        conv_scr[0:SUBLANE, :] = rec[sub - SUBLANE:, :]
        a, b = _lru_gates(xc, gate_w_ref, b_a_ref[...], b_x_ref[...], sp_lambda)
        a_scr[rows, :] = a
        b_scr[rows, :] = b
        g_scr[rows, :] = jax.nn.gelu(gate)

        qn = _rmsnorm(p[:, off:off + MLA_Q_RANK], q_g_ref[...]).astype(BF16)
        q_t = _dot_nt(w_uqt_ref[...], qn)
        cos_t, sin_t = cos_t_ref[:, rows] * scale, sin_t_ref[:, rows] * scale
        for h in range(MLA_HEADS):
            base = h * LANE
            x1 = q_t[base + MLA_NOPE:base + MLA_NOPE + hr, :]
            x2 = q_t[base + MLA_NOPE + hr:base + MLA_NOPE + MLA_ROPE, :]
            qt_ref[0, 0, base:base + MLA_NOPE, rows] = (q_t[base:base + MLA_NOPE, :] * scale).astype(BF16)
            qt_ref[0, 0, base + MLA_NOPE:base + MLA_NOPE + hr, rows] = (x1 * cos_t - x2 * sin_t).astype(BF16)
            qt_ref[0, 0, base + MLA_NOPE + hr:base + MLA_NOPE + MLA_ROPE, rows] = (
                x1 * sin_t + x2 * cos_t).astype(BF16)
            qt_ref[0, 0, base + MLA_NOPE + MLA_ROPE:base + LANE, rows] = jnp.zeros(
                (LANE - MLA_NOPE - MLA_ROPE, sub), BF16)
        kvn = _rmsnorm(p[:, off + MLA_Q_RANK:off + MLA_Q_RANK + MLA_KV_RANK], kv_g_ref[...]).astype(BF16)
        k_nope = _dot(kvn, w_uk_ref[...])
        kpe_r = _rope_mla(p[:, off + MLA_Q_RANK + MLA_KV_RANK:],
                          rope_c_ref[rows, :], rope_up_ref[rows, :], rope_dn_ref[rows, :])
        for h in range(MLA_HEADS):
            sl = slice(h * LANE, (h + 1) * LANE)
            k_ref[0, rows, sl] = (k_nope[:, sl] + kpe_r).astype(BF16)
        vt_ref[0, 0, :, rows] = _dot_nt(w_uvt_ref[...], kvn).astype(BF16)

    h_last = _lru_scan(a_scr, b_scr, h_scr[SUBLANE - 1:SUBLANE, :], ts)
    h_scr[SUBLANE - 1:SUBLANE, :] = h_last
    y_rec_ref[0] = (b_scr[...] * g_scr[...]).astype(BF16)


def _attn_kernel(qt_ref, k_ref, vt_ref, k_meta_ref, vt_meta_ref, o_ref, m_scr, acc_scr):
    tq = ATT_TILE
    tw = tq // ATT_SPLIT
    qi = pl.program_id(2)
    chains = [(hh, slice(hh * LANE, (hh + 1) * LANE), slice(hh * MLA_V, (hh + 1) * MLA_V),
               part, slice(part * tw, (part + 1) * tw))
              for hh in range(2) for part in range(ATT_SPLIT)]
    def with_ones(v_t):
        return jnp.concatenate([v_t, jnp.ones((ATT_ONES, v_t.shape[1]), BF16)], axis=0)

    def meta_scores():
        return tuple(_dot(k_meta_ref[:, sl], qt_ref[0, 0, sl, cols]) for _, sl, _, _, cols in chains)

    def meta_softmax_pv(scores):
        for (hh, sl, vrows, part, cols), s in zip(chains, scores):
            m = jnp.max(s, axis=0, keepdims=True)
            e = jnp.exp2(s - m)
            m_scr[hh, :, cols] = m
            acc_scr[hh, :, cols] = _dot(with_ones(vt_meta_ref[vrows, :]), e.astype(BF16))

    def n_keys(part, diagonal):
        return (part + 1) * tw if diagonal else tq

    def scores_for(kj, diagonal):
        return tuple(_dot(k_ref[0, kj * tq:kj * tq + n_keys(part, diagonal), sl], qt_ref[0, 0, sl, cols])
                     for _, sl, _, part, cols in chains)

    def softmax_pv(kj, scores, diagonal):
        for (hh, sl, vrows, part, cols), s in zip(chains, scores):
            nk = n_keys(part, diagonal)
            if diagonal:
                key = lax.broadcasted_iota(jnp.int32, (nk, tw), 0)
                qry = lax.broadcasted_iota(jnp.int32, (nk, tw), 1) + part * tw
                s = jnp.where(key <= qry, s, NEG_INF)
            m_old = m_scr[hh, :, cols]
            m_new = jnp.maximum(m_old, jnp.max(s, axis=0, keepdims=True))
            alpha = jnp.exp2(m_old - m_new)
            e = jnp.exp2(s - m_new)
            m_scr[hh, :, cols] = m_new
            acc_scr[hh, :, cols] = alpha * acc_scr[hh, :, cols] + _dot(
                with_ones(vt_ref[0, kj, vrows, 0:nk]), e.astype(BF16))

    def run(n_full):
        s_meta = meta_scores()
        scores = scores_for(0, n_full == 0)
        meta_softmax_pv(s_meta)
        for kj in range(n_full):
            nxt = scores_for(kj + 1, kj + 1 == n_full)
            softmax_pv(kj, scores, False)
            scores = nxt
        softmax_pv(n_full, scores, True)

    for n_full in range(k_ref.shape[1] // tq):
        pl.when(qi == n_full)(functools.partial(run, n_full))
    out_t = jnp.concatenate([acc_scr[hh, 0:MLA_V, :] / acc_scr[hh, MLA_V:MLA_V + 1, :] for hh in range(2)],
                            axis=0)
    o_ref[0] = out_t.T.astype(BF16)


def _mix_mlp_kernel(*refs, n_mix):
    h_ref = refs[0]
    y_refs = refs[1:1 + n_mix]
    w_out_ref, ln1_g, ln1_b, w1_ref, w2_ref, ln2_g, ln2_b, o_ref = refs[1 + n_mix:]
    sub = h_ref.shape[0] // ROW_SPLIT
    spans = [slice(s * sub, (s + 1) * sub) for s in range(ROW_SPLIT)]
    mixes = []
    for rows in spans:
        mix = None
        row = 0
        for y_ref in y_refs:
            width = y_ref.shape[-1]
            part = _dot(y_ref[rows, :], w_out_ref[row:row + width, :])
            mix = part if mix is None else mix + part
            row += width
        mixes.append(mix)
    h1s = [_layernorm(DN_ALPHA * h_ref[rows, :] + mix, ln1_g[...], ln1_b[...]) for rows, mix in zip(spans, mixes)]
    fs = [_ffn(h1, w1_ref, w2_ref) for h1 in h1s]
    for rows, h1, f in zip(spans, h1s, fs):
        o_ref[rows, :] = _layernorm(DN_ALPHA * h1 + f, ln2_g[...], ln2_b[...])


def _ret_proj_kernel(h_ref, w_ref, cos_ref, sin_ref, q_ref, k_ref, v_ref, g_ref):
    hb = h_ref[...].astype(BF16)
    qk = RET_HEADS * RET_QK_DIM
    half = RET_QK_DIM // 2
    k_scale = RET_QK_DIM ** -0.5
    mixw = RET_HEADS * RET_V_DIM
    g_ref[...] = jax.nn.silu(_dot(hb, w_ref[:, 2 * qk + mixw:])).astype(BF16)
    for out_ref, base, scale in ((q_ref, 0, None), (k_ref, qk, k_scale)):
        cos, sin = cos_ref[...], sin_ref[...]
        if scale is not None:
            cos, sin = cos * scale, sin * scale
        pr = _dot(hb, w_ref[:, base:base + qk])
        parts = []
        for hh in range(RET_HEADS):
            x1 = pr[:, hh * RET_QK_DIM:hh * RET_QK_DIM + half]
            x2 = pr[:, hh * RET_QK_DIM + half:(hh + 1) * RET_QK_DIM]
            parts.append((x1 * cos - x2 * sin).astype(BF16))
            parts.append((x1 * sin + x2 * cos).astype(BF16))
        out_ref[...] = jnp.concatenate(parts, axis=1)
    v_ref[...] = _dot(hb, w_ref[:, 2 * qk:2 * qk + mixw]).astype(BF16)


def _retention_kernel(q_ref, k_ref, v_ref, g_ref, s0_ref, lg_ref, y_ref, s_scr):
    c = RET_CHUNK
    log_gamma = lg_ref[0][:, 0:1]
    ii = lax.broadcasted_iota(jnp.int32, (c, c), 0)
    jj = lax.broadcasted_iota(jnp.int32, (c, c), 1)
    diff = (ii - jj).astype(F32)
    decay = jnp.where(diff >= 0, jnp.exp(log_gamma * jnp.maximum(diff, 0.0)), 0.0)
    idx = lax.broadcasted_iota(jnp.int32, (c, 1), 0).astype(F32)
    q_decay = jnp.exp(log_gamma * (idx + 1.0))
    k_decay = jnp.exp(log_gamma * (c - 1.0 - idx))
    chunk_decay = jnp.exp(log_gamma * c)
    s_scr[...] = s0_ref[0]

    def recur(ci):
        rows = slice(ci * c, (ci + 1) * c)
        q = q_ref[0, rows, :]
        k = k_ref[0, rows, :]
        v = v_ref[0, rows, :]
        s_prev = s_scr[...]
        scores = _dot_nt(q, k) * decay
        o = _dot(scores.astype(BF16), v) + q_decay * _dot(q, s_prev.astype(BF16))
        kd = (k.astype(F32) * k_decay).astype(BF16)
        s_scr[...] = chunk_decay * s_prev + _dot_tn(kd, v)
        return o

    def finish(ci, o):
        rows = slice(ci * c, (ci + 1) * c)
        o = o * lax.rsqrt(jnp.mean(o * o, axis=-1, keepdims=True) + EPS)
        y_ref[0, rows, :] = (g_ref[0, rows, :].astype(F32) * o).astype(BF16)

    o_prev = None
    for ci in range(q_ref.shape[1] // c):
        o = recur(ci)
        if o_prev is not None:
            finish(ci - 1, o_prev)
        o_prev = o
    finish(q_ref.shape[1] // c - 1, o_prev)


def _const_spec(shape):
    zeros = (0,) * len(shape)
    return pl.BlockSpec(shape, lambda *_: zeros, pipeline_mode=pl.Buffered(1))


def _params(*semantics):
    return pltpu.CompilerParams(dimension_semantics=semantics, vmem_limit_bytes=VMEM_LIMIT)


def _rope_tables(positions, half):
    inv = ROPE_BASE ** (-jnp.arange(half, dtype=F32) / half)
    ang = positions.astype(F32)[:, None] * inv[None, :]
    return jnp.cos(ang), jnp.sin(ang)


def _mla_rope_tables(positions):
    cos, sin = _rope_tables(positions, MLA_ROPE // 2)
    n = positions.shape[0]
    h = MLA_ROPE // 2
    ones = jnp.ones((n, MLA_NOPE), F32)
    zeros = lambda w: jnp.zeros((n, w), F32)
    c = jnp.concatenate([ones, cos, cos, zeros(LANE - MLA_NOPE - MLA_ROPE)], axis=1)
    s_up = jnp.concatenate([zeros(MLA_NOPE + h), sin, zeros(LANE - MLA_NOPE - MLA_ROPE)], axis=1)
    s_dn = jnp.concatenate([zeros(MLA_NOPE), -sin, zeros(LANE - MLA_NOPE - h)], axis=1)
    return c, s_up, s_dn


def _mix_mlp_call(name, h, ys, w_out, ln1_g, ln1_b, w1, w2, ln2_g, ln2_b):
    m = h.shape[0]
    tm = ROW_TILE
    row_spec = lambda w: pl.BlockSpec((tm, w), lambda i: (i, 0))
    vec = _const_spec((1, D_MODEL))
    return pl.pallas_call(
        functools.partial(_mix_mlp_kernel, n_mix=len(ys)),
        grid=(m // tm,),
        in_specs=[row_spec(D_MODEL)] + [row_spec(y.shape[1]) for y in ys] + [
            _const_spec(w_out.shape), vec, vec, _const_spec(w1.shape), _const_spec(w2.shape), vec, vec],
        out_specs=row_spec(D_MODEL),
        out_shape=jax.ShapeDtypeStruct((m, D_MODEL), F32),
        compiler_params=_params("parallel"),
        name=name,
    )(h, *ys, w_out, ln1_g, ln1_b, w1, w2, ln2_g, ln2_b)


def kernel(x, meta_tokens, ev_w_in, ev_conv_w, ev_conv_b, ev_w_rg_a, ev_b_rg_a, ev_w_rg_x, ev_b_rg_x,
           ev_lru_lambda, ev_q_norm_g, ev_w_uq, ev_kv_norm_g, ev_w_ukv, ev_w_out, od_w_in, od_w_out,
           ln_mix_g, ln_mix_b, mlp_w1, mlp_w2, ln_mlp_g, ln_mlp_b):
    bsz, seq, _ = x.shape
    row = lambda v: v.reshape(1, -1).astype(F32)

    w_in0 = ev_w_in[0]
    lat0 = 2 * LRU_WIDTH
    kpe0 = lat0 + MLA_Q_RANK + MLA_KV_RANK
    w_kpe = jnp.zeros((D_MODEL, LANE), F32).at[:, MLA_NOPE:MLA_NOPE + MLA_ROPE].set(w_in0[:, kpe0:])
    w_in_ev = jnp.concatenate([w_in0[:, :kpe0], w_kpe], axis=1).astype(BF16)
    gate_w = jnp.concatenate([ev_w_rg_a[0], ev_w_rg_x[0]], axis=2).astype(BF16)
    w_uq = ev_w_uq[0].reshape(MLA_Q_RANK, MLA_HEADS, MLA_NOPE + MLA_ROPE)
    w_uq = jnp.pad(w_uq, ((0, 0), (0, 0), (0, LANE - MLA_NOPE - MLA_ROPE)))
    w_uq = w_uq.reshape(MLA_Q_RANK, MLA_HEADS * LANE).astype(BF16)
    w_ukv = ev_w_ukv[0].reshape(MLA_KV_RANK, MLA_HEADS, MLA_NOPE + MLA_V)
    w_uk = jnp.pad(w_ukv[:, :, :MLA_NOPE], ((0, 0), (0, 0), (0, LANE - MLA_NOPE)))
    w_uk = w_uk.reshape(MLA_KV_RANK, MLA_HEADS * LANE)
    w_uv = w_ukv[:, :, MLA_NOPE:].reshape(MLA_KV_RANK, MLA_HEADS * MLA_V)
    w_ukv_p = jnp.concatenate([w_uk, w_uv], axis=1).astype(BF16)
    w_uq_t = w_uq.T
    w_uk = w_uk.astype(BF16)
    w_uv_t = w_uv.T.astype(BF16)
    w_out_ev = ev_w_out[0].astype(BF16)
    w_in_od = od_w_in[0].astype(BF16)
    w_out_od = od_w_out[0].astype(BF16)
    w1 = [mlp_w1[l].astype(BF16) for l in range(2)]
    w2 = [mlp_w2[l].astype(BF16) for l in range(2)]
    conv_w = ev_conv_w[0].astype(F32)
    conv_b, b_a, b_x, lam = row(ev_conv_b[0]), row(ev_b_rg_a[0]), row(ev_b_rg_x[0]), row(ev_lru_lambda[0])
    q_g, kv_g = row(ev_q_norm_g[0]), row(ev_kv_norm_g[0])
    ln = [(row(ln_mix_g[l]), row(ln_mix_b[l]), row(ln_mlp_g[l]), row(ln_mlp_b[l])) for l in range(2)]

    pos = jnp.arange(N_META + seq, dtype=jnp.int32)
    mla_tabs = _mla_rope_tables(pos)
    cos0, sin0 = _rope_tables(pos, MLA_ROPE // 2)
    cos0_t, sin0_t = cos0.T, sin0.T
    cos1, sin1 = _rope_tables(pos, RET_QK_DIM // 2)
    k_scale = RET_QK_DIM ** -0.5
    qk = RET_HEADS * RET_QK_DIM
    log_gamma = jnp.log(1.0 - 2.0 ** (-5.0 - jnp.arange(RET_HEADS, dtype=F32)))
    log_gamma = jnp.broadcast_to(log_gamma[:, None, None], (RET_HEADS, 1, LANE))

    meta_out = pl.pallas_call(
        _meta_kernel,
        out_shape=(jax.ShapeDtypeStruct((SUBLANE, LRU_WIDTH), F32),
                   jax.ShapeDtypeStruct((SUBLANE, LRU_WIDTH), F32),
                   jax.ShapeDtypeStruct((N_META, MLA_HEADS * LANE), BF16),
                   jax.ShapeDtypeStruct((MLA_HEADS * MLA_V, N_META), BF16),
                   jax.ShapeDtypeStruct((RET_HEADS, RET_QK_DIM, RET_V_DIM), F32)),
        scratch_shapes=[pltpu.VMEM((SUBLANE + N_META, LRU_WIDTH), F32),
                        pltpu.VMEM((N_META, LRU_WIDTH), F32),
                        pltpu.VMEM((N_META, LRU_WIDTH), F32)],
        compiler_params=pltpu.CompilerParams(vmem_limit_bytes=VMEM_LIMIT),
        name="meta_tokens",
    )(meta_tokens.astype(F32), w_in_ev, conv_w, conv_b, gate_w, b_a, b_x, lam, q_g, kv_g, w_uq, w_ukv_p, w_uv_t,
      *[t[:N_META] for t in mla_tabs], w_out_ev, *ln[0][:2], w1[0], w2[0], *ln[0][2:],
      w_in_od[:, qk:2 * qk], w_in_od[:, 2 * qk:2 * qk + RET_HEADS * RET_V_DIM],
      cos1[:N_META] * k_scale, sin1[:N_META] * k_scale, log_gamma)
    rec_tail, h_tail, k_meta, vt_meta, s_meta = meta_out

    ts = SEQ_TILE
    nt = seq // ts
    tab_spec = pl.BlockSpec((ts, LANE), lambda b, t: (t, 0))
    tab_t_spec = pl.BlockSpec((MLA_ROPE // 2, ts), lambda b, t: (0, t))
    seq_spec = lambda w: pl.BlockSpec((1, ts, w), lambda b, t: (b, t, 0))
    seq_t_spec = lambda w: pl.BlockSpec((1, 1, w, ts), lambda b, t: (b, t, 0, 0))
    y_rec, qt0, k0, vt0 = pl.pallas_call(
        _seq0_kernel,
        grid=(bsz, nt),
        in_specs=[seq_spec(D_MODEL), _const_spec(w_in_ev.shape), _const_spec(conv_w.shape),
                  _const_spec(conv_b.shape), _const_spec(gate_w.shape), _const_spec(b_a.shape),
                  _const_spec(b_x.shape), _const_spec(lam.shape), _const_spec(q_g.shape),
                  _const_spec(kv_g.shape), _const_spec(w_uq_t.shape), _const_spec(w_uk.shape),
                  _const_spec(w_uv_t.shape), tab_spec, tab_spec, tab_spec, tab_t_spec, tab_t_spec,
                  _const_spec(rec_tail.shape), _const_spec(h_tail.shape)],
        out_specs=[seq_spec(LRU_WIDTH), seq_t_spec(MLA_HEADS * LANE), seq_spec(MLA_HEADS * LANE),
                   seq_t_spec(MLA_HEADS * MLA_V)],
        out_shape=(jax.ShapeDtypeStruct((bsz, seq, LRU_WIDTH), BF16),
                   jax.ShapeDtypeStruct((bsz, nt, MLA_HEADS * LANE, ts), BF16),
                   jax.ShapeDtypeStruct((bsz, seq, MLA_HEADS * LANE), BF16),
                   jax.ShapeDtypeStruct((bsz, nt, MLA_HEADS * MLA_V, ts), BF16)),
        scratch_shapes=[pltpu.VMEM((SUBLANE + ts // SEQ_SPLIT, LRU_WIDTH), F32),
                        pltpu.VMEM((ts, LRU_WIDTH), F32),
                        pltpu.VMEM((ts, LRU_WIDTH), F32),
                        pltpu.VMEM((ts, LRU_WIDTH), F32),
                        pltpu.VMEM((SUBLANE, LRU_WIDTH), F32)],
        compiler_params=_params("parallel", "arbitrary"),
        name="seq0_mixer_proj",
    )(x, w_in_ev, conv_w, conv_b, gate_w, b_a, b_x, lam, q_g, kv_g, w_uq_t, w_uk, w_uv_t,
      *[t[N_META:] for t in mla_tabs], cos0_t[:, N_META:], sin0_t[:, N_META:], rec_tail, h_tail)

    tq = ATT_TILE
    assert tq == ts
    y_att = pl.pallas_call(
        _attn_kernel,
        grid=(bsz, MLA_HEADS // 2, seq // tq),
        in_specs=[pl.BlockSpec((1, 1, 2 * LANE, tq), lambda b, j, i: (b, i, j, 0)),
                  pl.BlockSpec((1, seq, 2 * LANE), lambda b, j, i: (b, 0, j)),
                  pl.BlockSpec((1, nt, LANE, tq), lambda b, j, i: (b, 0, j, 0)),
                  pl.BlockSpec((N_META, 2 * LANE), lambda b, j, i: (0, j)),
                  pl.BlockSpec((LANE, N_META), lambda b, j, i: (j, 0))],
        out_specs=pl.BlockSpec((1, tq, LANE), lambda b, j, i: (b, i, j)),
        out_shape=jax.ShapeDtypeStruct((bsz, seq, MLA_HEADS * MLA_V), BF16),
        scratch_shapes=[pltpu.VMEM((2, 1, tq), F32), pltpu.VMEM((2, MLA_V + ATT_ONES, tq), F32)],
        compiler_params=_params("parallel", "parallel", "arbitrary"),
        name="mla_attention",
    )(qt0, k0, vt0, k_meta, vt_meta)

    m = bsz * seq
    h1 = _mix_mlp_call("layer0_out_mlp", x.reshape(m, D_MODEL),
                       [y_rec.reshape(m, -1), y_att.reshape(m, -1)], w_out_ev,
                       ln[0][0], ln[0][1], w1[0], w2[0], ln[0][2], ln[0][3])

    tm = ROW_TILE
    per_seq = seq // tm
    row_spec = lambda w: pl.BlockSpec((tm, w), lambda i: (i, 0))
    rope_spec = pl.BlockSpec((tm, RET_QK_DIM // 2), lambda i: (i % per_seq, 0))
    mixw = RET_HEADS * RET_V_DIM
    q1, k1, v1, g1 = pl.pallas_call(
        _ret_proj_kernel,
        grid=(m // tm,),
        in_specs=[row_spec(D_MODEL), _const_spec(w_in_od.shape), rope_spec, rope_spec],
        out_specs=[row_spec(qk), row_spec(qk), row_spec(mixw), row_spec(mixw)],
        out_shape=(jax.ShapeDtypeStruct((m, qk), BF16), jax.ShapeDtypeStruct((m, qk), BF16),
                   jax.ShapeDtypeStruct((m, mixw), BF16), jax.ShapeDtypeStruct((m, mixw), BF16)),
        compiler_params=_params("parallel"),
        name="layer1_in_proj",
    )(h1, w_in_od, cos1[N_META:], sin1[N_META:])

    head_spec = lambda w: pl.BlockSpec((1, seq, w), lambda b, h: (b, 0, h))
    y_ret = pl.pallas_call(
        _retention_kernel,
        grid=(bsz, RET_HEADS),
        in_specs=[head_spec(RET_QK_DIM), head_spec(RET_QK_DIM), head_spec(RET_V_DIM), head_spec(RET_V_DIM),
                  pl.BlockSpec((1, RET_QK_DIM, RET_V_DIM), lambda b, h: (h, 0, 0)),
                  pl.BlockSpec((1, 1, LANE), lambda b, h: (h, 0, 0))],
        out_specs=head_spec(RET_V_DIM),
        out_shape=jax.ShapeDtypeStruct((bsz, seq, mixw), BF16),
        scratch_shapes=[pltpu.VMEM((RET_QK_DIM, RET_V_DIM), F32)],
        compiler_params=_params("parallel", "parallel"),
        name="retention",
    )(q1.reshape(bsz, seq, qk), k1.reshape(bsz, seq, qk), v1.reshape(bsz, seq, mixw),
      g1.reshape(bsz, seq, mixw), s_meta, log_gamma)

    out = _mix_mlp_call("layer1_out_mlp", h1, [y_ret.reshape(m, mixw)], w_out_od,
                        ln[1][0], ln[1][1], w1[1], w2[1], ln[1][2], ln[1][3])
    return out.reshape(bsz, seq, D_MODEL)
```

```python
import functools
import math

import jax
import jax.numpy as jnp
import numpy as np
from jax import lax
from jax.experimental import pallas as pl
from jax.experimental.pallas import tpu as pltpu

D_MODEL = 1024
N_META = 16
LRU_WIDTH = 512
LRU_HEADS = 4
LRU_HEAD_DIM = 128
CONV_WIDTH = 4
LRU_C = 8.0
MLA_HEADS = 8
MLA_NOPE = 64
MLA_ROPE = 32
MLA_V = 64
MLA_Q_RANK = 256
MLA_KV_RANK = 128
RET_HEADS = 4
RET_QK_DIM = 256
RET_V_DIM = 512
D_FF = 4096
ROPE_BASE = 10000.0
DN_ALPHA = 4.0 ** 0.25
EPS = 1e-5
NEG_INF = -1e30

LANE = 128
SUBLANE = 8
VMEM_LIMIT = 56 * 1024 * 1024

BF16 = jnp.bfloat16
F32 = jnp.float32

SEQ_TILE = 512
SEQ_SPLIT = 2
SCAN_STEP = 4
SCAN_BLOCK = SUBLANE * SCAN_STEP
ATT_TILE = 512
ATT_SPLIT = 2
ATT_ONES = 16
ROW_TILE = 512
ROW_SPLIT = 2
FF_TILE = 1024
RET_CHUNK = 256


def _dot(a, b):
    return jnp.dot(a, b, preferred_element_type=F32)


def _dot_nt(a, b):
    return lax.dot_general(a, b, (((1,), (1,)), ((), ())), preferred_element_type=F32)


def _dot_tn(a, b):
    return lax.dot_general(a, b, (((0,), (0,)), ((), ())), preferred_element_type=F32)


def _layernorm(x, g, b):
    mu = jnp.mean(x, axis=-1, keepdims=True)
    xc = x - mu
    var = jnp.mean(xc * xc, axis=-1, keepdims=True)
    return xc * lax.rsqrt(var + EPS) * g + b


def _rmsnorm(x, g):
    return x * lax.rsqrt(jnp.mean(x * x, axis=-1, keepdims=True) + EPS) * g


def _rope_mla(x, c, s_up, s_dn):
    return x * c + pltpu.roll(x, MLA_ROPE // 2, 1) * s_up + pltpu.roll(x, LANE - MLA_ROPE // 2, 1) * s_dn


def _lru_gates(xc, gate_w_ref, b_a, b_x, sp_lambda):
    rs, is_ = [], []
    for h in range(LRU_HEADS):
        g = _dot(xc[:, h * LRU_HEAD_DIM:(h + 1) * LRU_HEAD_DIM].astype(BF16), gate_w_ref[h])
        rs.append(g[:, :LRU_HEAD_DIM])
        is_.append(g[:, LRU_HEAD_DIM:])
    r = jax.nn.sigmoid(jnp.concatenate(rs, axis=1) + b_a)
    i = jax.nn.sigmoid(jnp.concatenate(is_, axis=1) + b_x)
    log_a = -LRU_C * r * sp_lambda
    a = jnp.exp(log_a)
    y = 1.0 - a * a
    mult = jnp.where(y > 0.0, y * lax.rsqrt(y), 0.0)
    return a, mult * (i * xc)


def _scan8(a, b):
    row = lax.broadcasted_iota(jnp.int32, a.shape, 0)
    for k in (1, 2, 4):
        keep = row >= k
        a_prev = jnp.where(keep, pltpu.roll(a, k, 0), 1.0)
        b_prev = jnp.where(keep, pltpu.roll(b, k, 0), 0.0)
        b = a * b_prev + b
        a = a * a_prev
    return a, b


def _lru_scan(a_ref, b_ref, h0, rows):
    def body(g, h_prev):
        sl = pl.ds(pl.multiple_of(g * SUBLANE, SUBLANE), SUBLANE)
        a_c, b_c = _scan8(a_ref[sl, :], b_ref[sl, :])
        h = a_c * h_prev + b_c
        b_ref[sl, :] = h
        return h[SUBLANE - 1:SUBLANE, :]
    return lax.fori_loop(0, rows // SUBLANE, body, h0, unroll=4)


def _lru_scan_blocked(a_ref, b_ref, h0, rows):
    n_slabs = a_ref.shape[0]
    row = lax.broadcasted_iota(jnp.int32, (SUBLANE, LANE), 0)

    def body(g, carry):
        base = g * SCAN_BLOCK
        out = []
        for c in range(n_slabs):
            idx = [pl.ds(base + l, SUBLANE, stride=SCAN_STEP) for l in range(SCAN_STEP)]
            a = [a_ref[c, i, :] for i in idx]
            h = [b_ref[c, idx[0], :]]
            p = [a[0]]
            for l in range(1, SCAN_STEP):
                h.append(a[l] * h[l - 1] + b_ref[c, idx[l], :])
                p.append(a[l] * p[l - 1])
            p_seg, h_seg = _scan8(p[-1], h[-1])
            end = p_seg * carry[c] + h_seg
            start = jnp.where(row == 0, carry[c], pltpu.roll(end, 1, 0))
            for l in range(SCAN_STEP):
                b_ref[c, idx[l], :] = h[l] + p[l] * start
            out.append(end[SUBLANE - 1:SUBLANE, :])
        return tuple(out)
    return lax.fori_loop(0, rows // SCAN_BLOCK, body, tuple(h0), unroll=2)


def _mla_project(qlat, kvlat, kpe, q_g, kv_g, w_uq_ref, w_ukv_ref, rope_c, rope_up, rope_dn):
    scale = (MLA_NOPE + MLA_ROPE) ** -0.5
    q_all = _dot(_rmsnorm(qlat, q_g).astype(BF16), w_uq_ref[...])
    kv_all = _dot(_rmsnorm(kvlat, kv_g).astype(BF16), w_ukv_ref[...])
    kpe_r = _rope_mla(kpe, rope_c, rope_up, rope_dn)
    qs, ks = [], []
    for h in range(MLA_HEADS):
        sl = slice(h * LANE, (h + 1) * LANE)
        qs.append((_rope_mla(q_all[:, sl], rope_c, rope_up, rope_dn) * scale).astype(BF16))
        ks.append((kv_all[:, sl] + kpe_r).astype(BF16))
    v = kv_all[:, MLA_HEADS * LANE:].astype(BF16)
    return jnp.concatenate(qs, axis=1), jnp.concatenate(ks, axis=1), v


def _ffn(h1, w1_ref, w2_ref):
    h1b = h1.astype(BF16)
    f = None
    for c in range(D_FF // FF_TILE):
        a = _dot(h1b, w1_ref[:, c * FF_TILE:(c + 1) * FF_TILE])
        a = jnp.maximum(a, 0.0)
        part = _dot((a * a).astype(BF16), w2_ref[c * FF_TILE:(c + 1) * FF_TILE, :])
        f = part if f is None else f + part
    return f


def _mlp_block(h_in, mix, ln1_g, ln1_b, w1_ref, w2_ref, ln2_g, ln2_b):
    h1 = _layernorm(DN_ALPHA * h_in + mix, ln1_g, ln1_b)
    return _layernorm(DN_ALPHA * h1 + _ffn(h1, w1_ref, w2_ref), ln2_g, ln2_b)


def _meta_kernel(meta_ref, w_in_ref, conv_w_ref, conv_b_ref, gate_w_ref, b_a_ref, b_x_ref, lam_ref,
                 q_g_ref, kv_g_ref, w_uq_ref, w_ukv_ref, w_uvt_ref, rope_c_ref, rope_up_ref, rope_dn_ref,
                 w_out_ref, ln1_g_ref, ln1_b_ref, w1_ref, w2_ref, ln2_g_ref, ln2_b_ref,
                 w_k_ref, w_v_ref, cos1_ref, sin1_ref, lg_ref,
                 rec_tail_ref, h_tail_ref, k_meta_ref, vt_meta_ref, s_meta_ref,
                 conv_scr, a_scr, b_scr):
    n = N_META
    x = meta_ref[...]
    p = _dot(x.astype(BF16), w_in_ref[...])
    gate, rec = p[:, :LRU_WIDTH], p[:, LRU_WIDTH:2 * LRU_WIDTH]
    conv_scr[0:SUBLANE, :] = jnp.zeros((SUBLANE, LRU_WIDTH), F32)
    conv_scr[SUBLANE:SUBLANE + n, :] = rec
    cw = conv_w_ref[...]
    xc = conv_b_ref[...] + cw[3:4, :] * rec
    for j in range(CONV_WIDTH - 1):
        off = SUBLANE - (CONV_WIDTH - 1) + j
        xc = xc + cw[j:j + 1, :] * conv_scr[off:off + n, :]
    sp_lambda = jax.nn.softplus(-lam_ref[...])
    a, b = _lru_gates(xc, gate_w_ref, b_a_ref[...], b_x_ref[...], sp_lambda)
    a_scr[...] = a
    b_scr[...] = b
    _lru_scan(a_scr, b_scr, jnp.zeros((1, LRU_WIDTH), F32), n)
    h = b_scr[...]
    y_rec = (h * jax.nn.gelu(gate)).astype(BF16)
    rec_tail_ref[...] = rec[n - SUBLANE:, :]
    h_tail_ref[...] = h[n - SUBLANE:, :]

    off = 2 * LRU_WIDTH
    q, k, v = _mla_project(p[:, off:off + MLA_Q_RANK],
                           p[:, off + MLA_Q_RANK:off + MLA_Q_RANK + MLA_KV_RANK],
                           p[:, off + MLA_Q_RANK + MLA_KV_RANK:],
                           q_g_ref[...], kv_g_ref[...], w_uq_ref, w_ukv_ref,
                           rope_c_ref[...], rope_up_ref[...], rope_dn_ref[...])
    k_meta_ref[...] = k
    kvn = _rmsnorm(p[:, off + MLA_Q_RANK:off + MLA_Q_RANK + MLA_KV_RANK], kv_g_ref[...]).astype(BF16)
    vt_meta_ref[...] = _dot_nt(w_uvt_ref[...], kvn).astype(BF16)
    causal = (lax.broadcasted_iota(jnp.int32, (n, n), 1) <= lax.broadcasted_iota(jnp.int32, (n, n), 0))
    outs = []
    for hh in range(MLA_HEADS):
        sl = slice(hh * LANE, (hh + 1) * LANE)
        s = jnp.where(causal, _dot_nt(q[:, sl], k[:, sl]), NEG_INF)
        e = jnp.exp(s - jnp.max(s, axis=-1, keepdims=True))
        pr = e / jnp.sum(e, axis=-1, keepdims=True)
        outs.append(_dot(pr.astype(BF16), v[:, hh * MLA_V:(hh + 1) * MLA_V]))
    y_att = jnp.concatenate(outs, axis=1).astype(BF16)
    mix = _dot(y_rec, w_out_ref[0:LRU_WIDTH, :]) + _dot(y_att, w_out_ref[LRU_WIDTH:, :])
    h2 = _mlp_block(x, mix, ln1_g_ref[...], ln1_b_ref[...], w1_ref, w2_ref, ln2_g_ref[...], ln2_b_ref[...])

    h2b = h2.astype(BF16)
    kk = _dot(h2b, w_k_ref[...])
    vv = _dot(h2b, w_v_ref[...]).astype(BF16)
    cos, sin = cos1_ref[...], sin1_ref[...]
    idx = lax.broadcasted_iota(jnp.int32, (n, 1), 0).astype(F32)
    half = RET_QK_DIM // 2
    for hh in range(RET_HEADS):
        log_gamma = lg_ref[hh][:, 0:1]
        k1 = kk[:, hh * RET_QK_DIM:hh * RET_QK_DIM + half]
        k2 = kk[:, hh * RET_QK_DIM + half:(hh + 1) * RET_QK_DIM]
        kr = jnp.concatenate([k1 * cos - k2 * sin, k1 * sin + k2 * cos], axis=1).astype(BF16)
        k_dec = jnp.exp(log_gamma * (n - 1.0 - idx))
        kd = (kr.astype(F32) * k_dec).astype(BF16)
        s_meta_ref[hh] = _dot_tn(kd, vv[:, hh * RET_V_DIM:(hh + 1) * RET_V_DIM])


def _seq0_kernel(x_ref, w_in_ref, conv_w_ref, conv_b_ref, gate_w_ref, b_a_ref, b_x_ref, lam_ref,
                 q_g_ref, kv_g_ref, w_uqt_ref, w_uk_ref, w_uvt_ref, rope_c_ref, rope_up_ref, rope_dn_ref,
                 cos_t_ref, sin_t_ref, rec_tail_ref, h_tail_ref,
                 y_rec_ref, qt_ref, k_ref, vt_ref,
                 conv_scr, a_scr, b_scr, g_scr, h_scr):
    ts = SEQ_TILE
    t = pl.program_id(1)

    @pl.when(t == 0)
    def _():
        conv_scr[0:SUBLANE, :] = rec_tail_ref[...]
        h_scr[...] = h_tail_ref[...]

    sub = ts // SEQ_SPLIT
    spans = [slice(i * sub, (i + 1) * sub) for i in range(SEQ_SPLIT)]
    ps = [_dot(x_ref[0, rows, :].astype(BF16), w_in_ref[...]) for rows in spans]
    cw = conv_w_ref[...]
    sp_lambda = jax.nn.softplus(-lam_ref[...])
    scale = (MLA_NOPE + MLA_ROPE) ** -0.5 * math.log2(math.e)
    hr = MLA_ROPE // 2
    off = 2 * LRU_WIDTH
    for rows, p in zip(spans, ps):
        gate, rec = p[:, :LRU_WIDTH], p[:, LRU_WIDTH:2 * LRU_WIDTH]
        conv_scr[SUBLANE:SUBLANE + sub, :] = rec
        xc = conv_b_ref[...] + cw[3:4, :] * rec
        for j in range(CONV_WIDTH - 1):
            o = SUBLANE - (CONV_WIDTH - 1) + j
            xc = xc + cw[j:j + 1, :] * conv_scr[o:o + sub, :]
        conv_scr[0:SUBLANE, :] = rec[sub - SUBLANE:, :]
        a, b = _lru_gates(xc, gate_w_ref, b_a_ref[...], b_x_ref[...], sp_lambda)
        for c in range(LRU_WIDTH // LANE):
            a_scr[c, rows, :] = a[:, c * LANE:(c + 1) * LANE]
            b_scr[c, rows, :] = b[:, c * LANE:(c + 1) * LANE]
        g_scr[rows, :] = jax.nn.gelu(gate)

        qn = _rmsnorm(p[:, off:off + MLA_Q_RANK], q_g_ref[...]).astype(BF16)
        q_t = _dot_nt(w_uqt_ref[...], qn)
        cos_t, sin_t = cos_t_ref[:, rows] * scale, sin_t_ref[:, rows] * scale
        for h in range(MLA_HEADS):
            base = h * LANE
            x1 = q_t[base + MLA_NOPE:base + MLA_NOPE + hr, :]
            x2 = q_t[base + MLA_NOPE + hr:base + MLA_NOPE + MLA_ROPE, :]
            qt_ref[0, 0, base:base + MLA_NOPE, rows] = (q_t[base:base + MLA_NOPE, :] * scale).astype(BF16)
            qt_ref[0, 0, base + MLA_NOPE:base + MLA_NOPE + hr, rows] = (x1 * cos_t - x2 * sin_t).astype(BF16)
            qt_ref[0, 0, base + MLA_NOPE + hr:base + MLA_NOPE + MLA_ROPE, rows] = (
                x1 * sin_t + x2 * cos_t).astype(BF16)
            qt_ref[0, 0, base + MLA_NOPE + MLA_ROPE:base + LANE, rows] = jnp.zeros(
                (LANE - MLA_NOPE - MLA_ROPE, sub), BF16)
        kvn = _rmsnorm(p[:, off + MLA_Q_RANK:off + MLA_Q_RANK + MLA_KV_RANK], kv_g_ref[...]).astype(BF16)
        k_nope = _dot(kvn, w_uk_ref[...])
        kpe_r = _rope_mla(p[:, off + MLA_Q_RANK + MLA_KV_RANK:],
                          rope_c_ref[rows, :], rope_up_ref[rows, :], rope_dn_ref[rows, :])
        for h in range(MLA_HEADS):
            sl = slice(h * LANE, (h + 1) * LANE)
            k_ref[0, rows, sl] = (k_nope[:, sl] + kpe_r).astype(BF16)
        vt_ref[0, 0, :, rows] = _dot_nt(w_uvt_ref[...], kvn).astype(BF16)

    n_slabs = LRU_WIDTH // LANE
    h0 = [h_scr[SUBLANE - 1:SUBLANE, c * LANE:(c + 1) * LANE] for c in range(n_slabs)]
    h_last = _lru_scan_blocked(a_scr, b_scr, h0, ts)
    h_scr[SUBLANE - 1:SUBLANE, :] = jnp.concatenate(h_last, axis=1)
    h = jnp.concatenate([b_scr[c] for c in range(n_slabs)], axis=1)
    y_rec_ref[0] = (h * g_scr[...]).astype(BF16)


def _attn_kernel(qt_ref, k_ref, vt_ref, k_meta_ref, vt_meta_ref, o_ref, m_scr, acc_scr):
    tq = ATT_TILE
    tw = tq // ATT_SPLIT
    qi = pl.program_id(2)
    chains = [(hh, slice(hh * LANE, (hh + 1) * LANE), slice(hh * MLA_V, (hh + 1) * MLA_V),
               part, slice(part * tw, (part + 1) * tw))
              for hh in range(2) for part in range(ATT_SPLIT)]
    def with_ones(v_t):
        return jnp.concatenate([v_t, jnp.ones((ATT_ONES, v_t.shape[1]), BF16)], axis=0)

    def meta_scores():
        return tuple(_dot(k_meta_ref[:, sl], qt_ref[0, 0, sl, cols]) for _, sl, _, _, cols in chains)

    def meta_softmax_pv(scores):
        for (hh, sl, vrows, part, cols), s in zip(chains, scores):
            m = jnp.max(s, axis=0, keepdims=True)
            e = jnp.exp2(s - m)
            m_scr[hh, :, cols] = m
            acc_scr[hh, :, cols] = _dot(with_ones(vt_meta_ref[vrows, :]), e.astype(BF16))

    def n_keys(part, diagonal):
        return (part + 1) * tw if diagonal else tq

    def scores_for(kj, diagonal):
        return tuple(_dot(k_ref[0, kj * tq:kj * tq + n_keys(part, diagonal), sl], qt_ref[0, 0, sl, cols])
                     for _, sl, _, part, cols in chains)

    def softmax_pv(kj, scores, diagonal):
        for (hh, sl, vrows, part, cols), s in zip(chains, scores):
            nk = n_keys(part, diagonal)
            if diagonal:
                key = lax.broadcasted_iota(jnp.int32, (nk, tw), 0)
                qry = lax.broadcasted_iota(jnp.int32, (nk, tw), 1) + part * tw
                s = jnp.where(key <= qry, s, NEG_INF)
            m_old = m_scr[hh, :, cols]
            m_new = jnp.maximum(m_old, jnp.max(s, axis=0, keepdims=True))
            alpha = jnp.exp2(m_old - m_new)
            e = jnp.exp2(s - m_new)
            m_scr[hh, :, cols] = m_new
            acc_scr[hh, :, cols] = alpha * acc_scr[hh, :, cols] + _dot(
                with_ones(vt_ref[0, kj, vrows, 0:nk]), e.astype(BF16))

    def run(n_full):
        s_meta = meta_scores()
        scores = scores_for(0, n_full == 0)
        meta_softmax_pv(s_meta)
        for kj in range(n_full):
            nxt = scores_for(kj + 1, kj + 1 == n_full)
            softmax_pv(kj, scores, False)
            scores = nxt
        softmax_pv(n_full, scores, True)

    for n_full in range(k_ref.shape[1] // tq):
        pl.when(qi == n_full)(functools.partial(run, n_full))
    out_t = jnp.concatenate([acc_scr[hh, 0:MLA_V, :] / acc_scr[hh, MLA_V:MLA_V + 1, :] for hh in range(2)],
                            axis=0)
    o_ref[0] = out_t.T.astype(BF16)


def _mix_mlp_kernel(*refs, n_mix):
    h_ref = refs[0]
    y_refs = refs[1:1 + n_mix]
    w_out_ref, ln1_g, ln1_b, w1_ref, w2_ref, ln2_g, ln2_b, o_ref = refs[1 + n_mix:]
    sub = h_ref.shape[0] // ROW_SPLIT
    spans = [slice(s * sub, (s + 1) * sub) for s in range(ROW_SPLIT)]
    mixes = []
    for rows in spans:
        mix = None
        row = 0
        for y_ref in y_refs:
            width = y_ref.shape[-1]
            part = _dot(y_ref[rows, :], w_out_ref[row:row + width, :])
            mix = part if mix is None else mix + part
            row += width
        mixes.append(mix)
    h1s = [_layernorm(DN_ALPHA * h_ref[rows, :] + mix, ln1_g[...], ln1_b[...]) for rows, mix in zip(spans, mixes)]
    fs = [_ffn(h1, w1_ref, w2_ref) for h1 in h1s]
    for rows, h1, f in zip(spans, h1s, fs):
        o_ref[rows, :] = _layernorm(DN_ALPHA * h1 + f, ln2_g[...], ln2_b[...])


def _ret_proj_kernel(h_ref, w_ref, cos_ref, sin_ref, q_ref, k_ref, v_ref, g_ref):
    hb = h_ref[...].astype(BF16)
    qk = RET_HEADS * RET_QK_DIM
    half = RET_QK_DIM // 2
    k_scale = RET_QK_DIM ** -0.5
    mixw = RET_HEADS * RET_V_DIM
    half_g = 0.5 * _dot(hb, w_ref[:, 2 * qk + mixw:])
    g_ref[...] = (half_g + half_g * jnp.tanh(half_g)).astype(BF16)
    for out_ref, base, scale in ((q_ref, 0, None), (k_ref, qk, k_scale)):
        cos, sin = cos_ref[...], sin_ref[...]
        if scale is not None:
            cos, sin = cos * scale, sin * scale
        pr = _dot(hb, w_ref[:, base:base + qk])
        parts = []
        for hh in range(RET_HEADS):
            x1 = pr[:, hh * RET_QK_DIM:hh * RET_QK_DIM + half]
            x2 = pr[:, hh * RET_QK_DIM + half:(hh + 1) * RET_QK_DIM]
            parts.append((x1 * cos - x2 * sin).astype(BF16))
            parts.append((x1 * sin + x2 * cos).astype(BF16))
        out_ref[...] = jnp.concatenate(parts, axis=1)
    v_ref[...] = _dot(hb, w_ref[:, 2 * qk:2 * qk + mixw]).astype(BF16)


def _retention_kernel(q_ref, k_ref, v_ref, g_ref, s0_ref, lg_ref, y_ref, s_scr):
    c = RET_CHUNK
    log_gamma = lg_ref[0][:, 0:1]
    ii = lax.broadcasted_iota(jnp.int32, (c, c), 0)
    jj = lax.broadcasted_iota(jnp.int32, (c, c), 1)
    diff = (ii - jj).astype(F32)
    decay = jnp.where(diff >= 0, jnp.exp(log_gamma * jnp.maximum(diff, 0.0)), 0.0)
    idx = lax.broadcasted_iota(jnp.int32, (c, 1), 0).astype(F32)
    q_decay = jnp.exp(log_gamma * (idx + 1.0))
    k_decay = jnp.exp(log_gamma * (c - 1.0 - idx))
    chunk_decay = jnp.exp(log_gamma * c)
    s_scr[...] = s0_ref[0]

    def recur(ci):
        rows = slice(ci * c, (ci + 1) * c)
        q = q_ref[0, rows, :]
        k = k_ref[0, rows, :]
        v = v_ref[0, rows, :]
        s_prev = s_scr[...]
        scores = _dot_nt(q, k) * decay
        o = _dot(scores.astype(BF16), v) + q_decay * _dot(q, s_prev.astype(BF16))
        kd = (k.astype(F32) * k_decay).astype(BF16)
        s_scr[...] = chunk_decay * s_prev + _dot_tn(kd, v)
        return o

    def finish(ci, o):
        rows = slice(ci * c, (ci + 1) * c)
        o = o * lax.rsqrt(jnp.mean(o * o, axis=-1, keepdims=True) + EPS)
        y_ref[0, rows, :] = (g_ref[0, rows, :].astype(F32) * o).astype(BF16)

    o_prev = None
    for ci in range(q_ref.shape[1] // c):
        o = recur(ci)
        if o_prev is not None:
            finish(ci - 1, o_prev)
        o_prev = o
    finish(q_ref.shape[1] // c - 1, o_prev)


def _const_spec(shape):
    zeros = (0,) * len(shape)
    return pl.BlockSpec(shape, lambda *_: zeros, pipeline_mode=pl.Buffered(1))


def _params(*semantics):
    return pltpu.CompilerParams(dimension_semantics=semantics, vmem_limit_bytes=VMEM_LIMIT)


def _rope_tables(positions, half):
    inv = ROPE_BASE ** (-jnp.arange(half, dtype=F32) / half)
    ang = positions.astype(F32)[:, None] * inv[None, :]
    return jnp.cos(ang), jnp.sin(ang)


def _mla_rope_tables(positions):
    cos, sin = _rope_tables(positions, MLA_ROPE // 2)
    n = positions.shape[0]
    h = MLA_ROPE // 2
    ones = jnp.ones((n, MLA_NOPE), F32)
    zeros = lambda w: jnp.zeros((n, w), F32)
    c = jnp.concatenate([ones, cos, cos, zeros(LANE - MLA_NOPE - MLA_ROPE)], axis=1)
    s_up = jnp.concatenate([zeros(MLA_NOPE + h), sin, zeros(LANE - MLA_NOPE - MLA_ROPE)], axis=1)
    s_dn = jnp.concatenate([zeros(MLA_NOPE), -sin, zeros(LANE - MLA_NOPE - h)], axis=1)
    return c, s_up, s_dn


def _layer_spec(stacked_shape, layer):
    return pl.BlockSpec((None,) + tuple(stacked_shape[1:]), lambda *_: (layer, 0, 0),
                        pipeline_mode=pl.Buffered(1))


def _mix_mlp_call(name, layer, h, ys, w_out, ln1_g, ln1_b, w1, w2, ln2_g, ln2_b):
    m = h.shape[0]
    tm = ROW_TILE
    row_spec = lambda w: pl.BlockSpec((tm, w), lambda i: (i, 0))
    vec = _const_spec((1, D_MODEL))
    return pl.pallas_call(
        functools.partial(_mix_mlp_kernel, n_mix=len(ys)),
        grid=(m // tm,),
        in_specs=[row_spec(D_MODEL)] + [row_spec(y.shape[1]) for y in ys] + [
            _const_spec(w_out.shape), vec, vec, _layer_spec(w1.shape, layer), _layer_spec(w2.shape, layer),
            vec, vec],
        out_specs=row_spec(D_MODEL),
        out_shape=jax.ShapeDtypeStruct((m, D_MODEL), F32),
        compiler_params=_params("parallel"),
        name=name,
    )(h, *ys, w_out, ln1_g, ln1_b, w1, w2, ln2_g, ln2_b)


def kernel(x, meta_tokens, ev_w_in, ev_conv_w, ev_conv_b, ev_w_rg_a, ev_b_rg_a, ev_w_rg_x, ev_b_rg_x,
           ev_lru_lambda, ev_q_norm_g, ev_w_uq, ev_kv_norm_g, ev_w_ukv, ev_w_out, od_w_in, od_w_out,
           ln_mix_g, ln_mix_b, mlp_w1, mlp_w2, ln_mlp_g, ln_mlp_b):
    bsz, seq, _ = x.shape
    row = lambda v: v.reshape(1, -1).astype(F32)

    w_in0 = ev_w_in[0]
    lat0 = 2 * LRU_WIDTH
    kpe0 = lat0 + MLA_Q_RANK + MLA_KV_RANK
    w_kpe = jnp.zeros((D_MODEL, LANE), F32).at[:, MLA_NOPE:MLA_NOPE + MLA_ROPE].set(w_in0[:, kpe0:])
    w_in_ev = jnp.concatenate([w_in0[:, :kpe0], w_kpe], axis=1).astype(BF16)
    gate_w = jnp.concatenate([ev_w_rg_a[0], ev_w_rg_x[0]], axis=2).astype(BF16)
    w_uq = ev_w_uq[0].reshape(MLA_Q_RANK, MLA_HEADS, MLA_NOPE + MLA_ROPE)
    w_uq = jnp.pad(w_uq, ((0, 0), (0, 0), (0, LANE - MLA_NOPE - MLA_ROPE)))
    w_uq = w_uq.reshape(MLA_Q_RANK, MLA_HEADS * LANE).astype(BF16)
    w_ukv = ev_w_ukv[0].reshape(MLA_KV_RANK, MLA_HEADS, MLA_NOPE + MLA_V)
    w_uk = jnp.pad(w_ukv[:, :, :MLA_NOPE], ((0, 0), (0, 0), (0, LANE - MLA_NOPE)))
    w_uk = w_uk.reshape(MLA_KV_RANK, MLA_HEADS * LANE)
    w_uv = w_ukv[:, :, MLA_NOPE:].reshape(MLA_KV_RANK, MLA_HEADS * MLA_V)
    w_ukv_p = jnp.concatenate([w_uk, w_uv], axis=1).astype(BF16)
    w_uq_t = w_uq.T
    w_uk = w_uk.astype(BF16)
    w_uv_t = w_uv.T.astype(BF16)
    w_out_ev = ev_w_out[0].astype(BF16)
    w_in_od = od_w_in[0].astype(BF16)
    w_out_od = od_w_out[0].astype(BF16)
    w1 = mlp_w1.astype(BF16)
    w2 = mlp_w2.astype(BF16)
    conv_w = ev_conv_w[0].astype(F32)
    conv_b, b_a, b_x, lam = row(ev_conv_b[0]), row(ev_b_rg_a[0]), row(ev_b_rg_x[0]), row(ev_lru_lambda[0])
    q_g, kv_g = row(ev_q_norm_g[0]), row(ev_kv_norm_g[0])
    ln = [(row(ln_mix_g[l]), row(ln_mix_b[l]), row(ln_mlp_g[l]), row(ln_mlp_b[l])) for l in range(2)]

    pos = jnp.arange(N_META + seq, dtype=jnp.int32)
    mla_tabs = _mla_rope_tables(pos)
    cos0, sin0 = _rope_tables(pos, MLA_ROPE // 2)
    cos0_t, sin0_t = cos0.T, sin0.T
    cos1, sin1 = _rope_tables(pos, RET_QK_DIM // 2)
    k_scale = RET_QK_DIM ** -0.5
    qk = RET_HEADS * RET_QK_DIM
    log_gamma = jnp.log(1.0 - 2.0 ** (-5.0 - jnp.arange(RET_HEADS, dtype=F32)))
    log_gamma = jnp.broadcast_to(log_gamma[:, None, None], (RET_HEADS, 1, LANE))

    mixw = RET_HEADS * RET_V_DIM
    meta_out_shapes = (jax.ShapeDtypeStruct((SUBLANE, LRU_WIDTH), F32),
                       jax.ShapeDtypeStruct((SUBLANE, LRU_WIDTH), F32),
                       jax.ShapeDtypeStruct((N_META, MLA_HEADS * LANE), BF16),
                       jax.ShapeDtypeStruct((MLA_HEADS * MLA_V, N_META), BF16),
                       jax.ShapeDtypeStruct((RET_HEADS, RET_QK_DIM, RET_V_DIM), F32))
    meta_args = [meta_tokens.astype(F32), w_in_ev, conv_w, conv_b, gate_w, b_a, b_x, lam, q_g, kv_g, w_uq,
                 w_ukv_p, w_uv_t, *[t[:N_META] for t in mla_tabs], w_out_ev, *ln[0][:2]]
    meta_specs = [_const_spec(a.shape) for a in meta_args]
    meta_args += [w1, w2, *ln[0][2:], w_in_od, w_in_od]
    meta_specs += [_layer_spec(w1.shape, 0), _layer_spec(w2.shape, 0), _const_spec(ln[0][2].shape),
                   _const_spec(ln[0][3].shape),
                   pl.BlockSpec((D_MODEL, qk), lambda i: (0, 1), pipeline_mode=pl.Buffered(1)),
                   pl.BlockSpec((D_MODEL, mixw), lambda i: (0, 1), pipeline_mode=pl.Buffered(1))]
    tail_args = [cos1[:N_META] * k_scale, sin1[:N_META] * k_scale, log_gamma]
    meta_args += tail_args
    meta_specs += [_const_spec(a.shape) for a in tail_args]
    meta_out = pl.pallas_call(
        _meta_kernel,
        grid=(1,),
        in_specs=meta_specs,
        out_specs=[pl.BlockSpec(s.shape, lambda i, nd=len(s.shape): (0,) * nd) for s in meta_out_shapes],
        out_shape=meta_out_shapes,
        scratch_shapes=[pltpu.VMEM((SUBLANE + N_META, LRU_WIDTH), F32),
                        pltpu.VMEM((N_META, LRU_WIDTH), F32),
                        pltpu.VMEM((N_META, LRU_WIDTH), F32)],
        compiler_params=_params("arbitrary"),
        name="meta_tokens",
    )(*meta_args)
    rec_tail, h_tail, k_meta, vt_meta, s_meta = meta_out

    ts = SEQ_TILE
    nt = seq // ts
    tab_spec = pl.BlockSpec((ts, LANE), lambda b, t: (t, 0))
    tab_t_spec = pl.BlockSpec((MLA_ROPE // 2, ts), lambda b, t: (0, t))
    seq_spec = lambda w: pl.BlockSpec((1, ts, w), lambda b, t: (b, t, 0))
    seq_t_spec = lambda w: pl.BlockSpec((1, 1, w, ts), lambda b, t: (b, t, 0, 0))
    y_rec, qt0, k0, vt0 = pl.pallas_call(
        _seq0_kernel,
        grid=(bsz, nt),
        in_specs=[seq_spec(D_MODEL), _const_spec(w_in_ev.shape), _const_spec(conv_w.shape),
                  _const_spec(conv_b.shape), _const_spec(gate_w.shape), _const_spec(b_a.shape),
                  _const_spec(b_x.shape), _const_spec(lam.shape), _const_spec(q_g.shape),
                  _const_spec(kv_g.shape), _const_spec(w_uq_t.shape), _const_spec(w_uk.shape),
                  _const_spec(w_uv_t.shape), tab_spec, tab_spec, tab_spec, tab_t_spec, tab_t_spec,
                  _const_spec(rec_tail.shape), _const_spec(h_tail.shape)],
        out_specs=[seq_spec(LRU_WIDTH), seq_t_spec(MLA_HEADS * LANE), seq_spec(MLA_HEADS * LANE),
                   seq_t_spec(MLA_HEADS * MLA_V)],
        out_shape=(jax.ShapeDtypeStruct((bsz, seq, LRU_WIDTH), BF16),
                   jax.ShapeDtypeStruct((bsz, nt, MLA_HEADS * LANE, ts), BF16),
                   jax.ShapeDtypeStruct((bsz, seq, MLA_HEADS * LANE), BF16),
                   jax.ShapeDtypeStruct((bsz, nt, MLA_HEADS * MLA_V, ts), BF16)),
        scratch_shapes=[pltpu.VMEM((SUBLANE + ts // SEQ_SPLIT, LRU_WIDTH), F32),
                        pltpu.VMEM((LRU_WIDTH // LANE, ts, LANE), F32),
                        pltpu.VMEM((LRU_WIDTH // LANE, ts, LANE), F32),
                        pltpu.VMEM((ts, LRU_WIDTH), F32),
                        pltpu.VMEM((SUBLANE, LRU_WIDTH), F32)],
        compiler_params=_params("parallel", "arbitrary"),
        name="seq0_mixer_proj",
    )(x, w_in_ev, conv_w, conv_b, gate_w, b_a, b_x, lam, q_g, kv_g, w_uq_t, w_uk, w_uv_t,
      *[t[N_META:] for t in mla_tabs], cos0_t[:, N_META:], sin0_t[:, N_META:], rec_tail, h_tail)

    tq = ATT_TILE
    assert tq == ts
    y_att = pl.pallas_call(
        _attn_kernel,
        grid=(bsz, MLA_HEADS // 2, seq // tq),
        in_specs=[pl.BlockSpec((1, 1, 2 * LANE, tq), lambda b, j, i: (b, i, j, 0)),
                  pl.BlockSpec((1, seq, 2 * LANE), lambda b, j, i: (b, 0, j)),
                  pl.BlockSpec((1, nt, LANE, tq), lambda b, j, i: (b, 0, j, 0)),
                  pl.BlockSpec((N_META, 2 * LANE), lambda b, j, i: (0, j)),
                  pl.BlockSpec((LANE, N_META), lambda b, j, i: (j, 0))],
        out_specs=pl.BlockSpec((1, tq, LANE), lambda b, j, i: (b, i, j)),
        out_shape=jax.ShapeDtypeStruct((bsz, seq, MLA_HEADS * MLA_V), BF16),
        scratch_shapes=[pltpu.VMEM((2, 1, tq), F32), pltpu.VMEM((2, MLA_V + ATT_ONES, tq), F32)],
        compiler_params=_params("parallel", "parallel", "arbitrary"),
        name="mla_attention",
    )(qt0, k0, vt0, k_meta, vt_meta)

    m = bsz * seq
    h1 = _mix_mlp_call("layer0_out_mlp", 0, x.reshape(m, D_MODEL),
                       [y_rec.reshape(m, -1), y_att.reshape(m, -1)], w_out_ev,
                       ln[0][0], ln[0][1], w1, w2, ln[0][2], ln[0][3])

    tm = ROW_TILE
    per_seq = seq // tm
    row_spec = lambda w: pl.BlockSpec((tm, w), lambda i: (i, 0))
    rope_spec = pl.BlockSpec((tm, RET_QK_DIM // 2), lambda i: (i % per_seq, 0))
    mixw = RET_HEADS * RET_V_DIM
    q1, k1, v1, g1 = pl.pallas_call(
        _ret_proj_kernel,
        grid=(m // tm,),
        in_specs=[row_spec(D_MODEL), _const_spec(w_in_od.shape), rope_spec, rope_spec],
        out_specs=[row_spec(qk), row_spec(qk), row_spec(mixw), row_spec(mixw)],
        out_shape=(jax.ShapeDtypeStruct((m, qk), BF16), jax.ShapeDtypeStruct((m, qk), BF16),
                   jax.ShapeDtypeStruct((m, mixw), BF16), jax.ShapeDtypeStruct((m, mixw), BF16)),
        compiler_params=_params("parallel"),
        name="layer1_in_proj",
    )(h1, w_in_od, cos1[N_META:], sin1[N_META:])

    head_spec = lambda w: pl.BlockSpec((1, seq, w), lambda b, h: (b, 0, h))
    y_ret = pl.pallas_call(
        _retention_kernel,
        grid=(bsz, RET_HEADS),
        in_specs=[head_spec(RET_QK_DIM), head_spec(RET_QK_DIM), head_spec(RET_V_DIM), head_spec(RET_V_DIM),
                  pl.BlockSpec((1, RET_QK_DIM, RET_V_DIM), lambda b, h: (h, 0, 0)),
                  pl.BlockSpec((1, 1, LANE), lambda b, h: (h, 0, 0))],
        out_specs=head_spec(RET_V_DIM),
        out_shape=jax.ShapeDtypeStruct((bsz, seq, mixw), BF16),
        scratch_shapes=[pltpu.VMEM((RET_QK_DIM, RET_V_DIM), F32)],
        compiler_params=_params("parallel", "parallel"),
        name="retention",
    )(q1.reshape(bsz, seq, qk), k1.reshape(bsz, seq, qk), v1.reshape(bsz, seq, mixw),
      g1.reshape(bsz, seq, mixw), s_meta, log_gamma)

    out = _mix_mlp_call("layer1_out_mlp", 1, h1, [y_ret.reshape(m, mixw)], w_out_od,
                        ln[1][0], ln[1][1], w1, w2, ln[1][2], ln[1][3])
    return out.reshape(bsz, seq, D_MODEL)
```

```python
import functools
import math

import jax
import jax.numpy as jnp
import numpy as np
from jax import lax
from jax.experimental import pallas as pl
from jax.experimental.pallas import tpu as pltpu

D_MODEL = 1024
N_META = 16
LRU_WIDTH = 512
LRU_HEADS = 4
LRU_HEAD_DIM = 128
CONV_WIDTH = 4
LRU_C = 8.0
MLA_HEADS = 8
MLA_NOPE = 64
MLA_ROPE = 32
MLA_V = 64
MLA_Q_RANK = 256
MLA_KV_RANK = 128
RET_HEADS = 4
RET_QK_DIM = 256
RET_V_DIM = 512
D_FF = 4096
ROPE_BASE = 10000.0
DN_ALPHA = 4.0 ** 0.25
EPS = 1e-5
NEG_INF = -1e30

LANE = 128
SUBLANE = 8
VMEM_LIMIT = 56 * 1024 * 1024

BF16 = jnp.bfloat16
F32 = jnp.float32

SEQ_TILE = 512
SEQ_SPLIT = 2
SCAN_STEP = 4
SCAN_BLOCK = SUBLANE * SCAN_STEP
ATT_TILE = 512
ATT_SPLIT = 2
ATT_ONES = 16
ROW_TILE = 512
ROW_SPLIT = 2
FF_TILE = 1024
RET_CHUNK = 256


def _dot(a, b):
    return jnp.dot(a, b, preferred_element_type=F32)


def _dot_nt(a, b):
    return lax.dot_general(a, b, (((1,), (1,)), ((), ())), preferred_element_type=F32)


def _dot_tn(a, b):
    return lax.dot_general(a, b, (((0,), (0,)), ((), ())), preferred_element_type=F32)


def _layernorm(x, g, b):
    mu = jnp.mean(x, axis=-1, keepdims=True)
    xc = x - mu
    var = jnp.mean(xc * xc, axis=-1, keepdims=True)
    return xc * lax.rsqrt(var + EPS) * g + b


def _rmsnorm(x, g):
    return x * lax.rsqrt(jnp.mean(x * x, axis=-1, keepdims=True) + EPS) * g


def _rope_mla(x, c, s_up, s_dn):
    return x * c + pltpu.roll(x, MLA_ROPE // 2, 1) * s_up + pltpu.roll(x, LANE - MLA_ROPE // 2, 1) * s_dn


def _lru_gates(xc, gate_w_ref, b_a, b_x, sp_lambda):
    rs, is_ = [], []
    for h in range(LRU_HEADS):
        g = _dot(xc[:, h * LRU_HEAD_DIM:(h + 1) * LRU_HEAD_DIM].astype(BF16), gate_w_ref[h])
        rs.append(g[:, :LRU_HEAD_DIM])
        is_.append(g[:, LRU_HEAD_DIM:])
    r = jax.nn.sigmoid(jnp.concatenate(rs, axis=1) + b_a)
    i = jax.nn.sigmoid(jnp.concatenate(is_, axis=1) + b_x)
    log_a = -LRU_C * r * sp_lambda
    a = jnp.exp(log_a)
    y = 1.0 - a * a
    mult = jnp.where(y > 0.0, y * lax.rsqrt(y), 0.0)
    return a, mult * (i * xc)


def _scan8(a, b):
    row = lax.broadcasted_iota(jnp.int32, a.shape, 0)
    for k in (1, 2, 4):
        keep = row >= k
        a_prev = jnp.where(keep, pltpu.roll(a, k, 0), 1.0)
        b_prev = jnp.where(keep, pltpu.roll(b, k, 0), 0.0)
        b = a * b_prev + b
        a = a * a_prev
    return a, b


def _lru_scan(a_ref, b_ref, h0, rows):
    def body(g, h_prev):
        sl = pl.ds(pl.multiple_of(g * SUBLANE, SUBLANE), SUBLANE)
        a_c, b_c = _scan8(a_ref[sl, :], b_ref[sl, :])
        h = a_c * h_prev + b_c
        b_ref[sl, :] = h
        return h[SUBLANE - 1:SUBLANE, :]
    return lax.fori_loop(0, rows // SUBLANE, body, h0, unroll=4)


def _lru_scan_blocked(a_ref, b_ref, h0, rows):
    n_slabs = a_ref.shape[0]
    row = lax.broadcasted_iota(jnp.int32, (SUBLANE, LANE), 0)

    def body(g, carry):
        base = g * SCAN_BLOCK
        out = []
        for c in range(n_slabs):
            idx = [pl.ds(base + l, SUBLANE, stride=SCAN_STEP) for l in range(SCAN_STEP)]
            a = [a_ref[c, i, :] for i in idx]
            h = [b_ref[c, idx[0], :]]
            p = [a[0]]
            for l in range(1, SCAN_STEP):
                h.append(a[l] * h[l - 1] + b_ref[c, idx[l], :])
                p.append(a[l] * p[l - 1])
            p_seg, h_seg = _scan8(p[-1], h[-1])
            end = p_seg * carry[c] + h_seg
            start = jnp.where(row == 0, carry[c], pltpu.roll(end, 1, 0))
            for l in range(SCAN_STEP):
                b_ref[c, idx[l], :] = h[l] + p[l] * start
            out.append(end[SUBLANE - 1:SUBLANE, :])
        return tuple(out)
    return lax.fori_loop(0, rows // SCAN_BLOCK, body, tuple(h0), unroll=2)


def _mla_project(qlat, kvlat, kpe, q_g, kv_g, w_uq_ref, w_ukv_ref, rope_c, rope_up, rope_dn):
    scale = (MLA_NOPE + MLA_ROPE) ** -0.5
    q_all = _dot(_rmsnorm(qlat, q_g).astype(BF16), w_uq_ref[...])
    kv_all = _dot(_rmsnorm(kvlat, kv_g).astype(BF16), w_ukv_ref[...])
    kpe_r = _rope_mla(kpe, rope_c, rope_up, rope_dn)
    qs, ks = [], []
    for h in range(MLA_HEADS):
        sl = slice(h * LANE, (h + 1) * LANE)
        qs.append((_rope_mla(q_all[:, sl], rope_c, rope_up, rope_dn) * scale).astype(BF16))
        ks.append((kv_all[:, sl] + kpe_r).astype(BF16))
    v = kv_all[:, MLA_HEADS * LANE:].astype(BF16)
    return jnp.concatenate(qs, axis=1), jnp.concatenate(ks, axis=1), v


def _ffn(h1, w1_ref, w2_ref):
    h1b = h1.astype(BF16)
    f = None
    for c in range(D_FF // FF_TILE):
        a = _dot(h1b, w1_ref[:, c * FF_TILE:(c + 1) * FF_TILE])
        a = jnp.maximum(a, 0.0)
        part = _dot((a * a).astype(BF16), w2_ref[c * FF_TILE:(c + 1) * FF_TILE, :])
        f = part if f is None else f + part
    return f


def _mlp_block(h_in, mix, ln1_g, ln1_b, w1_ref, w2_ref, ln2_g, ln2_b):
    h1 = _layernorm(DN_ALPHA * h_in + mix, ln1_g, ln1_b)
    return _layernorm(DN_ALPHA * h1 + _ffn(h1, w1_ref, w2_ref), ln2_g, ln2_b)


def _meta_kernel(meta_ref, w_in_ref, conv_w_ref, conv_b_ref, gate_w_ref, b_a_ref, b_x_ref, lam_ref,
                 q_g_ref, kv_g_ref, w_uq_ref, w_ukv_ref, w_uvt_ref, rope_c_ref, rope_up_ref, rope_dn_ref,
                 w_out_ref, ln1_g_ref, ln1_b_ref, w1_ref, w2_ref, ln2_g_ref, ln2_b_ref,
                 w_k_ref, w_v_ref, cos1_ref, sin1_ref, lg_ref,
                 rec_tail_ref, h_tail_ref, k_meta_ref, vt_meta_ref, s_meta_ref,
                 conv_scr, a_scr, b_scr):
    n = N_META
    x = meta_ref[...]
    p = _dot(x.astype(BF16), w_in_ref[...])
    gate, rec = p[:, :LRU_WIDTH], p[:, LRU_WIDTH:2 * LRU_WIDTH]
    conv_scr[0:SUBLANE, :] = jnp.zeros((SUBLANE, LRU_WIDTH), F32)
    conv_scr[SUBLANE:SUBLANE + n, :] = rec
    cw = conv_w_ref[...]
    xc = conv_b_ref[...] + cw[3:4, :] * rec
    for j in range(CONV_WIDTH - 1):
        off = SUBLANE - (CONV_WIDTH - 1) + j
        xc = xc + cw[j:j + 1, :] * conv_scr[off:off + n, :]
    sp_lambda = jax.nn.softplus(-lam_ref[...])
    a, b = _lru_gates(xc, gate_w_ref, b_a_ref[...], b_x_ref[...], sp_lambda)
    a_scr[...] = a
    b_scr[...] = b
    _lru_scan(a_scr, b_scr, jnp.zeros((1, LRU_WIDTH), F32), n)
    h = b_scr[...]
    y_rec = (h * jax.nn.gelu(gate)).astype(BF16)
    rec_tail_ref[...] = rec[n - SUBLANE:, :]
    h_tail_ref[...] = h[n - SUBLANE:, :]

    off = 2 * LRU_WIDTH
    q, k, v = _mla_project(p[:, off:off + MLA_Q_RANK],
                           p[:, off + MLA_Q_RANK:off + MLA_Q_RANK + MLA_KV_RANK],
                           p[:, off + MLA_Q_RANK + MLA_KV_RANK:],
                           q_g_ref[...], kv_g_ref[...], w_uq_ref, w_ukv_ref,
                           rope_c_ref[...], rope_up_ref[...], rope_dn_ref[...])
    k_meta_ref[...] = k
    kvn = _rmsnorm(p[:, off + MLA_Q_RANK:off + MLA_Q_RANK + MLA_KV_RANK], kv_g_ref[...]).astype(BF16)
    vt_meta_ref[...] = _dot_nt(w_uvt_ref[...], kvn).astype(BF16)
    causal = (lax.broadcasted_iota(jnp.int32, (n, n), 1) <= lax.broadcasted_iota(jnp.int32, (n, n), 0))
    outs = []
    for hh in range(MLA_HEADS):
        sl = slice(hh * LANE, (hh + 1) * LANE)
        s = jnp.where(causal, _dot_nt(q[:, sl], k[:, sl]), NEG_INF)
        e = jnp.exp(s - jnp.max(s, axis=-1, keepdims=True))
        pr = e / jnp.sum(e, axis=-1, keepdims=True)
        outs.append(_dot(pr.astype(BF16), v[:, hh * MLA_V:(hh + 1) * MLA_V]))
    y_att = jnp.concatenate(outs, axis=1).astype(BF16)
    mix = _dot(y_rec, w_out_ref[0:LRU_WIDTH, :]) + _dot(y_att, w_out_ref[LRU_WIDTH:, :])
    h2 = _mlp_block(x, mix, ln1_g_ref[...], ln1_b_ref[...], w1_ref, w2_ref, ln2_g_ref[...], ln2_b_ref[...])

    h2b = h2.astype(BF16)
    kk = _dot(h2b, w_k_ref[...])
    vv = _dot(h2b, w_v_ref[...]).astype(BF16)
    cos, sin = cos1_ref[...], sin1_ref[...]
    idx = lax.broadcasted_iota(jnp.int32, (n, 1), 0).astype(F32)
    half = RET_QK_DIM // 2
    for hh in range(RET_HEADS):
        log_gamma = lg_ref[hh][:, 0:1]
        k1 = kk[:, hh * RET_QK_DIM:hh * RET_QK_DIM + half]
        k2 = kk[:, hh * RET_QK_DIM + half:(hh + 1) * RET_QK_DIM]
        kr = jnp.concatenate([k1 * cos - k2 * sin, k1 * sin + k2 * cos], axis=1).astype(BF16)
        k_dec = jnp.exp(log_gamma * (n - 1.0 - idx))
        kd = (kr.astype(F32) * k_dec).astype(BF16)
        s_meta_ref[hh] = _dot_tn(kd, vv[:, hh * RET_V_DIM:(hh + 1) * RET_V_DIM])


def _seq0_kernel(x_ref, w_in_ref, conv_w_ref, conv_b_ref, gate_w_ref, b_a_ref, b_x_ref, lam_ref,
                 q_g_ref, kv_g_ref, w_uqt_ref, w_uk_ref, w_uvt_ref, rope_c_ref, rope_up_ref, rope_dn_ref,
                 cos_t_ref, sin_t_ref, rec_tail_ref, h_tail_ref,
                 y_rec_ref, qt_ref, k_ref, vt_ref,
                 conv_scr, a_scr, b_scr, g_scr, h_scr):
    ts = SEQ_TILE
    t = pl.program_id(1)

    @pl.when(t == 0)
    def _():
        conv_scr[0:SUBLANE, :] = rec_tail_ref[...]
        h_scr[...] = h_tail_ref[...]

    sub = ts // SEQ_SPLIT
    spans = [slice(i * sub, (i + 1) * sub) for i in range(SEQ_SPLIT)]
    ps = [_dot(x_ref[0, rows, :].astype(BF16), w_in_ref[...]) for rows in spans]
    cw = conv_w_ref[...]
    sp_lambda = jax.nn.softplus(-lam_ref[...])
    scale = (MLA_NOPE + MLA_ROPE) ** -0.5 * math.log2(math.e)
    hr = MLA_ROPE // 2
    off = 2 * LRU_WIDTH
    for rows, p in zip(spans, ps):
        gate, rec = p[:, :LRU_WIDTH], p[:, LRU_WIDTH:2 * LRU_WIDTH]
        conv_scr[SUBLANE:SUBLANE + sub, :] = rec
        xc = conv_b_ref[...] + cw[3:4, :] * rec
        for j in range(CONV_WIDTH - 1):
            o = SUBLANE - (CONV_WIDTH - 1) + j
            xc = xc + cw[j:j + 1, :] * conv_scr[o:o + sub, :]
        conv_scr[0:SUBLANE, :] = rec[sub - SUBLANE:, :]
        a, b = _lru_gates(xc, gate_w_ref, b_a_ref[...], b_x_ref[...], sp_lambda)
        for c in range(LRU_WIDTH // LANE):
            a_scr[c, rows, :] = a[:, c * LANE:(c + 1) * LANE]
            b_scr[c, rows, :] = b[:, c * LANE:(c + 1) * LANE]
        g_scr[rows, :] = jax.nn.gelu(gate)

        qn = _rmsnorm(p[:, off:off + MLA_Q_RANK], q_g_ref[...]).astype(BF16)
        q_t = _dot_nt(w_uqt_ref[...], qn)
        cos_t, sin_t = cos_t_ref[:, rows] * scale, sin_t_ref[:, rows] * scale
        for h in range(MLA_HEADS):
            base = h * LANE
            x1 = q_t[base + MLA_NOPE:base + MLA_NOPE + hr, :]
            x2 = q_t[base + MLA_NOPE + hr:base + MLA_NOPE + MLA_ROPE, :]
            qt_ref[0, 0, base:base + MLA_NOPE, rows] = (q_t[base:base + MLA_NOPE, :] * scale).astype(BF16)
            qt_ref[0, 0, base + MLA_NOPE:base + MLA_NOPE + hr, rows] = (x1 * cos_t - x2 * sin_t).astype(BF16)
            qt_ref[0, 0, base + MLA_NOPE + hr:base + MLA_NOPE + MLA_ROPE, rows] = (
                x1 * sin_t + x2 * cos_t).astype(BF16)
            qt_ref[0, 0, base + MLA_NOPE + MLA_ROPE:base + LANE, rows] = jnp.zeros(
                (LANE - MLA_NOPE - MLA_ROPE, sub), BF16)
        kvn = _rmsnorm(p[:, off + MLA_Q_RANK:off + MLA_Q_RANK + MLA_KV_RANK], kv_g_ref[...]).astype(BF16)
        k_nope = _dot(kvn, w_uk_ref[...])
        kpe_r = _rope_mla(p[:, off + MLA_Q_RANK + MLA_KV_RANK:],
                          rope_c_ref[rows, :], rope_up_ref[rows, :], rope_dn_ref[rows, :])
        for h in range(MLA_HEADS):
            sl = slice(h * LANE, (h + 1) * LANE)
            k_ref[0, rows, sl] = (k_nope[:, sl] + kpe_r).astype(BF16)
        vt_ref[0, 0, :, rows] = _dot_nt(w_uvt_ref[...], kvn).astype(BF16)

    n_slabs = LRU_WIDTH // LANE
    h0 = [h_scr[SUBLANE - 1:SUBLANE, c * LANE:(c + 1) * LANE] for c in range(n_slabs)]
    h_last = _lru_scan_blocked(a_scr, b_scr, h0, ts)
    h_scr[SUBLANE - 1:SUBLANE, :] = jnp.concatenate(h_last, axis=1)
    h = jnp.concatenate([b_scr[c] for c in range(n_slabs)], axis=1)
    y_rec_ref[0] = (h * g_scr[...]).astype(BF16)


def _attn_kernel(qt_ref, k_ref, vt_ref, k_meta_ref, vt_meta_ref, o_ref, m_scr, acc_scr):
    tq = ATT_TILE
    tw = tq // ATT_SPLIT
    qi = pl.program_id(2)
    chains = [(hh, slice(hh * LANE, (hh + 1) * LANE), slice(hh * MLA_V, (hh + 1) * MLA_V),
               part, slice(part * tw, (part + 1) * tw))
              for hh in range(2) for part in range(ATT_SPLIT)]
    def with_ones(v_t):
        return jnp.concatenate([v_t, jnp.ones((ATT_ONES, v_t.shape[1]), BF16)], axis=0)

    def n_keys(part, diagonal):
        return (part + 1) * tw if diagonal else tq

    def score(chain, kj, diagonal):
        _, sl, _, part, cols = chain
        keys = k_ref[0, kj * tq:kj * tq + n_keys(part, diagonal), sl]
        if kj == 0:
            keys = jnp.concatenate([k_meta_ref[:, sl], keys], axis=0)
        return _dot(keys, qt_ref[0, 0, sl, cols])

    def softmax_pv(chain, kj, s, diagonal):
        hh, sl, vrows, part, cols = chain
        nk = n_keys(part, diagonal)
        lead = N_META if kj == 0 else 0
        if diagonal:
            key = lax.broadcasted_iota(jnp.int32, (lead + nk, tw), 0) - lead
            qry = lax.broadcasted_iota(jnp.int32, (lead + nk, tw), 1) + part * tw
            s = jnp.where(key <= qry, s, NEG_INF)
        m_new = jnp.max(s, axis=0, keepdims=True)
        if kj > 0:
            m_old = m_scr[hh, :, cols]
            m_new = jnp.maximum(m_old, m_new)
        e = jnp.exp2(s - m_new).astype(BF16)
        m_scr[hh, :, cols] = m_new
        pv = _dot(with_ones(vt_ref[0, kj, vrows, 0:nk]), e[lead:])
        if kj == 0:
            acc_scr[hh, :, cols] = pv + _dot(with_ones(vt_meta_ref[vrows, :]), e[0:lead])
        else:
            acc_scr[hh, :, cols] = jnp.exp2(m_old - m_new) * acc_scr[hh, :, cols] + pv

    def run(n_full):
        scores = [score(c, 0, n_full == 0) for c in chains]
        for kj in range(n_full):
            nxt = [score(c, kj + 1, kj + 1 == n_full) for c in chains]
            for c, s in zip(chains, scores):
                softmax_pv(c, kj, s, False)
            scores = nxt
        for c, s in zip(chains, scores):
            softmax_pv(c, n_full, s, True)

    for n_full in range(k_ref.shape[1] // tq):
        pl.when(qi == n_full)(functools.partial(run, n_full))
    out_t = jnp.concatenate([acc_scr[hh, 0:MLA_V, :] / acc_scr[hh, MLA_V:MLA_V + 1, :] for hh in range(2)],
                            axis=0)
    o_ref[0] = out_t.T.astype(BF16)


def _mix_mlp_kernel(*refs, n_mix):
    h_ref = refs[0]
    y_refs = refs[1:1 + n_mix]
    w_out_ref, ln1_g, ln1_b, w1_ref, w2_ref, ln2_g, ln2_b, o_ref = refs[1 + n_mix:]
    sub = h_ref.shape[0] // ROW_SPLIT
    spans = [slice(s * sub, (s + 1) * sub) for s in range(ROW_SPLIT)]
    mixes = []
    for rows in spans:
        mix = None
        row = 0
        for y_ref in y_refs:
            width = y_ref.shape[-1]
            part = _dot(y_ref[rows, :], w_out_ref[row:row + width, :])
            mix = part if mix is None else mix + part
            row += width
        mixes.append(mix)
    h1s = [_layernorm(DN_ALPHA * h_ref[rows, :] + mix, ln1_g[...], ln1_b[...]) for rows, mix in zip(spans, mixes)]
    fs = [_ffn(h1, w1_ref, w2_ref) for h1 in h1s]
    for rows, h1, f in zip(spans, h1s, fs):
        o_ref[rows, :] = _layernorm(DN_ALPHA * h1 + f, ln2_g[...], ln2_b[...])


def _ret_proj_kernel(h_ref, w_ref, cos_ref, sin_ref, q_ref, k_ref, v_ref, g_ref):
    hb = h_ref[...].astype(BF16)
    qk = RET_HEADS * RET_QK_DIM
    half = RET_QK_DIM // 2
    k_scale = RET_QK_DIM ** -0.5
    mixw = RET_HEADS * RET_V_DIM
    half_g = 0.5 * _dot(hb, w_ref[:, 2 * qk + mixw:])
    g_ref[...] = (half_g + half_g * jnp.tanh(half_g)).astype(BF16)
    for out_ref, base, scale in ((q_ref, 0, None), (k_ref, qk, k_scale)):
        cos, sin = cos_ref[...], sin_ref[...]
        if scale is not None:
            cos, sin = cos * scale, sin * scale
        pr = _dot(hb, w_ref[:, base:base + qk])
        parts = []
        for hh in range(RET_HEADS):
            x1 = pr[:, hh * RET_QK_DIM:hh * RET_QK_DIM + half]
            x2 = pr[:, hh * RET_QK_DIM + half:(hh + 1) * RET_QK_DIM]
            parts.append((x1 * cos - x2 * sin).astype(BF16))
            parts.append((x1 * sin + x2 * cos).astype(BF16))
        out_ref[...] = jnp.concatenate(parts, axis=1)
    v_ref[...] = _dot(hb, w_ref[:, 2 * qk:2 * qk + mixw]).astype(BF16)


def _retention_kernel(q_ref, k_ref, v_ref, g_ref, s0_ref, lg_ref, y_ref, s_scr):
    c = RET_CHUNK
    log_gamma = lg_ref[0][:, 0:1]
    ii = lax.broadcasted_iota(jnp.int32, (c, c), 0)
    jj = lax.broadcasted_iota(jnp.int32, (c, c), 1)
    diff = (ii - jj).astype(F32)
    decay = jnp.where(diff >= 0, jnp.exp(log_gamma * jnp.maximum(diff, 0.0)), 0.0)
    idx = lax.broadcasted_iota(jnp.int32, (c, 1), 0).astype(F32)
    q_decay = jnp.exp(log_gamma * (idx + 1.0))
    k_decay = jnp.exp(log_gamma * (c - 1.0 - idx))
    chunk_decay = jnp.exp(log_gamma * c)
    s_scr[...] = s0_ref[0]

    def intra_scores(ci):
        rows = slice(ci * c, (ci + 1) * c)
        return (_dot_nt(q_ref[0, rows, :], k_ref[0, rows, :]) * decay).astype(BF16)

    def recur(ci, scores):
        rows = slice(ci * c, (ci + 1) * c)
        q = q_ref[0, rows, :]
        k = k_ref[0, rows, :]
        v = v_ref[0, rows, :]
        s_prev = s_scr[...]
        qd = (q.astype(F32) * q_decay).astype(BF16)
        o = _dot(jnp.concatenate([scores, qd], axis=1), jnp.concatenate([v, s_prev.astype(BF16)], axis=0))
        kd = (k.astype(F32) * k_decay).astype(BF16)
        s_scr[...] = chunk_decay * s_prev + _dot_tn(kd, v)
        return o

    def finish(ci, o):
        rows = slice(ci * c, (ci + 1) * c)
        o = o * lax.rsqrt(jnp.mean(o * o, axis=-1, keepdims=True) + EPS)
        y_ref[0, rows, :] = (g_ref[0, rows, :].astype(F32) * o).astype(BF16)

    n_chunks = q_ref.shape[1] // c
    o_prev = None
    scores = intra_scores(0)
    for ci in range(n_chunks):
        nxt = intra_scores(ci + 1) if ci + 1 < n_chunks else None
        o = recur(ci, scores)
        if o_prev is not None:
            finish(ci - 1, o_prev)
        o_prev, scores = o, nxt
    finish(n_chunks - 1, o_prev)


def _const_spec(shape):
    zeros = (0,) * len(shape)
    return pl.BlockSpec(shape, lambda *_: zeros, pipeline_mode=pl.Buffered(1))


def _params(*semantics):
    return pltpu.CompilerParams(dimension_semantics=semantics, vmem_limit_bytes=VMEM_LIMIT)


def _rope_tables(positions, half):
    inv = ROPE_BASE ** (-jnp.arange(half, dtype=F32) / half)
    ang = positions.astype(F32)[:, None] * inv[None, :]
    return jnp.cos(ang), jnp.sin(ang)


def _mla_rope_tables(positions):
    cos, sin = _rope_tables(positions, MLA_ROPE // 2)
    n = positions.shape[0]
    h = MLA_ROPE // 2
    ones = jnp.ones((n, MLA_NOPE), F32)
    zeros = lambda w: jnp.zeros((n, w), F32)
    c = jnp.concatenate([ones, cos, cos, zeros(LANE - MLA_NOPE - MLA_ROPE)], axis=1)
    s_up = jnp.concatenate([zeros(MLA_NOPE + h), sin, zeros(LANE - MLA_NOPE - MLA_ROPE)], axis=1)
    s_dn = jnp.concatenate([zeros(MLA_NOPE), -sin, zeros(LANE - MLA_NOPE - h)], axis=1)
    return c, s_up, s_dn


def _layer_spec(stacked_shape, layer):
    return pl.BlockSpec((None,) + tuple(stacked_shape[1:]), lambda *_: (layer, 0, 0),
                        pipeline_mode=pl.Buffered(1))


def _mix_mlp_call(name, layer, h, ys, w_out, ln1_g, ln1_b, w1, w2, ln2_g, ln2_b):
    m = h.shape[0]
    tm = ROW_TILE
    row_spec = lambda w: pl.BlockSpec((tm, w), lambda i: (i, 0))
    vec = _const_spec((1, D_MODEL))
    return pl.pallas_call(
        functools.partial(_mix_mlp_kernel, n_mix=len(ys)),
        grid=(m // tm,),
        in_specs=[row_spec(D_MODEL)] + [row_spec(y.shape[1]) for y in ys] + [
            _const_spec(w_out.shape), vec, vec, _layer_spec(w1.shape, layer), _layer_spec(w2.shape, layer),
            vec, vec],
        out_specs=row_spec(D_MODEL),
        out_shape=jax.ShapeDtypeStruct((m, D_MODEL), F32),
        compiler_params=_params("parallel"),
        name=name,
    )(h, *ys, w_out, ln1_g, ln1_b, w1, w2, ln2_g, ln2_b)


def kernel(x, meta_tokens, ev_w_in, ev_conv_w, ev_conv_b, ev_w_rg_a, ev_b_rg_a, ev_w_rg_x, ev_b_rg_x,
           ev_lru_lambda, ev_q_norm_g, ev_w_uq, ev_kv_norm_g, ev_w_ukv, ev_w_out, od_w_in, od_w_out,
           ln_mix_g, ln_mix_b, mlp_w1, mlp_w2, ln_mlp_g, ln_mlp_b):
    bsz, seq, _ = x.shape
    row = lambda v: v.reshape(1, -1).astype(F32)

    w_in0 = ev_w_in[0]
    lat0 = 2 * LRU_WIDTH
    kpe0 = lat0 + MLA_Q_RANK + MLA_KV_RANK
    w_kpe = jnp.zeros((D_MODEL, LANE), F32).at[:, MLA_NOPE:MLA_NOPE + MLA_ROPE].set(w_in0[:, kpe0:])
    w_in_ev = jnp.concatenate([w_in0[:, :kpe0], w_kpe], axis=1).astype(BF16)
    gate_w = jnp.concatenate([ev_w_rg_a[0], ev_w_rg_x[0]], axis=2).astype(BF16)
    w_uq = ev_w_uq[0].reshape(MLA_Q_RANK, MLA_HEADS, MLA_NOPE + MLA_ROPE)
    w_uq = jnp.pad(w_uq, ((0, 0), (0, 0), (0, LANE - MLA_NOPE - MLA_ROPE)))
    w_uq = w_uq.reshape(MLA_Q_RANK, MLA_HEADS * LANE).astype(BF16)
    w_ukv = ev_w_ukv[0].reshape(MLA_KV_RANK, MLA_HEADS, MLA_NOPE + MLA_V)
    w_uk = jnp.pad(w_ukv[:, :, :MLA_NOPE], ((0, 0), (0, 0), (0, LANE - MLA_NOPE)))
    w_uk = w_uk.reshape(MLA_KV_RANK, MLA_HEADS * LANE)
    w_uv = w_ukv[:, :, MLA_NOPE:].reshape(MLA_KV_RANK, MLA_HEADS * MLA_V)
    w_ukv_p = jnp.concatenate([w_uk, w_uv], axis=1).astype(BF16)
    w_uq_t = w_uq.T
    w_uk = w_uk.astype(BF16)
    w_uv_t = w_uv.T.astype(BF16)
    w_out_ev = ev_w_out[0].astype(BF16)
    w_in_od = od_w_in[0].astype(BF16)
    w_out_od = od_w_out[0].astype(BF16)
    w1 = mlp_w1.astype(BF16)
    w2 = mlp_w2.astype(BF16)
    conv_w = ev_conv_w[0].astype(F32)
    conv_b, b_a, b_x, lam = row(ev_conv_b[0]), row(ev_b_rg_a[0]), row(ev_b_rg_x[0]), row(ev_lru_lambda[0])
    q_g, kv_g = row(ev_q_norm_g[0]), row(ev_kv_norm_g[0])
    ln = [(row(ln_mix_g[l]), row(ln_mix_b[l]), row(ln_mlp_g[l]), row(ln_mlp_b[l])) for l in range(2)]

    pos = jnp.arange(N_META + seq, dtype=jnp.int32)
    mla_tabs = _mla_rope_tables(pos)
    cos0, sin0 = _rope_tables(pos, MLA_ROPE // 2)
    cos0_t, sin0_t = cos0.T, sin0.T
    cos1, sin1 = _rope_tables(pos, RET_QK_DIM // 2)
    k_scale = RET_QK_DIM ** -0.5
    qk = RET_HEADS * RET_QK_DIM
    log_gamma = jnp.log(1.0 - 2.0 ** (-5.0 - jnp.arange(RET_HEADS, dtype=F32)))
    log_gamma = jnp.broadcast_to(log_gamma[:, None, None], (RET_HEADS, 1, LANE))

    mixw = RET_HEADS * RET_V_DIM
    meta_out_shapes = (jax.ShapeDtypeStruct((SUBLANE, LRU_WIDTH), F32),
                       jax.ShapeDtypeStruct((SUBLANE, LRU_WIDTH), F32),
                       jax.ShapeDtypeStruct((N_META, MLA_HEADS * LANE), BF16),
                       jax.ShapeDtypeStruct((MLA_HEADS * MLA_V, N_META), BF16),
                       jax.ShapeDtypeStruct((RET_HEADS, RET_QK_DIM, RET_V_DIM), F32))
    meta_args = [meta_tokens.astype(F32), w_in_ev, conv_w, conv_b, gate_w, b_a, b_x, lam, q_g, kv_g, w_uq,
                 w_ukv_p, w_uv_t, *[t[:N_META] for t in mla_tabs], w_out_ev, *ln[0][:2]]
    meta_specs = [_const_spec(a.shape) for a in meta_args]
    meta_args += [w1, w2, *ln[0][2:], w_in_od, w_in_od]
    meta_specs += [_layer_spec(w1.shape, 0), _layer_spec(w2.shape, 0), _const_spec(ln[0][2].shape),
                   _const_spec(ln[0][3].shape),
                   pl.BlockSpec((D_MODEL, qk), lambda i: (0, 1), pipeline_mode=pl.Buffered(1)),
                   pl.BlockSpec((D_MODEL, mixw), lambda i: (0, 1), pipeline_mode=pl.Buffered(1))]
    tail_args = [cos1[:N_META] * k_scale, sin1[:N_META] * k_scale, log_gamma]
    meta_args += tail_args
    meta_specs += [_const_spec(a.shape) for a in tail_args]
    meta_out = pl.pallas_call(
        _meta_kernel,
        grid=(1,),
        in_specs=meta_specs,
        out_specs=[pl.BlockSpec(s.shape, lambda i, nd=len(s.shape): (0,) * nd) for s in meta_out_shapes],
        out_shape=meta_out_shapes,
        scratch_shapes=[pltpu.VMEM((SUBLANE + N_META, LRU_WIDTH), F32),
                        pltpu.VMEM((N_META, LRU_WIDTH), F32),
                        pltpu.VMEM((N_META, LRU_WIDTH), F32)],
        compiler_params=_params("arbitrary"),
        name="meta_tokens",
    )(*meta_args)
    rec_tail, h_tail, k_meta, vt_meta, s_meta = meta_out

    ts = SEQ_TILE
    nt = seq // ts
    tab_spec = pl.BlockSpec((ts, LANE), lambda b, t: (t, 0))
    tab_t_spec = pl.BlockSpec((MLA_ROPE // 2, ts), lambda b, t: (0, t))
    seq_spec = lambda w: pl.BlockSpec((1, ts, w), lambda b, t: (b, t, 0))
    seq_t_spec = lambda w: pl.BlockSpec((1, 1, w, ts), lambda b, t: (b, t, 0, 0))
    y_rec, qt0, k0, vt0 = pl.pallas_call(
        _seq0_kernel,
        grid=(bsz, nt),
        in_specs=[seq_spec(D_MODEL), _const_spec(w_in_ev.shape), _const_spec(conv_w.shape),
                  _const_spec(conv_b.shape), _const_spec(gate_w.shape), _const_spec(b_a.shape),
                  _const_spec(b_x.shape), _const_spec(lam.shape), _const_spec(q_g.shape),
                  _const_spec(kv_g.shape), _const_spec(w_uq_t.shape), _const_spec(w_uk.shape),
                  _const_spec(w_uv_t.shape), tab_spec, tab_spec, tab_spec, tab_t_spec, tab_t_spec,
                  _const_spec(rec_tail.shape), _const_spec(h_tail.shape)],
        out_specs=[seq_spec(LRU_WIDTH), seq_t_spec(MLA_HEADS * LANE), seq_spec(MLA_HEADS * LANE),
                   seq_t_spec(MLA_HEADS * MLA_V)],
        out_shape=(jax.ShapeDtypeStruct((bsz, seq, LRU_WIDTH), BF16),
                   jax.ShapeDtypeStruct((bsz, nt, MLA_HEADS * LANE, ts), BF16),
                   jax.ShapeDtypeStruct((bsz, seq, MLA_HEADS * LANE), BF16),
                   jax.ShapeDtypeStruct((bsz, nt, MLA_HEADS * MLA_V, ts), BF16)),
        scratch_shapes=[pltpu.VMEM((SUBLANE + ts // SEQ_SPLIT, LRU_WIDTH), F32),
                        pltpu.VMEM((LRU_WIDTH // LANE, ts, LANE), F32),
                        pltpu.VMEM((LRU_WIDTH // LANE, ts, LANE), F32),
                        pltpu.VMEM((ts, LRU_WIDTH), F32),
                        pltpu.VMEM((SUBLANE, LRU_WIDTH), F32)],
        compiler_params=_params("parallel", "arbitrary"),
        name="seq0_mixer_proj",
    )(x, w_in_ev, conv_w, conv_b, gate_w, b_a, b_x, lam, q_g, kv_g, w_uq_t, w_uk, w_uv_t,
      *[t[N_META:] for t in mla_tabs], cos0_t[:, N_META:], sin0_t[:, N_META:], rec_tail, h_tail)

    tq = ATT_TILE
    assert tq == ts
    y_att = pl.pallas_call(
        _attn_kernel,
        grid=(bsz, MLA_HEADS // 2, seq // tq),
        in_specs=[pl.BlockSpec((1, 1, 2 * LANE, tq), lambda b, j, i: (b, i, j, 0)),
                  pl.BlockSpec((1, seq, 2 * LANE), lambda b, j, i: (b, 0, j)),
                  pl.BlockSpec((1, nt, LANE, tq), lambda b, j, i: (b, 0, j, 0)),
                  pl.BlockSpec((N_META, 2 * LANE), lambda b, j, i: (0, j)),
                  pl.BlockSpec((LANE, N_META), lambda b, j, i: (j, 0))],
        out_specs=pl.BlockSpec((1, tq, LANE), lambda b, j, i: (b, i, j)),
        out_shape=jax.ShapeDtypeStruct((bsz, seq, MLA_HEADS * MLA_V), BF16),
        scratch_shapes=[pltpu.VMEM((2, 1, tq), F32), pltpu.VMEM((2, MLA_V + ATT_ONES, tq), F32)],
        compiler_params=_params("parallel", "parallel", "arbitrary"),
        name="mla_attention",
    )(qt0, k0, vt0, k_meta, vt_meta)

    m = bsz * seq
    h1 = _mix_mlp_call("layer0_out_mlp", 0, x.reshape(m, D_MODEL),
                       [y_rec.reshape(m, -1), y_att.reshape(m, -1)], w_out_ev,
                       ln[0][0], ln[0][1], w1, w2, ln[0][2], ln[0][3])

    tm = ROW_TILE
    per_seq = seq // tm
    row_spec = lambda w: pl.BlockSpec((tm, w), lambda i: (i, 0))
    rope_spec = pl.BlockSpec((tm, RET_QK_DIM // 2), lambda i: (i % per_seq, 0))
    mixw = RET_HEADS * RET_V_DIM
    q1, k1, v1, g1 = pl.pallas_call(
        _ret_proj_kernel,
        grid=(m // tm,),
        in_specs=[row_spec(D_MODEL), _const_spec(w_in_od.shape), rope_spec, rope_spec],
        out_specs=[row_spec(qk), row_spec(qk), row_spec(mixw), row_spec(mixw)],
        out_shape=(jax.ShapeDtypeStruct((m, qk), BF16), jax.ShapeDtypeStruct((m, qk), BF16),
                   jax.ShapeDtypeStruct((m, mixw), BF16), jax.ShapeDtypeStruct((m, mixw), BF16)),
        compiler_params=_params("parallel"),
        name="layer1_in_proj",
    )(h1, w_in_od, cos1[N_META:], sin1[N_META:])

    head_spec = lambda w: pl.BlockSpec((1, seq, w), lambda b, h: (b, 0, h))
    y_ret = pl.pallas_call(
        _retention_kernel,
        grid=(bsz, RET_HEADS),
        in_specs=[head_spec(RET_QK_DIM), head_spec(RET_QK_DIM), head_spec(RET_V_DIM), head_spec(RET_V_DIM),
                  pl.BlockSpec((1, RET_QK_DIM, RET_V_DIM), lambda b, h: (h, 0, 0)),
                  pl.BlockSpec((1, 1, LANE), lambda b, h: (h, 0, 0))],
        out_specs=head_spec(RET_V_DIM),
        out_shape=jax.ShapeDtypeStruct((bsz, seq, mixw), BF16),
        scratch_shapes=[pltpu.VMEM((RET_QK_DIM, RET_V_DIM), F32)],
        compiler_params=_params("parallel", "parallel"),
        name="retention",
    )(q1.reshape(bsz, seq, qk), k1.reshape(bsz, seq, qk), v1.reshape(bsz, seq, mixw),
      g1.reshape(bsz, seq, mixw), s_meta, log_gamma)

    out = _mix_mlp_call("layer1_out_mlp", 1, h1, [y_ret.reshape(m, mixw)], w_out_od,
                        ln[1][0], ln[1][1], w1, w2, ln[1][2], ln[1][3])
    return out.reshape(bsz, seq, D_MODEL)
```

```python
import functools
import math

import jax
import jax.numpy as jnp
import numpy as np
from jax import lax
from jax.experimental import pallas as pl
from jax.experimental.pallas import tpu as pltpu

D_MODEL = 1024
N_META = 16
LRU_WIDTH = 512
LRU_HEADS = 4
LRU_HEAD_DIM = 128
CONV_WIDTH = 4
LRU_C = 8.0
MLA_HEADS = 8
MLA_NOPE = 64
MLA_ROPE = 32
MLA_V = 64
MLA_Q_RANK = 256
MLA_KV_RANK = 128
RET_HEADS = 4
RET_QK_DIM = 256
RET_V_DIM = 512
D_FF = 4096
ROPE_BASE = 10000.0
DN_ALPHA = 4.0 ** 0.25
EPS = 1e-5
NEG_INF = -1e30

LANE = 128
SUBLANE = 8
VMEM_LIMIT = 56 * 1024 * 1024

BF16 = jnp.bfloat16
F32 = jnp.float32

SEQ_TILE = 512
SEQ_SPLIT = 2
SCAN_STEP = 4
SCAN_BLOCK = SUBLANE * SCAN_STEP
ATT_TILE = 512
ATT_SPLIT = 2
ATT_ONES = 16
ROW_TILE = 512
ROW_SPLIT = 2
FF_TILE = 1024
RET_CHUNK = 256


def _dot(a, b):
    return jnp.dot(a, b, preferred_element_type=F32)


def _dot_nt(a, b):
    return lax.dot_general(a, b, (((1,), (1,)), ((), ())), preferred_element_type=F32)


def _dot_tn(a, b):
    return lax.dot_general(a, b, (((0,), (0,)), ((), ())), preferred_element_type=F32)


def _layernorm(x, g, b):
    mu = jnp.mean(x, axis=-1, keepdims=True)
    xc = x - mu
    var = jnp.mean(xc * xc, axis=-1, keepdims=True)
    return xc * lax.rsqrt(var + EPS) * g + b


def _rmsnorm(x, g):
    return x * lax.rsqrt(jnp.mean(x * x, axis=-1, keepdims=True) + EPS) * g


def _rope_mla(x, c, s_up, s_dn):
    return x * c + pltpu.roll(x, MLA_ROPE // 2, 1) * s_up + pltpu.roll(x, LANE - MLA_ROPE // 2, 1) * s_dn


def _lru_gates(xc, gate_w_ref, b_a, b_x, sp_lambda):
    rs, is_ = [], []
    for h in range(LRU_HEADS):
        g = _dot(xc[:, h * LRU_HEAD_DIM:(h + 1) * LRU_HEAD_DIM].astype(BF16), gate_w_ref[h])
        rs.append(g[:, :LRU_HEAD_DIM])
        is_.append(g[:, LRU_HEAD_DIM:])
    r = jax.nn.sigmoid(jnp.concatenate(rs, axis=1) + b_a)
    i = jax.nn.sigmoid(jnp.concatenate(is_, axis=1) + b_x)
    log_a = -LRU_C * r * sp_lambda
    a = jnp.exp(log_a)
    y = 1.0 - a * a
    mult = jnp.where(y > 0.0, y * lax.rsqrt(y), 0.0)
    return a, mult * (i * xc)


def _scan8(a, b):
    row = lax.broadcasted_iota(jnp.int32, a.shape, 0)
    for k in (1, 2, 4):
        keep = row >= k
        a_prev = jnp.where(keep, pltpu.roll(a, k, 0), 1.0)
        b_prev = jnp.where(keep, pltpu.roll(b, k, 0), 0.0)
        b = a * b_prev + b
        a = a * a_prev
    return a, b


def _lru_scan(a_ref, b_ref, h0, rows):
    def body(g, h_prev):
        sl = pl.ds(pl.multiple_of(g * SUBLANE, SUBLANE), SUBLANE)
        a_c, b_c = _scan8(a_ref[sl, :], b_ref[sl, :])
        h = a_c * h_prev + b_c
        b_ref[sl, :] = h
        return h[SUBLANE - 1:SUBLANE, :]
    return lax.fori_loop(0, rows // SUBLANE, body, h0, unroll=4)


def _lru_scan_blocked(a_ref, b_ref, h0, rows):
    n_slabs = a_ref.shape[0]
    row = lax.broadcasted_iota(jnp.int32, (SUBLANE, LANE), 0)

    def body(g, carry):
        base = g * SCAN_BLOCK
        out = []
        for c in range(n_slabs):
            idx = [pl.ds(base + l, SUBLANE, stride=SCAN_STEP) for l in range(SCAN_STEP)]
            a = [a_ref[c, i, :] for i in idx]
            h = [b_ref[c, idx[0], :]]
            p = [a[0]]
            for l in range(1, SCAN_STEP):
                h.append(a[l] * h[l - 1] + b_ref[c, idx[l], :])
                p.append(a[l] * p[l - 1])
            p_seg, h_seg = _scan8(p[-1], h[-1])
            end = p_seg * carry[c] + h_seg
            start = jnp.where(row == 0, carry[c], pltpu.roll(end, 1, 0))
            for l in range(SCAN_STEP):
                b_ref[c, idx[l], :] = h[l] + p[l] * start
            out.append(end[SUBLANE - 1:SUBLANE, :])
        return tuple(out)
    return lax.fori_loop(0, rows // SCAN_BLOCK, body, tuple(h0), unroll=2)


def _mla_project(qlat, kvlat, kpe, q_g, kv_g, w_uq_ref, w_ukv_ref, rope_c, rope_up, rope_dn):
    scale = (MLA_NOPE + MLA_ROPE) ** -0.5
    q_all = _dot(_rmsnorm(qlat, q_g).astype(BF16), w_uq_ref[...])
    kv_all = _dot(_rmsnorm(kvlat, kv_g).astype(BF16), w_ukv_ref[...])
    kpe_r = _rope_mla(kpe, rope_c, rope_up, rope_dn)
    qs, ks = [], []
    for h in range(MLA_HEADS):
        sl = slice(h * LANE, (h + 1) * LANE)
        qs.append((_rope_mla(q_all[:, sl], rope_c, rope_up, rope_dn) * scale).astype(BF16))
        ks.append((kv_all[:, sl] + kpe_r).astype(BF16))
    v = kv_all[:, MLA_HEADS * LANE:].astype(BF16)
    return jnp.concatenate(qs, axis=1), jnp.concatenate(ks, axis=1), v


def _ffn(h1, w1_ref, w2_ref):
    h1b = h1.astype(BF16)
    f = None
    for c in range(D_FF // FF_TILE):
        a = _dot(h1b, w1_ref[:, c * FF_TILE:(c + 1) * FF_TILE])
        a = jnp.maximum(a, 0.0)
        part = _dot((a * a).astype(BF16), w2_ref[c * FF_TILE:(c + 1) * FF_TILE, :])
        f = part if f is None else f + part
    return f


def _mlp_block(h_in, mix, ln1_g, ln1_b, w1_ref, w2_ref, ln2_g, ln2_b):
    h1 = _layernorm(DN_ALPHA * h_in + mix, ln1_g, ln1_b)
    return _layernorm(DN_ALPHA * h1 + _ffn(h1, w1_ref, w2_ref), ln2_g, ln2_b)


def _meta_kernel(meta_ref, w_in_ref, conv_w_ref, conv_b_ref, gate_w_ref, b_a_ref, b_x_ref, lam_ref,
                 q_g_ref, kv_g_ref, w_uq_ref, w_ukv_ref, w_uvt_ref, rope_c_ref, rope_up_ref, rope_dn_ref,
                 w_out_ref, ln1_g_ref, ln1_b_ref, w1_ref, w2_ref, ln2_g_ref, ln2_b_ref,
                 w_k_ref, w_v_ref, cos1_ref, sin1_ref, lg_ref,
                 rec_tail_ref, h_tail_ref, k_meta_ref, vt_meta_ref, s_meta_ref,
                 conv_scr, a_scr, b_scr):
    n = N_META
    x = meta_ref[...]
    p = _dot(x.astype(BF16), w_in_ref[...])
    gate, rec = p[:, :LRU_WIDTH], p[:, LRU_WIDTH:2 * LRU_WIDTH]
    conv_scr[0:SUBLANE, :] = jnp.zeros((SUBLANE, LRU_WIDTH), F32)
    conv_scr[SUBLANE:SUBLANE + n, :] = rec
    cw = conv_w_ref[...]
    xc = conv_b_ref[...] + cw[3:4, :] * rec
    for j in range(CONV_WIDTH - 1):
        off = SUBLANE - (CONV_WIDTH - 1) + j
        xc = xc + cw[j:j + 1, :] * conv_scr[off:off + n, :]
    sp_lambda = jax.nn.softplus(-lam_ref[...])
    a, b = _lru_gates(xc, gate_w_ref, b_a_ref[...], b_x_ref[...], sp_lambda)
    a_scr[...] = a
    b_scr[...] = b
    _lru_scan(a_scr, b_scr, jnp.zeros((1, LRU_WIDTH), F32), n)
    h = b_scr[...]
    y_rec = (h * jax.nn.gelu(gate)).astype(BF16)
    rec_tail_ref[...] = rec[n - SUBLANE:, :]
    h_tail_ref[...] = h[n - SUBLANE:, :]

    off = 2 * LRU_WIDTH
    q, k, v = _mla_project(p[:, off:off + MLA_Q_RANK],
                           p[:, off + MLA_Q_RANK:off + MLA_Q_RANK + MLA_KV_RANK],
                           p[:, off + MLA_Q_RANK + MLA_KV_RANK:],
                           q_g_ref[...], kv_g_ref[...], w_uq_ref, w_ukv_ref,
                           rope_c_ref[...], rope_up_ref[...], rope_dn_ref[...])
    k_meta_ref[...] = k
    kvn = _rmsnorm(p[:, off + MLA_Q_RANK:off + MLA_Q_RANK + MLA_KV_RANK], kv_g_ref[...]).astype(BF16)
    vt_meta_ref[...] = _dot_nt(w_uvt_ref[...], kvn).astype(BF16)
    causal = (lax.broadcasted_iota(jnp.int32, (n, n), 1) <= lax.broadcasted_iota(jnp.int32, (n, n), 0))
    outs = []
    for hh in range(MLA_HEADS):
        sl = slice(hh * LANE, (hh + 1) * LANE)
        s = jnp.where(causal, _dot_nt(q[:, sl], k[:, sl]), NEG_INF)
        e = jnp.exp(s - jnp.max(s, axis=-1, keepdims=True))
        pr = e / jnp.sum(e, axis=-1, keepdims=True)
        outs.append(_dot(pr.astype(BF16), v[:, hh * MLA_V:(hh + 1) * MLA_V]))
    y_att = jnp.concatenate(outs, axis=1).astype(BF16)
    mix = _dot(y_rec, w_out_ref[0:LRU_WIDTH, :]) + _dot(y_att, w_out_ref[LRU_WIDTH:, :])
    h2 = _mlp_block(x, mix, ln1_g_ref[...], ln1_b_ref[...], w1_ref, w2_ref, ln2_g_ref[...], ln2_b_ref[...])

    h2b = h2.astype(BF16)
    kk = _dot(h2b, w_k_ref[...])
    vv = _dot(h2b, w_v_ref[...]).astype(BF16)
    cos, sin = cos1_ref[...], sin1_ref[...]
    idx = lax.broadcasted_iota(jnp.int32, (n, 1), 0).astype(F32)
    half = RET_QK_DIM // 2
    for hh in range(RET_HEADS):
        log_gamma = lg_ref[hh][:, 0:1]
        k1 = kk[:, hh * RET_QK_DIM:hh * RET_QK_DIM + half]
        k2 = kk[:, hh * RET_QK_DIM + half:(hh + 1) * RET_QK_DIM]
        kr = jnp.concatenate([k1 * cos - k2 * sin, k1 * sin + k2 * cos], axis=1).astype(BF16)
        k_dec = jnp.exp(log_gamma * (n - 1.0 - idx))
        kd = (kr.astype(F32) * k_dec).astype(BF16)
        s_meta_ref[hh] = _dot_tn(kd, vv[:, hh * RET_V_DIM:(hh + 1) * RET_V_DIM])


def _seq0_kernel(x_ref, w_in_ref, conv_w_ref, conv_b_ref, gate_w_ref, b_a_ref, b_x_ref, lam_ref,
                 q_g_ref, kv_g_ref, w_uqt_ref, w_uk_ref, w_uvt_ref, rope_c_ref, rope_up_ref, rope_dn_ref,
                 cos_t_ref, sin_t_ref, rec_tail_ref, h_tail_ref,
                 y_rec_ref, qt_ref, k_ref, vt_ref,
                 conv_scr, a_scr, b_scr, g_scr, h_scr):
    ts = SEQ_TILE
    t = pl.program_id(1)

    @pl.when(t == 0)
    def _():
        conv_scr[0:SUBLANE, :] = rec_tail_ref[...]
        h_scr[...] = h_tail_ref[...]

    sub = ts // SEQ_SPLIT
    spans = [slice(i * sub, (i + 1) * sub) for i in range(SEQ_SPLIT)]
    ps = [_dot(x_ref[0, rows, :].astype(BF16), w_in_ref[...]) for rows in spans]
    cw = conv_w_ref[...]
    sp_lambda = jax.nn.softplus(-lam_ref[...])
    scale = (MLA_NOPE + MLA_ROPE) ** -0.5 * math.log2(math.e)
    hr = MLA_ROPE // 2
    off = 2 * LRU_WIDTH
    for rows, p in zip(spans, ps):
        gate, rec = p[:, :LRU_WIDTH], p[:, LRU_WIDTH:2 * LRU_WIDTH]
        conv_scr[SUBLANE:SUBLANE + sub, :] = rec
        xc = conv_b_ref[...] + cw[3:4, :] * rec
        for j in range(CONV_WIDTH - 1):
            o = SUBLANE - (CONV_WIDTH - 1) + j
            xc = xc + cw[j:j + 1, :] * conv_scr[o:o + sub, :]
        conv_scr[0:SUBLANE, :] = rec[sub - SUBLANE:, :]
        a, b = _lru_gates(xc, gate_w_ref, b_a_ref[...], b_x_ref[...], sp_lambda)
        for c in range(LRU_WIDTH // LANE):
            a_scr[c, rows, :] = a[:, c * LANE:(c + 1) * LANE]
            b_scr[c, rows, :] = b[:, c * LANE:(c + 1) * LANE]
        g_scr[rows, :] = jax.nn.gelu(gate)

        qn = _rmsnorm(p[:, off:off + MLA_Q_RANK], q_g_ref[...]).astype(BF16)
        q_t = _dot_nt(w_uqt_ref[...], qn)
        cos_t, sin_t = cos_t_ref[:, rows] * scale, sin_t_ref[:, rows] * scale
        for h in range(MLA_HEADS):
            base = h * LANE
            x1 = q_t[base + MLA_NOPE:base + MLA_NOPE + hr, :]
            x2 = q_t[base + MLA_NOPE + hr:base + MLA_NOPE + MLA_ROPE, :]
            qt_ref[0, 0, base:base + MLA_NOPE, rows] = (q_t[base:base + MLA_NOPE, :] * scale).astype(BF16)
            qt_ref[0, 0, base + MLA_NOPE:base + MLA_NOPE + hr, rows] = (x1 * cos_t - x2 * sin_t).astype(BF16)
            qt_ref[0, 0, base + MLA_NOPE + hr:base + MLA_NOPE + MLA_ROPE, rows] = (
                x1 * sin_t + x2 * cos_t).astype(BF16)
            qt_ref[0, 0, base + MLA_NOPE + MLA_ROPE:base + LANE, rows] = jnp.zeros(
                (LANE - MLA_NOPE - MLA_ROPE, sub), BF16)
        kvn = _rmsnorm(p[:, off + MLA_Q_RANK:off + MLA_Q_RANK + MLA_KV_RANK], kv_g_ref[...]).astype(BF16)
        k_nope = _dot(kvn, w_uk_ref[...])
        kpe_r = _rope_mla(p[:, off + MLA_Q_RANK + MLA_KV_RANK:],
                          rope_c_ref[rows, :], rope_up_ref[rows, :], rope_dn_ref[rows, :])
        for h in range(MLA_HEADS):
            sl = slice(h * LANE, (h + 1) * LANE)
            k_ref[0, rows, sl] = (k_nope[:, sl] + kpe_r).astype(BF16)
        vt_ref[0, 0, :, rows] = _dot_nt(w_uvt_ref[...], kvn).astype(BF16)

    n_slabs = LRU_WIDTH // LANE
    h0 = [h_scr[SUBLANE - 1:SUBLANE, c * LANE:(c + 1) * LANE] for c in range(n_slabs)]
    h_last = _lru_scan_blocked(a_scr, b_scr, h0, ts)
    h_scr[SUBLANE - 1:SUBLANE, :] = jnp.concatenate(h_last, axis=1)
    h = jnp.concatenate([b_scr[c] for c in range(n_slabs)], axis=1)
    y_rec_ref[0] = (h * g_scr[...]).astype(BF16)


def _attn_kernel(qt_ref, k_ref, vt_ref, k_meta_ref, vt_meta_ref, o_ref, m_scr, acc_scr):
    tq = ATT_TILE
    tw = tq // ATT_SPLIT
    qi = pl.program_id(2)
    chains = [(hh, slice(hh * LANE, (hh + 1) * LANE), slice(hh * MLA_V, (hh + 1) * MLA_V),
               part, slice(part * tw, (part + 1) * tw))
              for hh in range(2) for part in range(ATT_SPLIT)]
    def with_ones(v_t):
        return jnp.concatenate([v_t, jnp.ones((ATT_ONES, v_t.shape[1]), BF16)], axis=0)

    def n_keys(part, diagonal):
        return (part + 1) * tw if diagonal else tq

    def meta_scores():
        return [_dot(k_meta_ref[:, sl], qt_ref[0, 0, sl, cols]) for _, sl, _, _, cols in chains]

    def meta_softmax_pv(scores):
        for (hh, sl, vrows, part, cols), s in zip(chains, scores):
            m = jnp.max(s, axis=0, keepdims=True)
            e = jnp.exp2(s - m)
            m_scr[hh, :, cols] = m
            acc_scr[hh, :, cols] = _dot(with_ones(vt_meta_ref[vrows, :]), e.astype(BF16))

    def score(chain, kj, diagonal):
        _, sl, _, part, cols = chain
        return _dot(k_ref[0, kj * tq:kj * tq + n_keys(part, diagonal), sl], qt_ref[0, 0, sl, cols])

    def softmax_pv(chain, kj, s, diagonal):
        hh, sl, vrows, part, cols = chain
        nk = n_keys(part, diagonal)
        if diagonal:
            key = lax.broadcasted_iota(jnp.int32, (nk, tw), 0)
            qry = lax.broadcasted_iota(jnp.int32, (nk, tw), 1) + part * tw
            s = jnp.where(key <= qry, s, NEG_INF)
        m_old = m_scr[hh, :, cols]
        m_new = jnp.maximum(m_old, jnp.max(s, axis=0, keepdims=True))
        alpha = jnp.exp2(m_old - m_new)
        e = jnp.exp2(s - m_new)
        m_scr[hh, :, cols] = m_new
        acc_scr[hh, :, cols] = alpha * acc_scr[hh, :, cols] + _dot(
            with_ones(vt_ref[0, kj, vrows, 0:nk]), e.astype(BF16))

    def run(n_full):
        s_meta = meta_scores()
        scores = [score(c, 0, n_full == 0) for c in chains]
        meta_softmax_pv(s_meta)
        for kj in range(n_full):
            nxt = [score(c, kj + 1, kj + 1 == n_full) for c in chains]
            for c, s in zip(chains, scores):
                softmax_pv(c, kj, s, False)
            scores = nxt
        for c, s in zip(chains, scores):
            softmax_pv(c, n_full, s, True)

    for n_full in range(k_ref.shape[1] // tq):
        pl.when(qi == n_full)(functools.partial(run, n_full))
    out_t = jnp.concatenate([acc_scr[hh, 0:MLA_V, :] / acc_scr[hh, MLA_V:MLA_V + 1, :] for hh in range(2)],
                            axis=0)
    o_ref[0] = out_t.T.astype(BF16)


def _mix_mlp_kernel(*refs, n_mix):
    h_ref = refs[0]
    y_refs = refs[1:1 + n_mix]
    w_out_ref, ln1_g, ln1_b, w1_ref, w2_ref, ln2_g, ln2_b, o_ref = refs[1 + n_mix:]
    sub = h_ref.shape[0] // ROW_SPLIT
    spans = [slice(s * sub, (s + 1) * sub) for s in range(ROW_SPLIT)]
    mixes = []
    for rows in spans:
        mix = None
        row = 0
        for y_ref in y_refs:
            width = y_ref.shape[-1]
            part = _dot(y_ref[rows, :], w_out_ref[row:row + width, :])
            mix = part if mix is None else mix + part
            row += width
        mixes.append(mix)
    h1s = [_layernorm(DN_ALPHA * h_ref[rows, :] + mix, ln1_g[...], ln1_b[...]) for rows, mix in zip(spans, mixes)]
    fs = [_ffn(h1, w1_ref, w2_ref) for h1 in h1s]
    for rows, h1, f in zip(spans, h1s, fs):
        o_ref[rows, :] = _layernorm(DN_ALPHA * h1 + f, ln2_g[...], ln2_b[...])


def _ret_fused_kernel(h_ref, w_ref, cos_ref, sin_ref, s0_ref, lg_ref, y_ref, s_scr):
    c = RET_CHUNK
    half = RET_QK_DIM // 2
    k_scale = RET_QK_DIM ** -0.5
    log_gamma = lg_ref[0][:, 0:1]
    ii = lax.broadcasted_iota(jnp.int32, (c, c), 0)
    jj = lax.broadcasted_iota(jnp.int32, (c, c), 1)
    diff = (ii - jj).astype(F32)
    decay = jnp.where(diff >= 0, jnp.exp(log_gamma * jnp.maximum(diff, 0.0)), 0.0)
    idx = lax.broadcasted_iota(jnp.int32, (c, 1), 0).astype(F32)
    q_decay = jnp.exp(log_gamma * (idx + 1.0))
    k_decay = jnp.exp(log_gamma * (c - 1.0 - idx))
    chunk_decay = jnp.exp(log_gamma * c)
    s_scr[...] = s0_ref[0]

    def project(ci):
        rows = slice(ci * c, (ci + 1) * c)
        p = _dot(h_ref[0, rows, :].astype(BF16), w_ref[...])
        cos, sin = cos_ref[rows, :], sin_ref[rows, :]
        q1, q2 = p[:, 0:half], p[:, half:2 * half]
        q = jnp.concatenate([q1 * cos - q2 * sin, q1 * sin + q2 * cos], axis=1)
        kcos, ksin = cos * k_scale, sin * k_scale
        k1, k2 = p[:, 2 * half:3 * half], p[:, 3 * half:4 * half]
        k = jnp.concatenate([k1 * kcos - k2 * ksin, k1 * ksin + k2 * kcos], axis=1)
        v = p[:, 2 * RET_QK_DIM:2 * RET_QK_DIM + RET_V_DIM].astype(BF16)
        half_g = 0.5 * p[:, 2 * RET_QK_DIM + RET_V_DIM:]
        gate = (half_g + half_g * jnp.tanh(half_g)).astype(BF16)
        qb, kb = q.astype(BF16), k.astype(BF16)
        scores = (_dot_nt(qb, kb) * decay).astype(BF16)
        return dict(scores=scores, qd=(q * q_decay).astype(BF16), kd=(k * k_decay).astype(BF16), v=v, gate=gate)

    def recur(t):
        s_prev = s_scr[...]
        o = _dot(jnp.concatenate([t["scores"], t["qd"]], axis=1),
                 jnp.concatenate([t["v"], s_prev.astype(BF16)], axis=0))
        s_scr[...] = chunk_decay * s_prev + _dot_tn(t["kd"], t["v"])
        return o

    def finish(ci, o, gate):
        rows = slice(ci * c, (ci + 1) * c)
        o = o * lax.rsqrt(jnp.mean(o * o, axis=-1, keepdims=True) + EPS)
        y_ref[0, rows, :] = (gate.astype(F32) * o).astype(BF16)

    n_chunks = h_ref.shape[1] // c
    cur = project(0)
    prev = None
    for ci in range(n_chunks):
        nxt = project(ci + 1) if ci + 1 < n_chunks else None
        o = recur(cur)
        if prev is not None:
            finish(ci - 1, *prev)
        prev, cur = (o, cur["gate"]), nxt
    finish(n_chunks - 1, *prev)


def _const_spec(shape):
    zeros = (0,) * len(shape)
    return pl.BlockSpec(shape, lambda *_: zeros, pipeline_mode=pl.Buffered(1))


def _params(*semantics):
    return pltpu.CompilerParams(dimension_semantics=semantics, vmem_limit_bytes=VMEM_LIMIT)


def _rope_tables(positions, half):
    inv = ROPE_BASE ** (-jnp.arange(half, dtype=F32) / half)
    ang = positions.astype(F32)[:, None] * inv[None, :]
    return jnp.cos(ang), jnp.sin(ang)


def _mla_rope_tables(positions):
    cos, sin = _rope_tables(positions, MLA_ROPE // 2)
    n = positions.shape[0]
    h = MLA_ROPE // 2
    ones = jnp.ones((n, MLA_NOPE), F32)
    zeros = lambda w: jnp.zeros((n, w), F32)
    c = jnp.concatenate([ones, cos, cos, zeros(LANE - MLA_NOPE - MLA_ROPE)], axis=1)
    s_up = jnp.concatenate([zeros(MLA_NOPE + h), sin, zeros(LANE - MLA_NOPE - MLA_ROPE)], axis=1)
    s_dn = jnp.concatenate([zeros(MLA_NOPE), -sin, zeros(LANE - MLA_NOPE - h)], axis=1)
    return c, s_up, s_dn


def _layer_spec(stacked_shape, layer):
    return pl.BlockSpec((None,) + tuple(stacked_shape[1:]), lambda *_: (layer, 0, 0),
                        pipeline_mode=pl.Buffered(1))


def _mix_mlp_call(name, layer, h, ys, w_out, ln1_g, ln1_b, w1, w2, ln2_g, ln2_b):
    m = h.shape[0]
    tm = ROW_TILE
    row_spec = lambda w: pl.BlockSpec((tm, w), lambda i: (i, 0))
    vec = _const_spec((1, D_MODEL))
    return pl.pallas_call(
        functools.partial(_mix_mlp_kernel, n_mix=len(ys)),
        grid=(m // tm,),
        in_specs=[row_spec(D_MODEL)] + [row_spec(y.shape[1]) for y in ys] + [
            _const_spec(w_out.shape), vec, vec, _layer_spec(w1.shape, layer), _layer_spec(w2.shape, layer),
            vec, vec],
        out_specs=row_spec(D_MODEL),
        out_shape=jax.ShapeDtypeStruct((m, D_MODEL), F32),
        compiler_params=_params("parallel"),
        name=name,
    )(h, *ys, w_out, ln1_g, ln1_b, w1, w2, ln2_g, ln2_b)


def kernel(x, meta_tokens, ev_w_in, ev_conv_w, ev_conv_b, ev_w_rg_a, ev_b_rg_a, ev_w_rg_x, ev_b_rg_x,
           ev_lru_lambda, ev_q_norm_g, ev_w_uq, ev_kv_norm_g, ev_w_ukv, ev_w_out, od_w_in, od_w_out,
           ln_mix_g, ln_mix_b, mlp_w1, mlp_w2, ln_mlp_g, ln_mlp_b):
    bsz, seq, _ = x.shape
    row = lambda v: v.reshape(1, -1).astype(F32)

    w_in0 = ev_w_in[0]
    lat0 = 2 * LRU_WIDTH
    kpe0 = lat0 + MLA_Q_RANK + MLA_KV_RANK
    w_kpe = jnp.zeros((D_MODEL, LANE), F32).at[:, MLA_NOPE:MLA_NOPE + MLA_ROPE].set(w_in0[:, kpe0:])
    w_in_ev = jnp.concatenate([w_in0[:, :kpe0], w_kpe], axis=1).astype(BF16)
    gate_w = jnp.concatenate([ev_w_rg_a[0], ev_w_rg_x[0]], axis=2).astype(BF16)
    w_uq = ev_w_uq[0].reshape(MLA_Q_RANK, MLA_HEADS, MLA_NOPE + MLA_ROPE)
    w_uq = jnp.pad(w_uq, ((0, 0), (0, 0), (0, LANE - MLA_NOPE - MLA_ROPE)))
    w_uq = w_uq.reshape(MLA_Q_RANK, MLA_HEADS * LANE).astype(BF16)
    w_ukv = ev_w_ukv[0].reshape(MLA_KV_RANK, MLA_HEADS, MLA_NOPE + MLA_V)
    w_uk = jnp.pad(w_ukv[:, :, :MLA_NOPE], ((0, 0), (0, 0), (0, LANE - MLA_NOPE)))
    w_uk = w_uk.reshape(MLA_KV_RANK, MLA_HEADS * LANE)
    w_uv = w_ukv[:, :, MLA_NOPE:].reshape(MLA_KV_RANK, MLA_HEADS * MLA_V)
    w_ukv_p = jnp.concatenate([w_uk, w_uv], axis=1).astype(BF16)
    w_uq_t = w_uq.T
    w_uk = w_uk.astype(BF16)
    w_uv_t = w_uv.T.astype(BF16)
    w_out_ev = ev_w_out[0].astype(BF16)
    w_in_od = od_w_in[0].astype(BF16)
    qk_all, v_all = RET_HEADS * RET_QK_DIM, RET_HEADS * RET_V_DIM
    per_head = lambda lo, width: od_w_in[0][:, lo:lo + RET_HEADS * width].reshape(D_MODEL, RET_HEADS, width)
    w_head_od = jnp.concatenate([per_head(0, RET_QK_DIM), per_head(qk_all, RET_QK_DIM),
                                 per_head(2 * qk_all, RET_V_DIM), per_head(2 * qk_all + v_all, RET_V_DIM)],
                                axis=2).transpose(1, 0, 2).astype(BF16)
    w_out_od = od_w_out[0].astype(BF16)
    w1 = mlp_w1.astype(BF16)
    w2 = mlp_w2.astype(BF16)
    conv_w = ev_conv_w[0].astype(F32)
    conv_b, b_a, b_x, lam = row(ev_conv_b[0]), row(ev_b_rg_a[0]), row(ev_b_rg_x[0]), row(ev_lru_lambda[0])
    q_g, kv_g = row(ev_q_norm_g[0]), row(ev_kv_norm_g[0])
    ln = [(row(ln_mix_g[l]), row(ln_mix_b[l]), row(ln_mlp_g[l]), row(ln_mlp_b[l])) for l in range(2)]

    pos = jnp.arange(N_META + seq, dtype=jnp.int32)
    mla_tabs = _mla_rope_tables(pos)
    cos0, sin0 = _rope_tables(pos, MLA_ROPE // 2)
    cos0_t, sin0_t = cos0.T, sin0.T
    cos1, sin1 = _rope_tables(pos, RET_QK_DIM // 2)
    k_scale = RET_QK_DIM ** -0.5
    qk = RET_HEADS * RET_QK_DIM
    log_gamma = jnp.log(1.0 - 2.0 ** (-5.0 - jnp.arange(RET_HEADS, dtype=F32)))
    log_gamma = jnp.broadcast_to(log_gamma[:, None, None], (RET_HEADS, 1, LANE))

    mixw = RET_HEADS * RET_V_DIM
    meta_out_shapes = (jax.ShapeDtypeStruct((SUBLANE, LRU_WIDTH), F32),
                       jax.ShapeDtypeStruct((SUBLANE, LRU_WIDTH), F32),
                       jax.ShapeDtypeStruct((N_META, MLA_HEADS * LANE), BF16),
                       jax.ShapeDtypeStruct((MLA_HEADS * MLA_V, N_META), BF16),
                       jax.ShapeDtypeStruct((RET_HEADS, RET_QK_DIM, RET_V_DIM), F32))
    meta_args = [meta_tokens.astype(F32), w_in_ev, conv_w, conv_b, gate_w, b_a, b_x, lam, q_g, kv_g, w_uq,
                 w_ukv_p, w_uv_t, *[t[:N_META] for t in mla_tabs], w_out_ev, *ln[0][:2]]
    meta_specs = [_const_spec(a.shape) for a in meta_args]
    meta_args += [w1, w2, *ln[0][2:], w_in_od, w_in_od]
    meta_specs += [_layer_spec(w1.shape, 0), _layer_spec(w2.shape, 0), _const_spec(ln[0][2].shape),
                   _const_spec(ln[0][3].shape),
                   pl.BlockSpec((D_MODEL, qk), lambda i: (0, 1), pipeline_mode=pl.Buffered(1)),
                   pl.BlockSpec((D_MODEL, mixw), lambda i: (0, 1), pipeline_mode=pl.Buffered(1))]
    tail_args = [cos1[:N_META] * k_scale, sin1[:N_META] * k_scale, log_gamma]
    meta_args += tail_args
    meta_specs += [_const_spec(a.shape) for a in tail_args]
    meta_out = pl.pallas_call(
        _meta_kernel,
        grid=(1,),
        in_specs=meta_specs,
        out_specs=[pl.BlockSpec(s.shape, lambda i, nd=len(s.shape): (0,) * nd) for s in meta_out_shapes],
        out_shape=meta_out_shapes,
        scratch_shapes=[pltpu.VMEM((SUBLANE + N_META, LRU_WIDTH), F32),
                        pltpu.VMEM((N_META, LRU_WIDTH), F32),
                        pltpu.VMEM((N_META, LRU_WIDTH), F32)],
        compiler_params=_params("arbitrary"),
        name="meta_tokens",
    )(*meta_args)
    rec_tail, h_tail, k_meta, vt_meta, s_meta = meta_out

    ts = SEQ_TILE
    nt = seq // ts
    tab_spec = pl.BlockSpec((ts, LANE), lambda b, t: (t, 0))
    tab_t_spec = pl.BlockSpec((MLA_ROPE // 2, ts), lambda b, t: (0, t))
    seq_spec = lambda w: pl.BlockSpec((1, ts, w), lambda b, t: (b, t, 0))
    seq_t_spec = lambda w: pl.BlockSpec((1, 1, w, ts), lambda b, t: (b, t, 0, 0))
    y_rec, qt0, k0, vt0 = pl.pallas_call(
        _seq0_kernel,
        grid=(bsz, nt),
        in_specs=[seq_spec(D_MODEL), _const_spec(w_in_ev.shape), _const_spec(conv_w.shape),
                  _const_spec(conv_b.shape), _const_spec(gate_w.shape), _const_spec(b_a.shape),
                  _const_spec(b_x.shape), _const_spec(lam.shape), _const_spec(q_g.shape),
                  _const_spec(kv_g.shape), _const_spec(w_uq_t.shape), _const_spec(w_uk.shape),
                  _const_spec(w_uv_t.shape), tab_spec, tab_spec, tab_spec, tab_t_spec, tab_t_spec,
                  _const_spec(rec_tail.shape), _const_spec(h_tail.shape)],
        out_specs=[seq_spec(LRU_WIDTH), seq_t_spec(MLA_HEADS * LANE), seq_spec(MLA_HEADS * LANE),
                   seq_t_spec(MLA_HEADS * MLA_V)],
        out_shape=(jax.ShapeDtypeStruct((bsz, seq, LRU_WIDTH), BF16),
                   jax.ShapeDtypeStruct((bsz, nt, MLA_HEADS * LANE, ts), BF16),
                   jax.ShapeDtypeStruct((bsz, seq, MLA_HEADS * LANE), BF16),
                   jax.ShapeDtypeStruct((bsz, nt, MLA_HEADS * MLA_V, ts), BF16)),
        scratch_shapes=[pltpu.VMEM((SUBLANE + ts // SEQ_SPLIT, LRU_WIDTH), F32),
                        pltpu.VMEM((LRU_WIDTH // LANE, ts, LANE), F32),
                        pltpu.VMEM((LRU_WIDTH // LANE, ts, LANE), F32),
                        pltpu.VMEM((ts, LRU_WIDTH), F32),
                        pltpu.VMEM((SUBLANE, LRU_WIDTH), F32)],
        compiler_params=_params("parallel", "arbitrary"),
        name="seq0_mixer_proj",
    )(x, w_in_ev, conv_w, conv_b, gate_w, b_a, b_x, lam, q_g, kv_g, w_uq_t, w_uk, w_uv_t,
      *[t[N_META:] for t in mla_tabs], cos0_t[:, N_META:], sin0_t[:, N_META:], rec_tail, h_tail)

    tq = ATT_TILE
    assert tq == ts
    y_att = pl.pallas_call(
        _attn_kernel,
        grid=(bsz, MLA_HEADS // 2, seq // tq),
        in_specs=[pl.BlockSpec((1, 1, 2 * LANE, tq), lambda b, j, i: (b, i, j, 0)),
                  pl.BlockSpec((1, seq, 2 * LANE), lambda b, j, i: (b, 0, j)),
                  pl.BlockSpec((1, nt, LANE, tq), lambda b, j, i: (b, 0, j, 0)),
                  pl.BlockSpec((N_META, 2 * LANE), lambda b, j, i: (0, j)),
                  pl.BlockSpec((LANE, N_META), lambda b, j, i: (j, 0))],
        out_specs=pl.BlockSpec((1, tq, LANE), lambda b, j, i: (b, i, j)),
        out_shape=jax.ShapeDtypeStruct((bsz, seq, MLA_HEADS * MLA_V), BF16),
        scratch_shapes=[pltpu.VMEM((2, 1, tq), F32), pltpu.VMEM((2, MLA_V + ATT_ONES, tq), F32)],
        compiler_params=_params("parallel", "parallel", "arbitrary"),
        name="mla_attention",
    )(qt0, k0, vt0, k_meta, vt_meta)

    m = bsz * seq
    h1 = _mix_mlp_call("layer0_out_mlp", 0, x.reshape(m, D_MODEL),
                       [y_rec.reshape(m, -1), y_att.reshape(m, -1)], w_out_ev,
                       ln[0][0], ln[0][1], w1, w2, ln[0][2], ln[0][3])

    mixw = RET_HEADS * RET_V_DIM
    head_w = RET_QK_DIM * 2 + RET_V_DIM * 2
    y_ret = pl.pallas_call(
        _ret_fused_kernel,
        grid=(bsz, RET_HEADS),
        in_specs=[pl.BlockSpec((1, seq, D_MODEL), lambda b, h: (b, 0, 0)),
                  pl.BlockSpec((None, D_MODEL, head_w), lambda b, h: (h, 0, 0)),
                  _const_spec((seq, RET_QK_DIM // 2)), _const_spec((seq, RET_QK_DIM // 2)),
                  pl.BlockSpec((1, RET_QK_DIM, RET_V_DIM), lambda b, h: (h, 0, 0)),
                  pl.BlockSpec((1, 1, LANE), lambda b, h: (h, 0, 0))],
        out_specs=pl.BlockSpec((1, seq, RET_V_DIM), lambda b, h: (b, 0, h)),
        out_shape=jax.ShapeDtypeStruct((bsz, seq, mixw), BF16),
        scratch_shapes=[pltpu.VMEM((RET_QK_DIM, RET_V_DIM), F32)],
        compiler_params=_params("parallel", "arbitrary"),
        name="layer1_retention",
    )(h1.reshape(bsz, seq, D_MODEL), w_head_od, cos1[N_META:], sin1[N_META:], s_meta, log_gamma)

    out = _mix_mlp_call("layer1_out_mlp", 1, h1, [y_ret.reshape(m, mixw)], w_out_od,
                        ln[1][0], ln[1][1], w1, w2, ln[1][2], ln[1][3])
    return out.reshape(bsz, seq, D_MODEL)
```

```python
import functools
import math

import jax
import jax.numpy as jnp
import numpy as np
from jax import lax
from jax.experimental import pallas as pl
from jax.experimental.pallas import tpu as pltpu

D_MODEL = 1024
N_META = 16
LRU_WIDTH = 512
LRU_HEADS = 4
LRU_HEAD_DIM = 128
CONV_WIDTH = 4
LRU_C = 8.0
MLA_HEADS = 8
MLA_NOPE = 64
MLA_ROPE = 32
MLA_V = 64
MLA_Q_RANK = 256
MLA_KV_RANK = 128
RET_HEADS = 4
RET_QK_DIM = 256
RET_V_DIM = 512
D_FF = 4096
ROPE_BASE = 10000.0
DN_ALPHA = 4.0 ** 0.25
EPS = 1e-5
NEG_INF = -1e30

LANE = 128
SUBLANE = 8
VMEM_LIMIT = 56 * 1024 * 1024

BF16 = jnp.bfloat16
F32 = jnp.float32

SEQ_TILE = 512
SEQ_SPLIT = 2
SCAN_STEP = 4
SCAN_BLOCK = SUBLANE * SCAN_STEP
ATT_TILE = 512
ATT_SPLIT = 2
ATT_ONES = 16
ROW_TILE = 512
ROW_SPLIT = 2
FF_TILE = 1024
RET_CHUNK = 256


def _dot(a, b):
    return jnp.dot(a, b, preferred_element_type=F32)


def _dot_nt(a, b):
    return lax.dot_general(a, b, (((1,), (1,)), ((), ())), preferred_element_type=F32)


def _dot_tn(a, b):
    return lax.dot_general(a, b, (((0,), (0,)), ((), ())), preferred_element_type=F32)


def _layernorm(x, g, b):
    mu = jnp.mean(x, axis=-1, keepdims=True)
    xc = x - mu
    var = jnp.mean(xc * xc, axis=-1, keepdims=True)
    return xc * lax.rsqrt(var + EPS) * g + b


def _rmsnorm(x, g):
    return x * lax.rsqrt(jnp.mean(x * x, axis=-1, keepdims=True) + EPS) * g


def _rope_mla(x, c, s_up, s_dn):
    return x * c + pltpu.roll(x, MLA_ROPE // 2, 1) * s_up + pltpu.roll(x, LANE - MLA_ROPE // 2, 1) * s_dn


def _lru_gates(xc, gate_w_ref, b_a, b_x, sp_lambda):
    rs, is_ = [], []
    for h in range(LRU_HEADS):
        g = _dot(xc[:, h * LRU_HEAD_DIM:(h + 1) * LRU_HEAD_DIM].astype(BF16), gate_w_ref[h])
        rs.append(g[:, :LRU_HEAD_DIM])
        is_.append(g[:, LRU_HEAD_DIM:])
    r = jax.nn.sigmoid(jnp.concatenate(rs, axis=1) + b_a)
    i = jax.nn.sigmoid(jnp.concatenate(is_, axis=1) + b_x)
    log_a = -LRU_C * r * sp_lambda
    a = jnp.exp(log_a)
    y = 1.0 - a * a
    mult = jnp.where(y > 0.0, y * lax.rsqrt(y), 0.0)
    return a, mult * (i * xc)


def _scan8(a, b):
    row = lax.broadcasted_iota(jnp.int32, a.shape, 0)
    for k in (1, 2, 4):
        keep = row >= k
        a_prev = jnp.where(keep, pltpu.roll(a, k, 0), 1.0)
        b_prev = jnp.where(keep, pltpu.roll(b, k, 0), 0.0)
        b = a * b_prev + b
        a = a * a_prev
    return a, b


def _lru_scan(a_ref, b_ref, h0, rows):
    def body(g, h_prev):
        sl = pl.ds(pl.multiple_of(g * SUBLANE, SUBLANE), SUBLANE)
        a_c, b_c = _scan8(a_ref[sl, :], b_ref[sl, :])
        h = a_c * h_prev + b_c
        b_ref[sl, :] = h
        return h[SUBLANE - 1:SUBLANE, :]
    return lax.fori_loop(0, rows // SUBLANE, body, h0, unroll=4)


def _lru_scan_blocked(a_ref, b_ref, h0, rows):
    n_slabs = a_ref.shape[0]
    row = lax.broadcasted_iota(jnp.int32, (SUBLANE, LANE), 0)

    def body(g, carry):
        base = g * SCAN_BLOCK
        out = []
        for c in range(n_slabs):
            idx = [pl.ds(base + l, SUBLANE, stride=SCAN_STEP) for l in range(SCAN_STEP)]
            a = [a_ref[c, i, :] for i in idx]
            h = [b_ref[c, idx[0], :]]
            p = [a[0]]
            for l in range(1, SCAN_STEP):
                h.append(a[l] * h[l - 1] + b_ref[c, idx[l], :])
                p.append(a[l] * p[l - 1])
            p_seg, h_seg = _scan8(p[-1], h[-1])
            end = p_seg * carry[c] + h_seg
            start = jnp.where(row == 0, carry[c], pltpu.roll(end, 1, 0))
            for l in range(SCAN_STEP):
                b_ref[c, idx[l], :] = h[l] + p[l] * start
            out.append(end[SUBLANE - 1:SUBLANE, :])
        return tuple(out)
    return lax.fori_loop(0, rows // SCAN_BLOCK, body, tuple(h0), unroll=2)


def _mla_project(qlat, kvlat, kpe, q_g, kv_g, w_uq_ref, w_ukv_ref, rope_c, rope_up, rope_dn):
    scale = (MLA_NOPE + MLA_ROPE) ** -0.5
    q_all = _dot(_rmsnorm(qlat, q_g).astype(BF16), w_uq_ref[...])
    kv_all = _dot(_rmsnorm(kvlat, kv_g).astype(BF16), w_ukv_ref[...])
    kpe_r = _rope_mla(kpe, rope_c, rope_up, rope_dn)
    qs, ks = [], []
    for h in range(MLA_HEADS):
        sl = slice(h * LANE, (h + 1) * LANE)
        qs.append((_rope_mla(q_all[:, sl], rope_c, rope_up, rope_dn) * scale).astype(BF16))
        ks.append((kv_all[:, sl] + kpe_r).astype(BF16))
    v = kv_all[:, MLA_HEADS * LANE:].astype(BF16)
    return jnp.concatenate(qs, axis=1), jnp.concatenate(ks, axis=1), v


def _ffn(h1, w1_ref, w2_ref):
    h1b = h1.astype(BF16)
    f = None
    for c in range(D_FF // FF_TILE):
        a = _dot(h1b, w1_ref[:, c * FF_TILE:(c + 1) * FF_TILE])
        a = jnp.maximum(a, 0.0)
        part = _dot((a * a).astype(BF16), w2_ref[c * FF_TILE:(c + 1) * FF_TILE, :])
        f = part if f is None else f + part
    return f


def _mlp_block(h_in, mix, ln1_g, ln1_b, w1_ref, w2_ref, ln2_g, ln2_b):
    h1 = _layernorm(DN_ALPHA * h_in + mix, ln1_g, ln1_b)
    return _layernorm(DN_ALPHA * h1 + _ffn(h1, w1_ref, w2_ref), ln2_g, ln2_b)


def _meta_kernel(meta_ref, w_in_ref, conv_w_ref, conv_b_ref, gate_w_ref, b_a_ref, b_x_ref, lam_ref,
                 q_g_ref, kv_g_ref, w_uq_ref, w_ukv_ref, w_uvt_ref, rope_c_ref, rope_up_ref, rope_dn_ref,
                 w_out_ref, ln1_g_ref, ln1_b_ref, w1_ref, w2_ref, ln2_g_ref, ln2_b_ref,
                 w_k_ref, w_v_ref, cos1_ref, sin1_ref, lg_ref,
                 rec_tail_ref, h_tail_ref, k_meta_ref, vt_meta_ref, s_meta_ref,
                 conv_scr, a_scr, b_scr):
    n = N_META
    x = meta_ref[...]
    p = _dot(x.astype(BF16), w_in_ref[...])
    gate, rec = p[:, :LRU_WIDTH], p[:, LRU_WIDTH:2 * LRU_WIDTH]
    conv_scr[0:SUBLANE, :] = jnp.zeros((SUBLANE, LRU_WIDTH), F32)
    conv_scr[SUBLANE:SUBLANE + n, :] = rec
    cw = conv_w_ref[...]
    xc = conv_b_ref[...] + cw[3:4, :] * rec
    for j in range(CONV_WIDTH - 1):
        off = SUBLANE - (CONV_WIDTH - 1) + j
        xc = xc + cw[j:j + 1, :] * conv_scr[off:off + n, :]
    sp_lambda = jax.nn.softplus(-lam_ref[...])
    a, b = _lru_gates(xc, gate_w_ref, b_a_ref[...], b_x_ref[...], sp_lambda)
    a_scr[...] = a
    b_scr[...] = b
    _lru_scan(a_scr, b_scr, jnp.zeros((1, LRU_WIDTH), F32), n)
    h = b_scr[...]
    y_rec = (h * jax.nn.gelu(gate)).astype(BF16)
    rec_tail_ref[...] = rec[n - SUBLANE:, :]
    h_tail_ref[...] = h[n - SUBLANE:, :]

    off = 2 * LRU_WIDTH
    q, k, v = _mla_project(p[:, off:off + MLA_Q_RANK],
                           p[:, off + MLA_Q_RANK:off + MLA_Q_RANK + MLA_KV_RANK],
                           p[:, off + MLA_Q_RANK + MLA_KV_RANK:],
                           q_g_ref[...], kv_g_ref[...], w_uq_ref, w_ukv_ref,
                           rope_c_ref[...], rope_up_ref[...], rope_dn_ref[...])
    k_meta_ref[...] = k
    kvn = _rmsnorm(p[:, off + MLA_Q_RANK:off + MLA_Q_RANK + MLA_KV_RANK], kv_g_ref[...]).astype(BF16)
    vt_meta_ref[...] = _dot_nt(w_uvt_ref[...], kvn).astype(BF16)
    causal = (lax.broadcasted_iota(jnp.int32, (n, n), 1) <= lax.broadcasted_iota(jnp.int32, (n, n), 0))
    outs = []
    for hh in range(MLA_HEADS):
        sl = slice(hh * LANE, (hh + 1) * LANE)
        s = jnp.where(causal, _dot_nt(q[:, sl], k[:, sl]), NEG_INF)
        e = jnp.exp(s - jnp.max(s, axis=-1, keepdims=True))
        pr = e / jnp.sum(e, axis=-1, keepdims=True)
        outs.append(_dot(pr.astype(BF16), v[:, hh * MLA_V:(hh + 1) * MLA_V]))
    y_att = jnp.concatenate(outs, axis=1).astype(BF16)
    mix = _dot(y_rec, w_out_ref[0:LRU_WIDTH, :]) + _dot(y_att, w_out_ref[LRU_WIDTH:, :])
    h2 = _mlp_block(x, mix, ln1_g_ref[...], ln1_b_ref[...], w1_ref, w2_ref, ln2_g_ref[...], ln2_b_ref[...])

    h2b = h2.astype(BF16)
    kk = _dot(h2b, w_k_ref[...])
    vv = _dot(h2b, w_v_ref[...]).astype(BF16)
    cos, sin = cos1_ref[...], sin1_ref[...]
    idx = lax.broadcasted_iota(jnp.int32, (n, 1), 0).astype(F32)
    half = RET_QK_DIM // 2
    for hh in range(RET_HEADS):
        log_gamma = lg_ref[hh][:, 0:1]
        k1 = kk[:, hh * RET_QK_DIM:hh * RET_QK_DIM + half]
        k2 = kk[:, hh * RET_QK_DIM + half:(hh + 1) * RET_QK_DIM]
        kr = jnp.concatenate([k1 * cos - k2 * sin, k1 * sin + k2 * cos], axis=1).astype(BF16)
        k_dec = jnp.exp(log_gamma * (n - 1.0 - idx))
        kd = (kr.astype(F32) * k_dec).astype(BF16)
        s_meta_ref[hh] = _dot_tn(kd, vv[:, hh * RET_V_DIM:(hh + 1) * RET_V_DIM])


def _seq0_kernel(x_ref, w_in_ref, conv_w_ref, conv_b_ref, gate_w_ref, b_a_ref, b_x_ref, lam_ref,
                 q_g_ref, kv_g_ref, w_uqt_ref, w_uk_ref, w_uvt_ref, rope_c_ref, rope_up_ref, rope_dn_ref,
                 cos_t_ref, sin_t_ref, rec_tail_ref, h_tail_ref,
                 y_rec_ref, qt_ref, k_ref, vt_ref,
                 conv_scr, a_scr, b_scr, g_scr, h_scr):
    ts = SEQ_TILE
    t = pl.program_id(1)

    @pl.when(t == 0)
    def _():
        conv_scr[0:SUBLANE, :] = rec_tail_ref[...]
        h_scr[...] = h_tail_ref[...]

    sub = ts // SEQ_SPLIT
    spans = [slice(i * sub, (i + 1) * sub) for i in range(SEQ_SPLIT)]
    ps = [_dot(x_ref[0, rows, :].astype(BF16), w_in_ref[...]) for rows in spans]
    cw = conv_w_ref[...]
    sp_lambda = jax.nn.softplus(-lam_ref[...])
    scale = (MLA_NOPE + MLA_ROPE) ** -0.5 * math.log2(math.e)
    hr = MLA_ROPE // 2
    off = 2 * LRU_WIDTH
    for rows, p in zip(spans, ps):
        gate, rec = p[:, :LRU_WIDTH], p[:, LRU_WIDTH:2 * LRU_WIDTH]
        conv_scr[SUBLANE:SUBLANE + sub, :] = rec
        xc = conv_b_ref[...] + cw[3:4, :] * rec
        for j in range(CONV_WIDTH - 1):
            o = SUBLANE - (CONV_WIDTH - 1) + j
            xc = xc + cw[j:j + 1, :] * conv_scr[o:o + sub, :]
        conv_scr[0:SUBLANE, :] = rec[sub - SUBLANE:, :]
        a, b = _lru_gates(xc, gate_w_ref, b_a_ref[...], b_x_ref[...], sp_lambda)
        for c in range(LRU_WIDTH // LANE):
            a_scr[c, rows, :] = a[:, c * LANE:(c + 1) * LANE]
            b_scr[c, rows, :] = b[:, c * LANE:(c + 1) * LANE]
        g_scr[rows, :] = jax.nn.gelu(gate)

        qn = _rmsnorm(p[:, off:off + MLA_Q_RANK], q_g_ref[...]).astype(BF16)
        q_t = _dot_nt(w_uqt_ref[...], qn)
        cos_t, sin_t = cos_t_ref[:, rows] * scale, sin_t_ref[:, rows] * scale
        for h in range(MLA_HEADS):
            base = h * LANE
            x1 = q_t[base + MLA_NOPE:base + MLA_NOPE + hr, :]
            x2 = q_t[base + MLA_NOPE + hr:base + MLA_NOPE + MLA_ROPE, :]
            qt_ref[0, 0, base:base + MLA_NOPE, rows] = (q_t[base:base + MLA_NOPE, :] * scale).astype(BF16)
            qt_ref[0, 0, base + MLA_NOPE:base + MLA_NOPE + hr, rows] = (x1 * cos_t - x2 * sin_t).astype(BF16)
            qt_ref[0, 0, base + MLA_NOPE + hr:base + MLA_NOPE + MLA_ROPE, rows] = (
                x1 * sin_t + x2 * cos_t).astype(BF16)
            qt_ref[0, 0, base + MLA_NOPE + MLA_ROPE:base + LANE, rows] = jnp.zeros(
                (LANE - MLA_NOPE - MLA_ROPE, sub), BF16)
        kvn = _rmsnorm(p[:, off + MLA_Q_RANK:off + MLA_Q_RANK + MLA_KV_RANK], kv_g_ref[...]).astype(BF16)
        k_nope = _dot(kvn, w_uk_ref[...])
        kpe_r = _rope_mla(p[:, off + MLA_Q_RANK + MLA_KV_RANK:],
                          rope_c_ref[rows, :], rope_up_ref[rows, :], rope_dn_ref[rows, :])
        for h in range(MLA_HEADS):
            sl = slice(h * LANE, (h + 1) * LANE)
            k_ref[0, rows, sl] = (k_nope[:, sl] + kpe_r).astype(BF16)
        vt_ref[0, 0, :, rows] = _dot_nt(w_uvt_ref[...], kvn).astype(BF16)

    n_slabs = LRU_WIDTH // LANE
    h0 = [h_scr[SUBLANE - 1:SUBLANE, c * LANE:(c + 1) * LANE] for c in range(n_slabs)]
    h_last = _lru_scan_blocked(a_scr, b_scr, h0, ts)
    h_scr[SUBLANE - 1:SUBLANE, :] = jnp.concatenate(h_last, axis=1)
    h = jnp.concatenate([b_scr[c] for c in range(n_slabs)], axis=1)
    y_rec_ref[0] = (h * g_scr[...]).astype(BF16)


def _attn_kernel(qt_ref, k_ref, vt_ref, k_meta_ref, vt_meta_ref, o_ref, m_scr, acc_scr):
    tq = ATT_TILE
    tw = tq // ATT_SPLIT
    qi = pl.program_id(2)
    chains = [(hh, slice(hh * LANE, (hh + 1) * LANE), slice(hh * MLA_V, (hh + 1) * MLA_V),
               part, slice(part * tw, (part + 1) * tw))
              for hh in range(2) for part in range(ATT_SPLIT)]
    def with_ones(v_t):
        return jnp.concatenate([v_t, jnp.ones((ATT_ONES, v_t.shape[1]), BF16)], axis=0)

    def n_keys(part, diagonal):
        return (part + 1) * tw if diagonal else tq

    def meta_scores():
        return [_dot(k_meta_ref[:, sl], qt_ref[0, 0, sl, cols]) for _, sl, _, _, cols in chains]

    def meta_softmax_pv(scores):
        for (hh, sl, vrows, part, cols), s in zip(chains, scores):
            m = jnp.max(s, axis=0, keepdims=True)
            e = jnp.exp2(s - m)
            m_scr[hh, :, cols] = m
            acc_scr[hh, :, cols] = _dot(with_ones(vt_meta_ref[vrows, :]), e.astype(BF16))

    def score(chain, kj, diagonal):
        _, sl, _, part, cols = chain
        return _dot(k_ref[0, kj * tq:kj * tq + n_keys(part, diagonal), sl], qt_ref[0, 0, sl, cols])

    def softmax_pv(chain, kj, s, diagonal):
        hh, sl, vrows, part, cols = chain
        nk = n_keys(part, diagonal)
        if diagonal:
            key = lax.broadcasted_iota(jnp.int32, (nk, tw), 0)
            qry = lax.broadcasted_iota(jnp.int32, (nk, tw), 1) + part * tw
            s = jnp.where(key <= qry, s, NEG_INF)
        m_old = m_scr[hh, :, cols]
        m_new = jnp.maximum(m_old, jnp.max(s, axis=0, keepdims=True))
        alpha = jnp.exp2(m_old - m_new)
        e = jnp.exp2(s - m_new)
        m_scr[hh, :, cols] = m_new
        acc_scr[hh, :, cols] = alpha * acc_scr[hh, :, cols] + _dot(
            with_ones(vt_ref[0, kj, vrows, 0:nk]), e.astype(BF16))

    def run(n_full):
        s_meta = meta_scores()
        scores = [score(c, 0, n_full == 0) for c in chains]
        meta_softmax_pv(s_meta)
        for kj in range(n_full):
            nxt = [score(c, kj + 1, kj + 1 == n_full) for c in chains]
            for c, s in zip(chains, scores):
                softmax_pv(c, kj, s, False)
            scores = nxt
        for c, s in zip(chains, scores):
            softmax_pv(c, n_full, s, True)

    for n_full in range(k_ref.shape[1] // tq):
        pl.when(qi == n_full)(functools.partial(run, n_full))
    out_t = jnp.concatenate([acc_scr[hh, 0:MLA_V, :] / acc_scr[hh, MLA_V:MLA_V + 1, :] for hh in range(2)],
                            axis=0)
    o_ref[0] = out_t.T.astype(BF16)


def _mix_mlp_kernel(*refs, n_mix):
    h_ref = refs[0]
    y_refs = refs[1:1 + n_mix]
    w_out_ref, ln1_g, ln1_b, w1_ref, w2_ref, ln2_g, ln2_b, o_ref = refs[1 + n_mix:]
    sub = h_ref.shape[0] // ROW_SPLIT
    spans = [slice(s * sub, (s + 1) * sub) for s in range(ROW_SPLIT)]
    mixes = []
    for rows in spans:
        mix = None
        row = 0
        for y_ref in y_refs:
            width = y_ref.shape[-1]
            part = _dot(y_ref[rows, :], w_out_ref[row:row + width, :])
            mix = part if mix is None else mix + part
            row += width
        mixes.append(mix)
    h1s = [_layernorm(DN_ALPHA * h_ref[rows, :] + mix, ln1_g[...], ln1_b[...]) for rows, mix in zip(spans, mixes)]
    fs = [_ffn(h1, w1_ref, w2_ref) for h1 in h1s]
    for rows, h1, f in zip(spans, h1s, fs):
        o_ref[rows, :] = _layernorm(DN_ALPHA * h1 + f, ln2_g[...], ln2_b[...])


def _ret_fused_kernel(h_ref, wq_ref, wk_ref, wv_ref, wg_ref, cos_ref, sin_ref, s0_ref, lg_ref, y_ref, s_scr):
    c = RET_CHUNK
    half = RET_QK_DIM // 2
    k_scale = RET_QK_DIM ** -0.5
    log_gamma = lg_ref[0][:, 0:1]
    ii = lax.broadcasted_iota(jnp.int32, (c, c), 0)
    jj = lax.broadcasted_iota(jnp.int32, (c, c), 1)
    diff = (ii - jj).astype(F32)
    decay = jnp.where(diff >= 0, jnp.exp(log_gamma * jnp.maximum(diff, 0.0)), 0.0)
    idx = lax.broadcasted_iota(jnp.int32, (c, 1), 0).astype(F32)
    q_decay = jnp.exp(log_gamma * (idx + 1.0))
    k_decay = jnp.exp(log_gamma * (c - 1.0 - idx))
    chunk_decay = jnp.exp(log_gamma * c)
    s_scr[...] = s0_ref[0]

    def project(ci):
        rows = slice(ci * c, (ci + 1) * c)
        hb = h_ref[0, rows, :].astype(BF16)
        cos, sin = cos_ref[rows, :], sin_ref[rows, :]
        pq = _dot(hb, wq_ref[...])
        q1, q2 = pq[:, 0:half], pq[:, half:]
        q = jnp.concatenate([q1 * cos - q2 * sin, q1 * sin + q2 * cos], axis=1)
        kcos, ksin = cos * k_scale, sin * k_scale
        pk = _dot(hb, wk_ref[...])
        k1, k2 = pk[:, 0:half], pk[:, half:]
        k = jnp.concatenate([k1 * kcos - k2 * ksin, k1 * ksin + k2 * kcos], axis=1)
        v = _dot(hb, wv_ref[...]).astype(BF16)
        half_g = 0.5 * _dot(hb, wg_ref[...])
        gate = (half_g + half_g * jnp.tanh(half_g)).astype(BF16)
        qb, kb = q.astype(BF16), k.astype(BF16)
        scores = (_dot_nt(qb, kb) * decay).astype(BF16)
        return dict(scores=scores, qd=(q * q_decay).astype(BF16), kd=(k * k_decay).astype(BF16), v=v, gate=gate)

    def recur(t):
        s_prev = s_scr[...]
        o = _dot(jnp.concatenate([t["scores"], t["qd"]], axis=1),
                 jnp.concatenate([t["v"], s_prev.astype(BF16)], axis=0))
        s_scr[...] = chunk_decay * s_prev + _dot_tn(t["kd"], t["v"])
        return o

    def finish(ci, o, gate):
        rows = slice(ci * c, (ci + 1) * c)
        o = o * lax.rsqrt(jnp.mean(o * o, axis=-1, keepdims=True) + EPS)
        y_ref[0, rows, :] = (gate.astype(F32) * o).astype(BF16)

    n_chunks = h_ref.shape[1] // c
    cur = project(0)
    prev = None
    for ci in range(n_chunks):
        nxt = project(ci + 1) if ci + 1 < n_chunks else None
        o = recur(cur)
        if prev is not None:
            finish(ci - 1, *prev)
        prev, cur = (o, cur["gate"]), nxt
    finish(n_chunks - 1, *prev)


def _const_spec(shape):
    zeros = (0,) * len(shape)
    return pl.BlockSpec(shape, lambda *_: zeros, pipeline_mode=pl.Buffered(1))


def _params(*semantics):
    return pltpu.CompilerParams(dimension_semantics=semantics, vmem_limit_bytes=VMEM_LIMIT)


def _rope_tables(positions, half):
    inv = ROPE_BASE ** (-jnp.arange(half, dtype=F32) / half)
    ang = positions.astype(F32)[:, None] * inv[None, :]
    return jnp.cos(ang), jnp.sin(ang)


def _mla_rope_tables(positions):
    cos, sin = _rope_tables(positions, MLA_ROPE // 2)
    n = positions.shape[0]
    h = MLA_ROPE // 2
    ones = jnp.ones((n, MLA_NOPE), F32)
    zeros = lambda w: jnp.zeros((n, w), F32)
    c = jnp.concatenate([ones, cos, cos, zeros(LANE - MLA_NOPE - MLA_ROPE)], axis=1)
    s_up = jnp.concatenate([zeros(MLA_NOPE + h), sin, zeros(LANE - MLA_NOPE - MLA_ROPE)], axis=1)
    s_dn = jnp.concatenate([zeros(MLA_NOPE), -sin, zeros(LANE - MLA_NOPE - h)], axis=1)
    return c, s_up, s_dn


def _layer_spec(stacked_shape, layer):
    return pl.BlockSpec((None,) + tuple(stacked_shape[1:]), lambda *_: (layer, 0, 0),
                        pipeline_mode=pl.Buffered(1))


def _mix_mlp_call(name, layer, h, ys, w_out, ln1_g, ln1_b, w1, w2, ln2_g, ln2_b):
    m = h.shape[0]
    tm = ROW_TILE
    row_spec = lambda w: pl.BlockSpec((tm, w), lambda i: (i, 0))
    vec = _const_spec((1, D_MODEL))
    return pl.pallas_call(
        functools.partial(_mix_mlp_kernel, n_mix=len(ys)),
        grid=(m // tm,),
        in_specs=[row_spec(D_MODEL)] + [row_spec(y.shape[1]) for y in ys] + [
            _const_spec(w_out.shape), vec, vec, _layer_spec(w1.shape, layer), _layer_spec(w2.shape, layer),
            vec, vec],
        out_specs=row_spec(D_MODEL),
        out_shape=jax.ShapeDtypeStruct((m, D_MODEL), F32),
        compiler_params=_params("parallel"),
        name=name,
    )(h, *ys, w_out, ln1_g, ln1_b, w1, w2, ln2_g, ln2_b)


def kernel(x, meta_tokens, ev_w_in, ev_conv_w, ev_conv_b, ev_w_rg_a, ev_b_rg_a, ev_w_rg_x, ev_b_rg_x,
           ev_lru_lambda, ev_q_norm_g, ev_w_uq, ev_kv_norm_g, ev_w_ukv, ev_w_out, od_w_in, od_w_out,
           ln_mix_g, ln_mix_b, mlp_w1, mlp_w2, ln_mlp_g, ln_mlp_b):
    bsz, seq, _ = x.shape
    row = lambda v: v.reshape(1, -1).astype(F32)

    w_in0 = ev_w_in[0]
    lat0 = 2 * LRU_WIDTH
    kpe0 = lat0 + MLA_Q_RANK + MLA_KV_RANK
    w_kpe = jnp.zeros((D_MODEL, LANE), F32).at[:, MLA_NOPE:MLA_NOPE + MLA_ROPE].set(w_in0[:, kpe0:])
    w_in_ev = jnp.concatenate([w_in0[:, :kpe0], w_kpe], axis=1).astype(BF16)
    gate_w = jnp.concatenate([ev_w_rg_a[0], ev_w_rg_x[0]], axis=2).astype(BF16)
    w_uq = ev_w_uq[0].reshape(MLA_Q_RANK, MLA_HEADS, MLA_NOPE + MLA_ROPE)
    w_uq = jnp.pad(w_uq, ((0, 0), (0, 0), (0, LANE - MLA_NOPE - MLA_ROPE)))
    w_uq = w_uq.reshape(MLA_Q_RANK, MLA_HEADS * LANE).astype(BF16)
    w_ukv = ev_w_ukv[0].reshape(MLA_KV_RANK, MLA_HEADS, MLA_NOPE + MLA_V)
    w_uk = jnp.pad(w_ukv[:, :, :MLA_NOPE], ((0, 0), (0, 0), (0, LANE - MLA_NOPE)))
    w_uk = w_uk.reshape(MLA_KV_RANK, MLA_HEADS * LANE)
    w_uv = w_ukv[:, :, MLA_NOPE:].reshape(MLA_KV_RANK, MLA_HEADS * MLA_V)
    w_ukv_p = jnp.concatenate([w_uk, w_uv], axis=1).astype(BF16)
    w_uq_t = w_uq.T
    w_uk = w_uk.astype(BF16)
    w_uv_t = w_uv.T.astype(BF16)
    w_out_ev = ev_w_out[0].astype(BF16)
    w_in_od = od_w_in[0].astype(BF16)
    w_out_od = od_w_out[0].astype(BF16)
    w1 = mlp_w1.astype(BF16)
    w2 = mlp_w2.astype(BF16)
    conv_w = ev_conv_w[0].astype(F32)
    conv_b, b_a, b_x, lam = row(ev_conv_b[0]), row(ev_b_rg_a[0]), row(ev_b_rg_x[0]), row(ev_lru_lambda[0])
    q_g, kv_g = row(ev_q_norm_g[0]), row(ev_kv_norm_g[0])
    ln = [(row(ln_mix_g[l]), row(ln_mix_b[l]), row(ln_mlp_g[l]), row(ln_mlp_b[l])) for l in range(2)]

    pos = jnp.arange(N_META + seq, dtype=jnp.int32)
    mla_tabs = _mla_rope_tables(pos)
    cos0, sin0 = _rope_tables(pos, MLA_ROPE // 2)
    cos0_t, sin0_t = cos0.T, sin0.T
    cos1, sin1 = _rope_tables(pos, RET_QK_DIM // 2)
    k_scale = RET_QK_DIM ** -0.5
    qk = RET_HEADS * RET_QK_DIM
    log_gamma = jnp.log(1.0 - 2.0 ** (-5.0 - jnp.arange(RET_HEADS, dtype=F32)))
    log_gamma = jnp.broadcast_to(log_gamma[:, None, None], (RET_HEADS, 1, LANE))

    mixw = RET_HEADS * RET_V_DIM
    meta_out_shapes = (jax.ShapeDtypeStruct((SUBLANE, LRU_WIDTH), F32),
                       jax.ShapeDtypeStruct((SUBLANE, LRU_WIDTH), F32),
                       jax.ShapeDtypeStruct((N_META, MLA_HEADS * LANE), BF16),
                       jax.ShapeDtypeStruct((MLA_HEADS * MLA_V, N_META), BF16),
                       jax.ShapeDtypeStruct((RET_HEADS, RET_QK_DIM, RET_V_DIM), F32))
    meta_args = [meta_tokens.astype(F32), w_in_ev, conv_w, conv_b, gate_w, b_a, b_x, lam, q_g, kv_g, w_uq,
                 w_ukv_p, w_uv_t, *[t[:N_META] for t in mla_tabs], w_out_ev, *ln[0][:2]]
    meta_specs = [_const_spec(a.shape) for a in meta_args]
    meta_args += [w1, w2, *ln[0][2:], w_in_od, w_in_od]
    meta_specs += [_layer_spec(w1.shape, 0), _layer_spec(w2.shape, 0), _const_spec(ln[0][2].shape),
                   _const_spec(ln[0][3].shape),
                   pl.BlockSpec((D_MODEL, qk), lambda i: (0, 1), pipeline_mode=pl.Buffered(1)),
                   pl.BlockSpec((D_MODEL, mixw), lambda i: (0, 1), pipeline_mode=pl.Buffered(1))]
    tail_args = [cos1[:N_META] * k_scale, sin1[:N_META] * k_scale, log_gamma]
    meta_args += tail_args
    meta_specs += [_const_spec(a.shape) for a in tail_args]
    meta_out = pl.pallas_call(
        _meta_kernel,
        grid=(1,),
        in_specs=meta_specs,
        out_specs=[pl.BlockSpec(s.shape, lambda i, nd=len(s.shape): (0,) * nd) for s in meta_out_shapes],
        out_shape=meta_out_shapes,
        scratch_shapes=[pltpu.VMEM((SUBLANE + N_META, LRU_WIDTH), F32),
                        pltpu.VMEM((N_META, LRU_WIDTH), F32),
                        pltpu.VMEM((N_META, LRU_WIDTH), F32)],
        compiler_params=_params("arbitrary"),
        name="meta_tokens",
    )(*meta_args)
    rec_tail, h_tail, k_meta, vt_meta, s_meta = meta_out

    ts = SEQ_TILE
    nt = seq // ts
    tab_spec = pl.BlockSpec((ts, LANE), lambda b, t: (t, 0))
    tab_t_spec = pl.BlockSpec((MLA_ROPE // 2, ts), lambda b, t: (0, t))
    seq_spec = lambda w: pl.BlockSpec((1, ts, w), lambda b, t: (b, t, 0))
    seq_t_spec = lambda w: pl.BlockSpec((1, 1, w, ts), lambda b, t: (b, t, 0, 0))
    y_rec, qt0, k0, vt0 = pl.pallas_call(
        _seq0_kernel,
        grid=(bsz, nt),
        in_specs=[seq_spec(D_MODEL), _const_spec(w_in_ev.shape), _const_spec(conv_w.shape),
                  _const_spec(conv_b.shape), _const_spec(gate_w.shape), _const_spec(b_a.shape),
                  _const_spec(b_x.shape), _const_spec(lam.shape), _const_spec(q_g.shape),
                  _const_spec(kv_g.shape), _const_spec(w_uq_t.shape), _const_spec(w_uk.shape),
                  _const_spec(w_uv_t.shape), tab_spec, tab_spec, tab_spec, tab_t_spec, tab_t_spec,
                  _const_spec(rec_tail.shape), _const_spec(h_tail.shape)],
        out_specs=[seq_spec(LRU_WIDTH), seq_t_spec(MLA_HEADS * LANE), seq_spec(MLA_HEADS * LANE),
                   seq_t_spec(MLA_HEADS * MLA_V)],
        out_shape=(jax.ShapeDtypeStruct((bsz, seq, LRU_WIDTH), BF16),
                   jax.ShapeDtypeStruct((bsz, nt, MLA_HEADS * LANE, ts), BF16),
                   jax.ShapeDtypeStruct((bsz, seq, MLA_HEADS * LANE), BF16),
                   jax.ShapeDtypeStruct((bsz, nt, MLA_HEADS * MLA_V, ts), BF16)),
        scratch_shapes=[pltpu.VMEM((SUBLANE + ts // SEQ_SPLIT, LRU_WIDTH), F32),
                        pltpu.VMEM((LRU_WIDTH // LANE, ts, LANE), F32),
                        pltpu.VMEM((LRU_WIDTH // LANE, ts, LANE), F32),
                        pltpu.VMEM((ts, LRU_WIDTH), F32),
                        pltpu.VMEM((SUBLANE, LRU_WIDTH), F32)],
        compiler_params=_params("parallel", "arbitrary"),
        name="seq0_mixer_proj",
    )(x, w_in_ev, conv_w, conv_b, gate_w, b_a, b_x, lam, q_g, kv_g, w_uq_t, w_uk, w_uv_t,
      *[t[N_META:] for t in mla_tabs], cos0_t[:, N_META:], sin0_t[:, N_META:], rec_tail, h_tail)

    tq = ATT_TILE
    assert tq == ts
    y_att = pl.pallas_call(
        _attn_kernel,
        grid=(bsz, MLA_HEADS // 2, seq // tq),
        in_specs=[pl.BlockSpec((1, 1, 2 * LANE, tq), lambda b, j, i: (b, i, j, 0)),
                  pl.BlockSpec((1, seq, 2 * LANE), lambda b, j, i: (b, 0, j)),
                  pl.BlockSpec((1, nt, LANE, tq), lambda b, j, i: (b, 0, j, 0)),
                  pl.BlockSpec((N_META, 2 * LANE), lambda b, j, i: (0, j)),
                  pl.BlockSpec((LANE, N_META), lambda b, j, i: (j, 0))],
        out_specs=pl.BlockSpec((1, tq, LANE), lambda b, j, i: (b, i, j)),
        out_shape=jax.ShapeDtypeStruct((bsz, seq, MLA_HEADS * MLA_V), BF16),
        scratch_shapes=[pltpu.VMEM((2, 1, tq), F32), pltpu.VMEM((2, MLA_V + ATT_ONES, tq), F32)],
        compiler_params=_params("parallel", "parallel", "arbitrary"),
        name="mla_attention",
    )(qt0, k0, vt0, k_meta, vt_meta)

    m = bsz * seq
    h1 = _mix_mlp_call("layer0_out_mlp", 0, x.reshape(m, D_MODEL),
                       [y_rec.reshape(m, -1), y_att.reshape(m, -1)], w_out_ev,
                       ln[0][0], ln[0][1], w1, w2, ln[0][2], ln[0][3])

    mixw = RET_HEADS * RET_V_DIM
    col_spec = lambda width, first: pl.BlockSpec((D_MODEL, width), lambda b, h: (0, first + h))
    y_ret = pl.pallas_call(
        _ret_fused_kernel,
        grid=(bsz, RET_HEADS),
        in_specs=[pl.BlockSpec((1, seq, D_MODEL), lambda b, h: (b, 0, 0)),
                  col_spec(RET_QK_DIM, 0), col_spec(RET_QK_DIM, RET_HEADS),
                  col_spec(RET_V_DIM, RET_HEADS), col_spec(RET_V_DIM, 2 * RET_HEADS),
                  _const_spec((seq, RET_QK_DIM // 2)), _const_spec((seq, RET_QK_DIM // 2)),
                  pl.BlockSpec((1, RET_QK_DIM, RET_V_DIM), lambda b, h: (h, 0, 0)),
                  pl.BlockSpec((1, 1, LANE), lambda b, h: (h, 0, 0))],
        out_specs=pl.BlockSpec((1, seq, RET_V_DIM), lambda b, h: (b, 0, h)),
        out_shape=jax.ShapeDtypeStruct((bsz, seq, mixw), BF16),
        scratch_shapes=[pltpu.VMEM((RET_QK_DIM, RET_V_DIM), F32)],
        compiler_params=_params("parallel", "arbitrary"),
        name="layer1_retention",
    )(h1.reshape(bsz, seq, D_MODEL), w_in_od, w_in_od, w_in_od, w_in_od, cos1[N_META:], sin1[N_META:],
      s_meta, log_gamma)

    out = _mix_mlp_call("layer1_out_mlp", 1, h1, [y_ret.reshape(m, mixw)], w_out_od,
                        ln[1][0], ln[1][1], w1, w2, ln[1][2], ln[1][3])
    return out.reshape(bsz, seq, D_MODEL)
```

```python
import functools
import math

import jax
import jax.numpy as jnp
from jax import lax
from jax.experimental import pallas as pl
from jax.experimental.pallas import tpu as pltpu

D_MODEL = 1024
N_META = 16
LRU_WIDTH = 512
LRU_HEADS = 4
LRU_HEAD_DIM = 128
CONV_WIDTH = 4
LRU_C = 8.0
MLA_HEADS = 8
MLA_NOPE = 64
MLA_ROPE = 32
MLA_V = 64
MLA_Q_RANK = 256
MLA_KV_RANK = 128
RET_HEADS = 4
RET_QK_DIM = 256
RET_V_DIM = 512
D_FF = 4096
ROPE_BASE = 10000.0
DN_ALPHA = 4.0 ** 0.25
EPS = 1e-5
NEG_INF = -1e30

LANE = 128
SUBLANE = 8
VMEM_LIMIT = 56 * 1024 * 1024

BF16 = jnp.bfloat16
F32 = jnp.float32

SEQ_TILE = 512
SEQ_SPLIT = 2
SCAN_STEP = 4
SCAN_BLOCK = SUBLANE * SCAN_STEP
ATT_TILE = 512
ATT_SPLIT = 2
ATT_ONES = 16
ROW_TILE = 512
ROW_SPLIT = 2
FF_TILE = 1024
RET_CHUNK = 256


def _dot(a, b):
    return jnp.dot(a, b, preferred_element_type=F32)


def _dot_nt(a, b):
    return lax.dot_general(a, b, (((1,), (1,)), ((), ())), preferred_element_type=F32)


def _dot_tn(a, b):
    return lax.dot_general(a, b, (((0,), (0,)), ((), ())), preferred_element_type=F32)


def _layernorm(x, g, b):
    mu = jnp.mean(x, axis=-1, keepdims=True)
    xc = x - mu
    var = jnp.mean(xc * xc, axis=-1, keepdims=True)
    return xc * lax.rsqrt(var + EPS) * g + b


def _rmsnorm(x, g):
    return x * lax.rsqrt(jnp.mean(x * x, axis=-1, keepdims=True) + EPS) * g


def _rope_mla(x, c, s_up, s_dn):
    return x * c + pltpu.roll(x, MLA_ROPE // 2, 1) * s_up + pltpu.roll(x, LANE - MLA_ROPE // 2, 1) * s_dn


def _lru_gates(xc, gate_w_ref, b_a, b_x, sp_lambda):
    rs, is_ = [], []
    for h in range(LRU_HEADS):
        g = _dot(xc[:, h * LRU_HEAD_DIM:(h + 1) * LRU_HEAD_DIM].astype(BF16), gate_w_ref[h])
        rs.append(g[:, :LRU_HEAD_DIM])
        is_.append(g[:, LRU_HEAD_DIM:])
    r = jax.nn.sigmoid(jnp.concatenate(rs, axis=1) + b_a)
    i = jax.nn.sigmoid(jnp.concatenate(is_, axis=1) + b_x)
    log_a = -LRU_C * r * sp_lambda
    a = jnp.exp(log_a)
    y = 1.0 - a * a
    mult = jnp.where(y > 0.0, y * lax.rsqrt(y), 0.0)
    return a, mult * (i * xc)


def _scan8(a, b):
    row = lax.broadcasted_iota(jnp.int32, a.shape, 0)
    for k in (1, 2, 4):
        keep = row >= k
        a_prev = jnp.where(keep, pltpu.roll(a, k, 0), 1.0)
        b_prev = jnp.where(keep, pltpu.roll(b, k, 0), 0.0)
        b = a * b_prev + b
        a = a * a_prev
    return a, b


def _lru_scan(a_ref, b_ref, h0, rows):
    def body(g, h_prev):
        sl = pl.ds(pl.multiple_of(g * SUBLANE, SUBLANE), SUBLANE)
        a_c, b_c = _scan8(a_ref[sl, :], b_ref[sl, :])
        h = a_c * h_prev + b_c
        b_ref[sl, :] = h
        return h[SUBLANE - 1:SUBLANE, :]
    return lax.fori_loop(0, rows // SUBLANE, body, h0, unroll=4)


def _lru_scan_blocked(a_ref, b_ref, h0, rows):
    n_slabs = a_ref.shape[0]
    row = lax.broadcasted_iota(jnp.int32, (SUBLANE, LANE), 0)

    def body(g, carry):
        base = g * SCAN_BLOCK
        out = []
        for c in range(n_slabs):
            idx = [pl.ds(base + l, SUBLANE, stride=SCAN_STEP) for l in range(SCAN_STEP)]
            a = [a_ref[c, i, :] for i in idx]
            h = [b_ref[c, idx[0], :]]
            p = [a[0]]
            for l in range(1, SCAN_STEP):
                h.append(a[l] * h[l - 1] + b_ref[c, idx[l], :])
                p.append(a[l] * p[l - 1])
            p_seg, h_seg = _scan8(p[-1], h[-1])
            end = p_seg * carry[c] + h_seg
            start = jnp.where(row == 0, carry[c], pltpu.roll(end, 1, 0))
            for l in range(SCAN_STEP):
                b_ref[c, idx[l], :] = h[l] + p[l] * start
            out.append(end[SUBLANE - 1:SUBLANE, :])
        return tuple(out)
    return lax.fori_loop(0, rows // SCAN_BLOCK, body, tuple(h0), unroll=2)


def _mla_project(qlat, kvlat, kpe, q_g, kv_g, w_uq_ref, w_ukv_ref, rope_c, rope_up, rope_dn):
    scale = (MLA_NOPE + MLA_ROPE) ** -0.5
    q_all = _dot(_rmsnorm(qlat, q_g).astype(BF16), w_uq_ref[...])
    kv_all = _dot(_rmsnorm(kvlat, kv_g).astype(BF16), w_ukv_ref[...])
    kpe_r = _rope_mla(kpe, rope_c, rope_up, rope_dn)
    qs, ks = [], []
    for h in range(MLA_HEADS):
        sl = slice(h * LANE, (h + 1) * LANE)
        qs.append((_rope_mla(q_all[:, sl], rope_c, rope_up, rope_dn) * scale).astype(BF16))
        ks.append((kv_all[:, sl] + kpe_r).astype(BF16))
    v = kv_all[:, MLA_HEADS * LANE:].astype(BF16)
    return jnp.concatenate(qs, axis=1), jnp.concatenate(ks, axis=1), v


def _ffn(h1, w1_ref, w2_ref):
    h1b = h1.astype(BF16)
    f = None
    for c in range(D_FF // FF_TILE):
        a = _dot(h1b, w1_ref[:, c * FF_TILE:(c + 1) * FF_TILE])
        a = jnp.maximum(a, 0.0)
        part = _dot((a * a).astype(BF16), w2_ref[c * FF_TILE:(c + 1) * FF_TILE, :])
        f = part if f is None else f + part
    return f


def _mlp_block(h_in, mix, ln1_g, ln1_b, w1_ref, w2_ref, ln2_g, ln2_b):
    h1 = _layernorm(DN_ALPHA * h_in + mix, ln1_g, ln1_b)
    return _layernorm(DN_ALPHA * h1 + _ffn(h1, w1_ref, w2_ref), ln2_g, ln2_b)


def _meta_kernel(meta_ref, w_in_ref, conv_w_ref, conv_b_ref, gate_w_ref, b_a_ref, b_x_ref, lam_ref,
                 q_g_ref, kv_g_ref, w_uq_ref, w_ukv_ref, w_uvt_ref, rope_c_ref, rope_up_ref, rope_dn_ref,
                 w_out_ref, ln1_g_ref, ln1_b_ref, w1_ref, w2_ref, ln2_g_ref, ln2_b_ref,
                 w_k_ref, w_v_ref, cos1_ref, sin1_ref, lg_ref,
                 rec_tail_ref, h_tail_ref, k_meta_ref, vt_meta_ref, s_meta_ref,
                 conv_scr, a_scr, b_scr):
    n = N_META
    x = meta_ref[...]
    p = _dot(x.astype(BF16), w_in_ref[...])
    gate, rec = p[:, :LRU_WIDTH], p[:, LRU_WIDTH:2 * LRU_WIDTH]
    conv_scr[0:SUBLANE, :] = jnp.zeros((SUBLANE, LRU_WIDTH), F32)
    conv_scr[SUBLANE:SUBLANE + n, :] = rec
    cw = conv_w_ref[...]
    xc = conv_b_ref[...] + cw[3:4, :] * rec
    for j in range(CONV_WIDTH - 1):
        off = SUBLANE - (CONV_WIDTH - 1) + j
        xc = xc + cw[j:j + 1, :] * conv_scr[off:off + n, :]
    sp_lambda = jax.nn.softplus(-lam_ref[...])
    a, b = _lru_gates(xc, gate_w_ref, b_a_ref[...], b_x_ref[...], sp_lambda)
    a_scr[...] = a
    b_scr[...] = b
    _lru_scan(a_scr, b_scr, jnp.zeros((1, LRU_WIDTH), F32), n)
    h = b_scr[...]
    y_rec = (h * jax.nn.gelu(gate)).astype(BF16)
    rec_tail_ref[...] = rec[n - SUBLANE:, :]
    h_tail_ref[...] = h[n - SUBLANE:, :]

    off = 2 * LRU_WIDTH
    q, k, v = _mla_project(p[:, off:off + MLA_Q_RANK],
                           p[:, off + MLA_Q_RANK:off + MLA_Q_RANK + MLA_KV_RANK],
                           p[:, off + MLA_Q_RANK + MLA_KV_RANK:],
                           q_g_ref[...], kv_g_ref[...], w_uq_ref, w_ukv_ref,
                           rope_c_ref[...], rope_up_ref[...], rope_dn_ref[...])
    k_meta_ref[...] = k
    kvn = _rmsnorm(p[:, off + MLA_Q_RANK:off + MLA_Q_RANK + MLA_KV_RANK], kv_g_ref[...]).astype(BF16)
    vt_meta_ref[...] = _dot_nt(w_uvt_ref[...], kvn).astype(BF16)
    causal = (lax.broadcasted_iota(jnp.int32, (n, n), 1) <= lax.broadcasted_iota(jnp.int32, (n, n), 0))
    outs = []
    for hh in range(MLA_HEADS):
        sl = slice(hh * LANE, (hh + 1) * LANE)
        s = jnp.where(causal, _dot_nt(q[:, sl], k[:, sl]), NEG_INF)
        e = jnp.exp(s - jnp.max(s, axis=-1, keepdims=True))
        pr = e / jnp.sum(e, axis=-1, keepdims=True)
        outs.append(_dot(pr.astype(BF16), v[:, hh * MLA_V:(hh + 1) * MLA_V]))
    y_att = jnp.concatenate(outs, axis=1).astype(BF16)
    mix = _dot(y_rec, w_out_ref[0:LRU_WIDTH, :]) + _dot(y_att, w_out_ref[LRU_WIDTH:, :])
    h2 = _mlp_block(x, mix, ln1_g_ref[...], ln1_b_ref[...], w1_ref, w2_ref, ln2_g_ref[...], ln2_b_ref[...])

    h2b = h2.astype(BF16)
    kk = _dot(h2b, w_k_ref[...].astype(BF16))
    vv = _dot(h2b, w_v_ref[...].astype(BF16)).astype(BF16)
    cos, sin = cos1_ref[...], sin1_ref[...]
    idx = lax.broadcasted_iota(jnp.int32, (n, 1), 0).astype(F32)
    half = RET_QK_DIM // 2
    for hh in range(RET_HEADS):
        log_gamma = lg_ref[hh][:, 0:1]
        k1 = kk[:, hh * RET_QK_DIM:hh * RET_QK_DIM + half]
        k2 = kk[:, hh * RET_QK_DIM + half:(hh + 1) * RET_QK_DIM]
        kr = jnp.concatenate([k1 * cos - k2 * sin, k1 * sin + k2 * cos], axis=1).astype(BF16)
        k_dec = jnp.exp(log_gamma * (n - 1.0 - idx))
        kd = (kr.astype(F32) * k_dec).astype(BF16)
        s_meta_ref[hh] = _dot_tn(kd, vv[:, hh * RET_V_DIM:(hh + 1) * RET_V_DIM])


def _seq0_kernel(x_ref, w_in_ref, conv_w_ref, conv_b_ref, gate_w_ref, b_a_ref, b_x_ref, lam_ref,
                 q_g_ref, kv_g_ref, w_uqt_ref, w_uk_ref, w_uvt_ref, rope_c_ref, rope_up_ref, rope_dn_ref,
                 cos_t_ref, sin_t_ref, rec_tail_ref, h_tail_ref,
                 y_rec_ref, qt_ref, k_ref, vt_ref,
                 conv_scr, a_scr, b_scr, g_scr, h_scr):
    ts = SEQ_TILE
    t = pl.program_id(1)

    n_slabs = LRU_WIDTH // LANE

    @pl.when(t == 0)
    def _():
        conv_scr[0:SUBLANE, :] = rec_tail_ref[...]
        h_scr[...] = h_tail_ref[...]

    sub = ts // SEQ_SPLIT
    spans = [slice(i * sub, (i + 1) * sub) for i in range(SEQ_SPLIT)]
    ps = [_dot(x_ref[0, rows, :].astype(BF16), w_in_ref[...]) for rows in spans]
    cw = conv_w_ref[...]
    sp_lambda = jax.nn.softplus(-lam_ref[...])
    scale = (MLA_NOPE + MLA_ROPE) ** -0.5 * math.log2(math.e)
    hr = MLA_ROPE // 2
    off = 2 * LRU_WIDTH
    for rows, p in zip(spans, ps):
        gate, rec = p[:, :LRU_WIDTH], p[:, LRU_WIDTH:2 * LRU_WIDTH]
        conv_scr[SUBLANE:SUBLANE + sub, :] = rec
        xc = conv_b_ref[...] + cw[3:4, :] * rec
        for j in range(CONV_WIDTH - 1):
            o = SUBLANE - (CONV_WIDTH - 1) + j
            xc = xc + cw[j:j + 1, :] * conv_scr[o:o + sub, :]
        conv_scr[0:SUBLANE, :] = rec[sub - SUBLANE:, :]
        a, b = _lru_gates(xc, gate_w_ref, b_a_ref[...], b_x_ref[...], sp_lambda)
        for c in range(LRU_WIDTH // LANE):
            a_scr[c, rows, :] = a[:, c * LANE:(c + 1) * LANE]
            b_scr[c, rows, :] = b[:, c * LANE:(c + 1) * LANE]
        g_scr[rows, :] = jax.nn.gelu(gate)

        qn = _rmsnorm(p[:, off:off + MLA_Q_RANK], q_g_ref[...]).astype(BF16)
        q_t = _dot_nt(w_uqt_ref[...], qn)
        cos_t, sin_t = cos_t_ref[:, rows] * scale, sin_t_ref[:, rows] * scale
        for h in range(MLA_HEADS):
            base = h * LANE
            x1 = q_t[base + MLA_NOPE:base + MLA_NOPE + hr, :]
            x2 = q_t[base + MLA_NOPE + hr:base + MLA_NOPE + MLA_ROPE, :]
            qt_ref[0, 0, base:base + MLA_NOPE, rows] = (q_t[base:base + MLA_NOPE, :] * scale).astype(BF16)
            qt_ref[0, 0, base + MLA_NOPE:base + MLA_NOPE + hr, rows] = (x1 * cos_t - x2 * sin_t).astype(BF16)
            qt_ref[0, 0, base + MLA_NOPE + hr:base + MLA_NOPE + MLA_ROPE, rows] = (
                x1 * sin_t + x2 * cos_t).astype(BF16)
            qt_ref[0, 0, base + MLA_NOPE + MLA_ROPE:base + LANE, rows] = jnp.zeros(
                (LANE - MLA_NOPE - MLA_ROPE, sub), BF16)
        kvn = _rmsnorm(p[:, off + MLA_Q_RANK:off + MLA_Q_RANK + MLA_KV_RANK], kv_g_ref[...]).astype(BF16)
        k_nope = _dot(kvn, w_uk_ref[...])
        kpe_r = _rope_mla(p[:, off + MLA_Q_RANK + MLA_KV_RANK:],
                          rope_c_ref[rows, :], rope_up_ref[rows, :], rope_dn_ref[rows, :])
        for h in range(MLA_HEADS):
            sl = slice(h * LANE, (h + 1) * LANE)
            k_ref[0, rows, sl] = (k_nope[:, sl] + kpe_r).astype(BF16)
        vt_ref[0, 0, :, rows] = _dot_nt(w_uvt_ref[...], kvn).astype(BF16)

    h0 = [h_scr[SUBLANE - 1:SUBLANE, c * LANE:(c + 1) * LANE] for c in range(n_slabs)]
    h_last = _lru_scan_blocked(a_scr, b_scr, h0, ts)
    h_scr[SUBLANE - 1:SUBLANE, :] = jnp.concatenate(h_last, axis=1)
    h = jnp.concatenate([b_scr[c] for c in range(n_slabs)], axis=1)
    y_rec_ref[0] = (h * g_scr[...]).astype(BF16)


def _attn_kernel(qt_ref, k_ref, vt_ref, k_meta_ref, vt_meta_ref, o_ref, m_scr, acc_scr):
    tq = ATT_TILE
    tw = tq // ATT_SPLIT
    qi = pl.program_id(2)
    chains = [(hh, slice(hh * LANE, (hh + 1) * LANE), slice(hh * MLA_V, (hh + 1) * MLA_V),
               part, slice(part * tw, (part + 1) * tw))
              for hh in range(2) for part in range(ATT_SPLIT)]
    def with_ones(v_t):
        return jnp.concatenate([v_t, jnp.ones((ATT_ONES, v_t.shape[1]), BF16)], axis=0)

    def n_keys(part, diagonal):
        return (part + 1) * tw if diagonal else tq

    def meta_scores():
        return [_dot(k_meta_ref[:, sl], qt_ref[0, 0, sl, cols]) for _, sl, _, _, cols in chains]

    def meta_softmax_pv(scores):
        for (hh, sl, vrows, part, cols), s in zip(chains, scores):
            m = jnp.max(s, axis=0, keepdims=True)
            e = jnp.exp2(s - m)
            m_scr[hh, :, cols] = m
            acc_scr[hh, :, cols] = _dot(with_ones(vt_meta_ref[vrows, :]), e.astype(BF16))

    def score(chain, kj, diagonal):
        _, sl, _, part, cols = chain
        return _dot(k_ref[0, kj * tq:kj * tq + n_keys(part, diagonal), sl], qt_ref[0, 0, sl, cols])

    def softmax_pv(chain, kj, s, diagonal):
        hh, sl, vrows, part, cols = chain
        nk = n_keys(part, diagonal)
        if diagonal:
            key = lax.broadcasted_iota(jnp.int32, (nk, tw), 0)
            qry = lax.broadcasted_iota(jnp.int32, (nk, tw), 1) + part * tw
            s = jnp.where(key <= qry, s, NEG_INF)
        m_old = m_scr[hh, :, cols]
        m_new = jnp.maximum(m_old, jnp.max(s, axis=0, keepdims=True))
        alpha = jnp.exp2(m_old - m_new)
        e = jnp.exp2(s - m_new)
        m_scr[hh, :, cols] = m_new
        acc_scr[hh, :, cols] = alpha * acc_scr[hh, :, cols] + _dot(
            with_ones(vt_ref[0, kj, vrows, 0:nk]), e.astype(BF16))

    def run(n_full):
        s_meta = meta_scores()
        scores = [score(c, 0, n_full == 0) for c in chains]
        meta_softmax_pv(s_meta)
        for kj in range(n_full):
            nxt = [score(c, kj + 1, kj + 1 == n_full) for c in chains]
            for c, s in zip(chains, scores):
                softmax_pv(c, kj, s, False)
            scores = nxt
        for c, s in zip(chains, scores):
            softmax_pv(c, n_full, s, True)

    for n_full in range(k_ref.shape[1] // tq):
        pl.when(qi == n_full)(functools.partial(run, n_full))
    out_t = jnp.concatenate([acc_scr[hh, 0:MLA_V, :] / acc_scr[hh, MLA_V:MLA_V + 1, :] for hh in range(2)],
                            axis=0)
    o_ref[0] = out_t.T.astype(BF16)


def _mix_mlp_kernel(*refs, n_mix):
    h_ref = refs[0]
    y_refs = refs[1:1 + n_mix]
    w_out_ref, ln1_g, ln1_b, w1_ref, w2_ref, ln2_g, ln2_b, o_ref = refs[1 + n_mix:]
    sub = h_ref.shape[0] // ROW_SPLIT
    spans = [slice(s * sub, (s + 1) * sub) for s in range(ROW_SPLIT)]
    mixes = []
    for rows in spans:
        mix = None
        row = 0
        for y_ref in y_refs:
            width = y_ref.shape[-1]
            part = _dot(y_ref[rows, :], w_out_ref[row:row + width, :])
            mix = part if mix is None else mix + part
            row += width
        mixes.append(mix)
    h1s = [_layernorm(DN_ALPHA * h_ref[rows, :] + mix, ln1_g[...], ln1_b[...]) for rows, mix in zip(spans, mixes)]
    fs = [_ffn(h1, w1_ref, w2_ref) for h1 in h1s]
    for rows, h1, f in zip(spans, h1s, fs):
        o_ref[rows, :] = _layernorm(DN_ALPHA * h1 + f, ln2_g[...], ln2_b[...])


def _ret_fused_kernel(h_ref, wq_ref, wk_ref, wv_ref, wg_ref, cos_ref, sin_ref, s0_ref, lg_ref, y_ref, s_scr):
    c = RET_CHUNK
    half = RET_QK_DIM // 2
    k_scale = RET_QK_DIM ** -0.5
    log_gamma = lg_ref[0][:, 0:1]
    ii = lax.broadcasted_iota(jnp.int32, (c, c), 0)
    jj = lax.broadcasted_iota(jnp.int32, (c, c), 1)
    diff = (ii - jj).astype(F32)
    decay = jnp.where(diff >= 0, jnp.exp(log_gamma * jnp.maximum(diff, 0.0)), 0.0)
    idx = lax.broadcasted_iota(jnp.int32, (c, 1), 0).astype(F32)
    q_decay = jnp.exp(log_gamma * (idx + 1.0))
    k_decay = jnp.exp(log_gamma * (c - 1.0 - idx))
    chunk_decay = jnp.exp(log_gamma * c)
    s_scr[...] = s0_ref[0]

    wq, wk, wv, wg = (r[...].astype(BF16) for r in (wq_ref, wk_ref, wv_ref, wg_ref))

    def project(ci):
        rows = slice(ci * c, (ci + 1) * c)
        hb = h_ref[0, rows, :].astype(BF16)
        cos, sin = cos_ref[rows, :], sin_ref[rows, :]
        pq = _dot(hb, wq)
        q1, q2 = pq[:, 0:half], pq[:, half:]
        q = jnp.concatenate([q1 * cos - q2 * sin, q1 * sin + q2 * cos], axis=1)
        kcos, ksin = cos * k_scale, sin * k_scale
        pk = _dot(hb, wk)
        k1, k2 = pk[:, 0:half], pk[:, half:]
        k = jnp.concatenate([k1 * kcos - k2 * ksin, k1 * ksin + k2 * kcos], axis=1)
        v = _dot(hb, wv).astype(BF16)
        half_g = 0.5 * _dot(hb, wg)
        gate = (half_g + half_g * jnp.tanh(half_g)).astype(BF16)
        qb, kb = q.astype(BF16), k.astype(BF16)
        scores = (_dot_nt(qb, kb) * decay).astype(BF16)
        return dict(scores=scores, qd=(q * q_decay).astype(BF16), kd=(k * k_decay).astype(BF16), v=v, gate=gate)

    def recur(t):
        s_prev = s_scr[...]
        o = _dot(jnp.concatenate([t["scores"], t["qd"]], axis=1),
                 jnp.concatenate([t["v"], s_prev.astype(BF16)], axis=0))
        s_scr[...] = chunk_decay * s_prev + _dot_tn(t["kd"], t["v"])
        return o

    def finish(ci, o, gate):
        rows = slice(ci * c, (ci + 1) * c)
        o = o * lax.rsqrt(jnp.mean(o * o, axis=-1, keepdims=True) + EPS)
        y_ref[0, rows, :] = (gate.astype(F32) * o).astype(BF16)

    n_chunks = h_ref.shape[1] // c
    cur = project(0)
    prev = None
    for ci in range(n_chunks):
        nxt = project(ci + 1) if ci + 1 < n_chunks else None
        o = recur(cur)
        if prev is not None:
            finish(ci - 1, *prev)
        prev, cur = (o, cur["gate"]), nxt
    finish(n_chunks - 1, *prev)


def _const_spec(shape):
    zeros = (0,) * len(shape)
    return pl.BlockSpec(shape, lambda *_: zeros, pipeline_mode=pl.Buffered(1))


def _params(*semantics):
    return pltpu.CompilerParams(dimension_semantics=semantics, vmem_limit_bytes=VMEM_LIMIT)


def _rope_tables(positions, half):
    inv = ROPE_BASE ** (-jnp.arange(half, dtype=F32) / half)
    ang = positions.astype(F32)[:, None] * inv[None, :]
    return jnp.cos(ang), jnp.sin(ang)


def _mla_rope_tables(positions):
    cos, sin = _rope_tables(positions, MLA_ROPE // 2)
    n = positions.shape[0]
    h = MLA_ROPE // 2
    ones = jnp.ones((n, MLA_NOPE), F32)
    zeros = lambda w: jnp.zeros((n, w), F32)
    c = jnp.concatenate([ones, cos, cos, zeros(LANE - MLA_NOPE - MLA_ROPE)], axis=1)
    s_up = jnp.concatenate([zeros(MLA_NOPE + h), sin, zeros(LANE - MLA_NOPE - MLA_ROPE)], axis=1)
    s_dn = jnp.concatenate([zeros(MLA_NOPE), -sin, zeros(LANE - MLA_NOPE - h)], axis=1)
    return c, s_up, s_dn


def _layer_spec(stacked_shape, layer):
    return pl.BlockSpec((None,) + tuple(stacked_shape[1:]), lambda *_: (layer, 0, 0),
                        pipeline_mode=pl.Buffered(1))


def _mix_mlp_call(name, layer, h, ys, w_out, ln1_g, ln1_b, w1, w2, ln2_g, ln2_b):
    m = h.shape[0]
    tm = ROW_TILE
    row_spec = lambda w: pl.BlockSpec((tm, w), lambda i: (i, 0))
    vec = _const_spec((1, D_MODEL))
    return pl.pallas_call(
        functools.partial(_mix_mlp_kernel, n_mix=len(ys)),
        grid=(m // tm,),
        in_specs=[row_spec(D_MODEL)] + [row_spec(y.shape[1]) for y in ys] + [
            _const_spec(w_out.shape), vec, vec, _layer_spec(w1.shape, layer), _layer_spec(w2.shape, layer),
            vec, vec],
        out_specs=row_spec(D_MODEL),
        out_shape=jax.ShapeDtypeStruct((m, D_MODEL), F32),
        compiler_params=_params("parallel"),
        name=name,
    )(h, *ys, w_out, ln1_g, ln1_b, w1, w2, ln2_g, ln2_b)


def kernel(x, meta_tokens, ev_w_in, ev_conv_w, ev_conv_b, ev_w_rg_a, ev_b_rg_a, ev_w_rg_x, ev_b_rg_x,
           ev_lru_lambda, ev_q_norm_g, ev_w_uq, ev_kv_norm_g, ev_w_ukv, ev_w_out, od_w_in, od_w_out,
           ln_mix_g, ln_mix_b, mlp_w1, mlp_w2, ln_mlp_g, ln_mlp_b):
    bsz, seq, d_model = x.shape
    assert d_model == D_MODEL and meta_tokens.shape == (N_META, D_MODEL)
    assert seq % SEQ_TILE == 0 and seq % RET_CHUNK == 0 and (bsz * seq) % ROW_TILE == 0
    row = lambda v: v.reshape(1, -1).astype(F32)

    w_in0 = ev_w_in[0]
    lat0 = 2 * LRU_WIDTH
    kpe0 = lat0 + MLA_Q_RANK + MLA_KV_RANK
    w_kpe = jnp.zeros((D_MODEL, LANE), F32).at[:, MLA_NOPE:MLA_NOPE + MLA_ROPE].set(w_in0[:, kpe0:])
    w_in_ev = jnp.concatenate([w_in0[:, :kpe0], w_kpe], axis=1).astype(BF16)
    gate_w = jnp.concatenate([ev_w_rg_a[0], ev_w_rg_x[0]], axis=2).astype(BF16)
    w_uq = ev_w_uq[0].reshape(MLA_Q_RANK, MLA_HEADS, MLA_NOPE + MLA_ROPE)
    w_uq = jnp.pad(w_uq, ((0, 0), (0, 0), (0, LANE - MLA_NOPE - MLA_ROPE)))
    w_uq = w_uq.reshape(MLA_Q_RANK, MLA_HEADS * LANE).astype(BF16)
    w_ukv = ev_w_ukv[0].reshape(MLA_KV_RANK, MLA_HEADS, MLA_NOPE + MLA_V)
    w_uk = jnp.pad(w_ukv[:, :, :MLA_NOPE], ((0, 0), (0, 0), (0, LANE - MLA_NOPE)))
    w_uk = w_uk.reshape(MLA_KV_RANK, MLA_HEADS * LANE)
    w_uv = w_ukv[:, :, MLA_NOPE:].reshape(MLA_KV_RANK, MLA_HEADS * MLA_V)
    w_ukv_p = jnp.concatenate([w_uk, w_uv], axis=1).astype(BF16)
    w_uq_t = w_uq.T
    w_uk = w_uk.astype(BF16)
    w_uv_t = w_uv.T.astype(BF16)
    w_out_ev = ev_w_out[0].astype(BF16)
    w_in_od = od_w_in[0]
    w_out_od = od_w_out[0].astype(BF16)
    w1 = mlp_w1.astype(BF16)
    w2 = mlp_w2.astype(BF16)
    conv_w = ev_conv_w[0].astype(F32)
    conv_b, b_a, b_x, lam = row(ev_conv_b[0]), row(ev_b_rg_a[0]), row(ev_b_rg_x[0]), row(ev_lru_lambda[0])
    q_g, kv_g = row(ev_q_norm_g[0]), row(ev_kv_norm_g[0])
    ln = [(row(ln_mix_g[l]), row(ln_mix_b[l]), row(ln_mlp_g[l]), row(ln_mlp_b[l])) for l in range(2)]

    pos = jnp.arange(N_META + seq, dtype=jnp.int32)
    mla_tabs = _mla_rope_tables(pos)
    cos0, sin0 = _rope_tables(pos, MLA_ROPE // 2)
    cos0_t, sin0_t = cos0.T, sin0.T
    cos1, sin1 = _rope_tables(pos, RET_QK_DIM // 2)
    k_scale = RET_QK_DIM ** -0.5
    qk = RET_HEADS * RET_QK_DIM
    log_gamma = jnp.log(1.0 - 2.0 ** (-5.0 - jnp.arange(RET_HEADS, dtype=F32)))
    log_gamma = jnp.broadcast_to(log_gamma[:, None, None], (RET_HEADS, 1, LANE))

    mixw = RET_HEADS * RET_V_DIM
    meta_out_shapes = (jax.ShapeDtypeStruct((SUBLANE, LRU_WIDTH), F32),
                       jax.ShapeDtypeStruct((SUBLANE, LRU_WIDTH), F32),
                       jax.ShapeDtypeStruct((N_META, MLA_HEADS * LANE), BF16),
                       jax.ShapeDtypeStruct((MLA_HEADS * MLA_V, N_META), BF16),
                       jax.ShapeDtypeStruct((RET_HEADS, RET_QK_DIM, RET_V_DIM), F32))
    meta_args = [meta_tokens.astype(F32), w_in_ev, conv_w, conv_b, gate_w, b_a, b_x, lam, q_g, kv_g, w_uq,
                 w_ukv_p, w_uv_t, *[t[:N_META] for t in mla_tabs], w_out_ev, *ln[0][:2]]
    meta_specs = [_const_spec(a.shape) for a in meta_args]
    meta_args += [w1, w2, *ln[0][2:], w_in_od, w_in_od]
    meta_specs += [_layer_spec(w1.shape, 0), _layer_spec(w2.shape, 0), _const_spec(ln[0][2].shape),
                   _const_spec(ln[0][3].shape),
                   pl.BlockSpec((D_MODEL, qk), lambda i: (0, 1), pipeline_mode=pl.Buffered(1)),
                   pl.BlockSpec((D_MODEL, mixw), lambda i: (0, 1), pipeline_mode=pl.Buffered(1))]
    tail_args = [cos1[:N_META] * k_scale, sin1[:N_META] * k_scale, log_gamma]
    meta_args += tail_args
    meta_specs += [_const_spec(a.shape) for a in tail_args]
    meta_out = pl.pallas_call(
        _meta_kernel,
        grid=(1,),
        in_specs=meta_specs,
        out_specs=[pl.BlockSpec(s.shape, lambda i, nd=len(s.shape): (0,) * nd) for s in meta_out_shapes],
        out_shape=meta_out_shapes,
        scratch_shapes=[pltpu.VMEM((SUBLANE + N_META, LRU_WIDTH), F32),
                        pltpu.VMEM((N_META, LRU_WIDTH), F32),
                        pltpu.VMEM((N_META, LRU_WIDTH), F32)],
        compiler_params=_params("arbitrary"),
        name="meta_tokens",
    )(*meta_args)
    rec_tail, h_tail, k_meta, vt_meta, s_meta = meta_out

    ts = SEQ_TILE
    nt = seq // ts
    tab_spec = pl.BlockSpec((ts, LANE), lambda b, t: (t, 0))
    tab_t_spec = pl.BlockSpec((MLA_ROPE // 2, ts), lambda b, t: (0, t))
    seq_spec = lambda w: pl.BlockSpec((1, ts, w), lambda b, t: (b, t, 0))
    seq_t_spec = lambda w: pl.BlockSpec((1, 1, w, ts), lambda b, t: (b, t, 0, 0))
    y_rec, qt0, k0, vt0 = pl.pallas_call(
        _seq0_kernel,
        grid=(bsz, nt),
        in_specs=[seq_spec(D_MODEL), _const_spec(w_in_ev.shape), _const_spec(conv_w.shape),
                  _const_spec(conv_b.shape), _const_spec(gate_w.shape), _const_spec(b_a.shape),
                  _const_spec(b_x.shape), _const_spec(lam.shape), _const_spec(q_g.shape),
                  _const_spec(kv_g.shape), _const_spec(w_uq_t.shape), _const_spec(w_uk.shape),
                  _const_spec(w_uv_t.shape), tab_spec, tab_spec, tab_spec, tab_t_spec, tab_t_spec,
                  _const_spec(rec_tail.shape), _const_spec(h_tail.shape)],
        out_specs=[seq_spec(LRU_WIDTH), seq_t_spec(MLA_HEADS * LANE), seq_spec(MLA_HEADS * LANE),
                   seq_t_spec(MLA_HEADS * MLA_V)],
        out_shape=(jax.ShapeDtypeStruct((bsz, seq, LRU_WIDTH), BF16),
                   jax.ShapeDtypeStruct((bsz, nt, MLA_HEADS * LANE, ts), BF16),
                   jax.ShapeDtypeStruct((bsz, seq, MLA_HEADS * LANE), BF16),
                   jax.ShapeDtypeStruct((bsz, nt, MLA_HEADS * MLA_V, ts), BF16)),
        scratch_shapes=[pltpu.VMEM((SUBLANE + ts // SEQ_SPLIT, LRU_WIDTH), F32),
                        pltpu.VMEM((LRU_WIDTH // LANE, ts, LANE), F32),
                        pltpu.VMEM((LRU_WIDTH // LANE, ts, LANE), F32),
                        pltpu.VMEM((ts, LRU_WIDTH), F32),
                        pltpu.VMEM((SUBLANE, LRU_WIDTH), F32)],
        compiler_params=_params("parallel", "arbitrary"),
        name="seq0_mixer_proj",
    )(x, w_in_ev, conv_w, conv_b, gate_w, b_a, b_x, lam, q_g, kv_g, w_uq_t, w_uk, w_uv_t,
      *[t[N_META:] for t in mla_tabs], cos0_t[:, N_META:], sin0_t[:, N_META:], rec_tail, h_tail)

    tq = ATT_TILE
    assert tq == ts
    y_att = pl.pallas_call(
        _attn_kernel,
        grid=(bsz, MLA_HEADS // 2, seq // tq),
        in_specs=[pl.BlockSpec((1, 1, 2 * LANE, tq), lambda b, j, i: (b, i, j, 0)),
                  pl.BlockSpec((1, seq, 2 * LANE), lambda b, j, i: (b, 0, j)),
                  pl.BlockSpec((1, nt, LANE, tq), lambda b, j, i: (b, 0, j, 0)),
                  pl.BlockSpec((N_META, 2 * LANE), lambda b, j, i: (0, j)),
                  pl.BlockSpec((LANE, N_META), lambda b, j, i: (j, 0))],
        out_specs=pl.BlockSpec((1, tq, LANE), lambda b, j, i: (b, i, j)),
        out_shape=jax.ShapeDtypeStruct((bsz, seq, MLA_HEADS * MLA_V), BF16),
        scratch_shapes=[pltpu.VMEM((2, 1, tq), F32), pltpu.VMEM((2, MLA_V + ATT_ONES, tq), F32)],
        compiler_params=_params("parallel", "parallel", "arbitrary"),
        name="mla_attention",
    )(qt0, k0, vt0, k_meta, vt_meta)

    m = bsz * seq
    h1 = _mix_mlp_call("layer0_out_mlp", 0, x.reshape(m, D_MODEL),
                       [y_rec.reshape(m, -1), y_att.reshape(m, -1)], w_out_ev,
                       ln[0][0], ln[0][1], w1, w2, ln[0][2], ln[0][3])

    mixw = RET_HEADS * RET_V_DIM
    col_spec = lambda width, first: pl.BlockSpec((D_MODEL, width), lambda b, h: (0, first + h))
    y_ret = pl.pallas_call(
        _ret_fused_kernel,
        grid=(bsz, RET_HEADS),
        in_specs=[pl.BlockSpec((1, seq, D_MODEL), lambda b, h: (b, 0, 0)),
                  col_spec(RET_QK_DIM, 0), col_spec(RET_QK_DIM, RET_HEADS),
                  col_spec(RET_V_DIM, RET_HEADS), col_spec(RET_V_DIM, 2 * RET_HEADS),
                  _const_spec((seq, RET_QK_DIM // 2)), _const_spec((seq, RET_QK_DIM // 2)),
                  pl.BlockSpec((1, RET_QK_DIM, RET_V_DIM), lambda b, h: (h, 0, 0)),
                  pl.BlockSpec((1, 1, LANE), lambda b, h: (h, 0, 0))],
        out_specs=pl.BlockSpec((1, seq, RET_V_DIM), lambda b, h: (b, 0, h)),
        out_shape=jax.ShapeDtypeStruct((bsz, seq, mixw), BF16),
        scratch_shapes=[pltpu.VMEM((RET_QK_DIM, RET_V_DIM), F32)],
        compiler_params=_params("parallel", "arbitrary"),
        name="layer1_retention",
    )(h1.reshape(bsz, seq, D_MODEL), w_in_od, w_in_od, w_in_od, w_in_od, cos1[N_META:], sin1[N_META:],
      s_meta, log_gamma)

    out = _mix_mlp_call("layer1_out_mlp", 1, h1, [y_ret.reshape(m, mixw)], w_out_od,
                        ln[1][0], ln[1][1], w1, w2, ln[1][2], ln[1][3])
    return out.reshape(bsz, seq, D_MODEL)
```

```python
import functools
import math

import jax
import jax.numpy as jnp
from jax import lax
from jax.experimental import pallas as pl
from jax.experimental.pallas import tpu as pltpu

D_MODEL = 1024
N_META = 16
LRU_WIDTH = 512
LRU_HEADS = 4
LRU_HEAD_DIM = 128
CONV_WIDTH = 4
LRU_C = 8.0
MLA_HEADS = 8
MLA_NOPE = 64
MLA_ROPE = 32
MLA_V = 64
MLA_Q_RANK = 256
MLA_KV_RANK = 128
RET_HEADS = 4
RET_QK_DIM = 256
RET_V_DIM = 512
D_FF = 4096
ROPE_BASE = 10000.0
DN_ALPHA = 4.0 ** 0.25
EPS = 1e-5
NEG_INF = -1e30

LANE = 128
SUBLANE = 8
VMEM_LIMIT = 56 * 1024 * 1024

BF16 = jnp.bfloat16
F32 = jnp.float32

SEQ_TILE = 512
SEQ_SPLIT = 2
SCAN_STEP = 4
SCAN_BLOCK = SUBLANE * SCAN_STEP
ATT_TILE = 512
ATT_SPLIT = 2
ATT_ONES = 16
ROW_TILE = 512
ROW_SPLIT = 2
FF_TILE = 1024
RET_CHUNK = 256


def _dot(a, b):
    return jnp.dot(a, b, preferred_element_type=F32)


def _dot_nt(a, b):
    return lax.dot_general(a, b, (((1,), (1,)), ((), ())), preferred_element_type=F32)


def _dot_tn(a, b):
    return lax.dot_general(a, b, (((0,), (0,)), ((), ())), preferred_element_type=F32)


def _layernorm(x, g, b):
    mu = jnp.mean(x, axis=-1, keepdims=True)
    xc = x - mu
    var = jnp.mean(xc * xc, axis=-1, keepdims=True)
    return xc * lax.rsqrt(var + EPS) * g + b


def _rmsnorm(x, g):
    return x * lax.rsqrt(jnp.mean(x * x, axis=-1, keepdims=True) + EPS) * g


def _rope_mla(x, c, s_up, s_dn):
    return x * c + pltpu.roll(x, MLA_ROPE // 2, 1) * s_up + pltpu.roll(x, LANE - MLA_ROPE // 2, 1) * s_dn


def _lru_gates(xc, gate_w_ref, b_a, b_x, sp_lambda):
    rs, is_ = [], []
    for h in range(LRU_HEADS):
        g = _dot(xc[:, h * LRU_HEAD_DIM:(h + 1) * LRU_HEAD_DIM].astype(BF16), gate_w_ref[h])
        rs.append(g[:, :LRU_HEAD_DIM])
        is_.append(g[:, LRU_HEAD_DIM:])
    r = jax.nn.sigmoid(jnp.concatenate(rs, axis=1) + b_a)
    i = jax.nn.sigmoid(jnp.concatenate(is_, axis=1) + b_x)
    log_a = -LRU_C * r * sp_lambda
    a = jnp.exp(log_a)
    y = 1.0 - a * a
    mult = jnp.where(y > 0.0, y * lax.rsqrt(y), 0.0)
    return a, mult * (i * xc)


def _scan8(a, b):
    row = lax.broadcasted_iota(jnp.int32, a.shape, 0)
    for k in (1, 2, 4):
        keep = row >= k
        a_prev = jnp.where(keep, pltpu.roll(a, k, 0), 1.0)
        b_prev = jnp.where(keep, pltpu.roll(b, k, 0), 0.0)
        b = a * b_prev + b
        a = a * a_prev
    return a, b


def _lru_scan(a_ref, b_ref, h0, rows):
    def body(g, h_prev):
        sl = pl.ds(pl.multiple_of(g * SUBLANE, SUBLANE), SUBLANE)
        a_c, b_c = _scan8(a_ref[sl, :], b_ref[sl, :])
        h = a_c * h_prev + b_c
        b_ref[sl, :] = h
        return h[SUBLANE - 1:SUBLANE, :]
    return lax.fori_loop(0, rows // SUBLANE, body, h0, unroll=4)


def _lru_scan_blocked(a_ref, b_ref, h0, rows):
    n_slabs = a_ref.shape[0]
    row = lax.broadcasted_iota(jnp.int32, (SUBLANE, LANE), 0)

    def body(g, carry):
        base = g * SCAN_BLOCK
        out = []
        for c in range(n_slabs):
            idx = [pl.ds(base + l, SUBLANE, stride=SCAN_STEP) for l in range(SCAN_STEP)]
            a = [a_ref[c, i, :] for i in idx]
            h = [b_ref[c, idx[0], :]]
            p = [a[0]]
            for l in range(1, SCAN_STEP):
                h.append(a[l] * h[l - 1] + b_ref[c, idx[l], :])
                p.append(a[l] * p[l - 1])
            p_seg, h_seg = _scan8(p[-1], h[-1])
            end = p_seg * carry[c] + h_seg
            start = jnp.where(row == 0, carry[c], pltpu.roll(end, 1, 0))
            for l in range(SCAN_STEP):
                b_ref[c, idx[l], :] = h[l] + p[l] * start
            out.append(end[SUBLANE - 1:SUBLANE, :])
        return tuple(out)
    return lax.fori_loop(0, rows // SCAN_BLOCK, body, tuple(h0), unroll=2)


def _mla_project(qlat, kvlat, kpe, q_g, kv_g, w_uq_ref, w_ukv_ref, rope_c, rope_up, rope_dn):
    scale = (MLA_NOPE + MLA_ROPE) ** -0.5
    q_all = _dot(_rmsnorm(qlat, q_g).astype(BF16), w_uq_ref[...])
    kv_all = _dot(_rmsnorm(kvlat, kv_g).astype(BF16), w_ukv_ref[...])
    kpe_r = _rope_mla(kpe, rope_c, rope_up, rope_dn)
    qs, ks = [], []
    for h in range(MLA_HEADS):
        sl = slice(h * LANE, (h + 1) * LANE)
        qs.append((_rope_mla(q_all[:, sl], rope_c, rope_up, rope_dn) * scale).astype(BF16))
        ks.append((kv_all[:, sl] + kpe_r).astype(BF16))
    v = kv_all[:, MLA_HEADS * LANE:].astype(BF16)
    return jnp.concatenate(qs, axis=1), jnp.concatenate(ks, axis=1), v


def _ffn(h1, w1_ref, w2_ref):
    h1b = h1.astype(BF16)
    f = None
    for c in range(D_FF // FF_TILE):
        a = _dot(h1b, w1_ref[:, c * FF_TILE:(c + 1) * FF_TILE])
        a = jnp.maximum(a, 0.0)
        part = _dot((a * a).astype(BF16), w2_ref[c * FF_TILE:(c + 1) * FF_TILE, :])
        f = part if f is None else f + part
    return f


def _mlp_block(h_in, mix, ln1_g, ln1_b, w1_ref, w2_ref, ln2_g, ln2_b):
    h1 = _layernorm(DN_ALPHA * h_in + mix, ln1_g, ln1_b)
    return _layernorm(DN_ALPHA * h1 + _ffn(h1, w1_ref, w2_ref), ln2_g, ln2_b)


def _meta_kernel(meta_ref, w_in_ref, conv_w_ref, conv_b_ref, gate_w_ref, b_a_ref, b_x_ref, lam_ref,
                 q_g_ref, kv_g_ref, w_uq_ref, w_ukv_ref, w_uvt_ref, rope_c_ref, rope_up_ref, rope_dn_ref,
                 w_out_ref, ln1_g_ref, ln1_b_ref, w1_ref, w2_ref, ln2_g_ref, ln2_b_ref,
                 w_k_ref, w_v_ref, cos1_ref, sin1_ref, lg_ref,
                 rec_tail_ref, h_tail_ref, k_meta_ref, vt_meta_ref, s_meta_ref,
                 conv_scr, a_scr, b_scr):
    n = N_META
    x = meta_ref[...]
    p = _dot(x.astype(BF16), w_in_ref[...])
    gate, rec = p[:, :LRU_WIDTH], p[:, LRU_WIDTH:2 * LRU_WIDTH]
    conv_scr[0:SUBLANE, :] = jnp.zeros((SUBLANE, LRU_WIDTH), F32)
    conv_scr[SUBLANE:SUBLANE + n, :] = rec
    cw = conv_w_ref[...]
    xc = conv_b_ref[...] + cw[3:4, :] * rec
    for j in range(CONV_WIDTH - 1):
        off = SUBLANE - (CONV_WIDTH - 1) + j
        xc = xc + cw[j:j + 1, :] * conv_scr[off:off + n, :]
    sp_lambda = jax.nn.softplus(-lam_ref[...])
    a, b = _lru_gates(xc, gate_w_ref, b_a_ref[...], b_x_ref[...], sp_lambda)
    a_scr[...] = a
    b_scr[...] = b
    _lru_scan(a_scr, b_scr, jnp.zeros((1, LRU_WIDTH), F32), n)
    h = b_scr[...]
    y_rec = (h * jax.nn.gelu(gate)).astype(BF16)
    rec_tail_ref[...] = rec[n - SUBLANE:, :]
    h_tail_ref[...] = h[n - SUBLANE:, :]

    off = 2 * LRU_WIDTH
    q, k, v = _mla_project(p[:, off:off + MLA_Q_RANK],
                           p[:, off + MLA_Q_RANK:off + MLA_Q_RANK + MLA_KV_RANK],
                           p[:, off + MLA_Q_RANK + MLA_KV_RANK:],
                           q_g_ref[...], kv_g_ref[...], w_uq_ref, w_ukv_ref,
                           rope_c_ref[...], rope_up_ref[...], rope_dn_ref[...])
    k_meta_ref[...] = k
    kvn = _rmsnorm(p[:, off + MLA_Q_RANK:off + MLA_Q_RANK + MLA_KV_RANK], kv_g_ref[...]).astype(BF16)
    vt_meta_ref[...] = _dot_nt(w_uvt_ref[...], kvn).astype(BF16)
    causal = (lax.broadcasted_iota(jnp.int32, (n, n), 1) <= lax.broadcasted_iota(jnp.int32, (n, n), 0))
    outs = []
    for hh in range(MLA_HEADS):
        sl = slice(hh * LANE, (hh + 1) * LANE)
        s = jnp.where(causal, _dot_nt(q[:, sl], k[:, sl]), NEG_INF)
        e = jnp.exp(s - jnp.max(s, axis=-1, keepdims=True))
        pr = e / jnp.sum(e, axis=-1, keepdims=True)
        outs.append(_dot(pr.astype(BF16), v[:, hh * MLA_V:(hh + 1) * MLA_V]))
    y_att = jnp.concatenate(outs, axis=1).astype(BF16)
    mix = _dot(y_rec, w_out_ref[0:LRU_WIDTH, :]) + _dot(y_att, w_out_ref[LRU_WIDTH:, :])
    h2 = _mlp_block(x, mix, ln1_g_ref[...], ln1_b_ref[...], w1_ref, w2_ref, ln2_g_ref[...], ln2_b_ref[...])

    h2b = h2.astype(BF16)
    kk = _dot(h2b, w_k_ref[...].astype(BF16))
    vv = _dot(h2b, w_v_ref[...].astype(BF16)).astype(BF16)
    cos, sin = cos1_ref[...], sin1_ref[...]
    idx = lax.broadcasted_iota(jnp.int32, (n, 1), 0).astype(F32)
    half = RET_QK_DIM // 2
    for hh in range(RET_HEADS):
        log_gamma = lg_ref[hh][:, 0:1]
        k1 = kk[:, hh * RET_QK_DIM:hh * RET_QK_DIM + half]
        k2 = kk[:, hh * RET_QK_DIM + half:(hh + 1) * RET_QK_DIM]
        kr = jnp.concatenate([k1 * cos - k2 * sin, k1 * sin + k2 * cos], axis=1).astype(BF16)
        k_dec = jnp.exp(log_gamma * (n - 1.0 - idx))
        kd = (kr.astype(F32) * k_dec).astype(BF16)
        s_meta_ref[hh] = _dot_tn(kd, vv[:, hh * RET_V_DIM:(hh + 1) * RET_V_DIM])


def _seq0_kernel(x_ref, w_in_ref, conv_w_ref, conv_b_ref, gate_w_ref, b_a_ref, b_x_ref, lam_ref,
                 q_g_ref, kv_g_ref, w_uqt_ref, w_uk_ref, w_uvt_ref, rope_c_ref, rope_up_ref, rope_dn_ref,
                 cos_t_ref, sin_t_ref, rec_tail_ref, h_tail_ref,
                 y_rec_ref, qt_ref, k_ref, vt_ref,
                 conv_scr, a_scr, b_scr, g_scr, h_scr):
    ts = SEQ_TILE
    t = pl.program_id(1)

    n_slabs = LRU_WIDTH // LANE

    @pl.when(t == 0)
    def _():
        conv_scr[0:SUBLANE, :] = rec_tail_ref[...]
        h_scr[...] = h_tail_ref[...]

    sub = ts // SEQ_SPLIT
    spans = [slice(i * sub, (i + 1) * sub) for i in range(SEQ_SPLIT)]
    ps = [_dot(x_ref[0, rows, :].astype(BF16), w_in_ref[...]) for rows in spans]
    cw = conv_w_ref[...]
    sp_lambda = jax.nn.softplus(-lam_ref[...])
    scale = (MLA_NOPE + MLA_ROPE) ** -0.5 * math.log2(math.e)
    hr = MLA_ROPE // 2
    off = 2 * LRU_WIDTH
    for rows, p in zip(spans, ps):
        gate, rec = p[:, :LRU_WIDTH], p[:, LRU_WIDTH:2 * LRU_WIDTH]
        conv_scr[SUBLANE:SUBLANE + sub, :] = rec
        xc = conv_b_ref[...] + cw[3:4, :] * rec
        for j in range(CONV_WIDTH - 1):
            o = SUBLANE - (CONV_WIDTH - 1) + j
            xc = xc + cw[j:j + 1, :] * conv_scr[o:o + sub, :]
        conv_scr[0:SUBLANE, :] = rec[sub - SUBLANE:, :]
        a, b = _lru_gates(xc, gate_w_ref, b_a_ref[...], b_x_ref[...], sp_lambda)
        for c in range(LRU_WIDTH // LANE):
            a_scr[c, rows, :] = a[:, c * LANE:(c + 1) * LANE]
            b_scr[c, rows, :] = b[:, c * LANE:(c + 1) * LANE]
        g_scr[rows, :] = jax.nn.gelu(gate)

        qn = _rmsnorm(p[:, off:off + MLA_Q_RANK], q_g_ref[...]).astype(BF16)
        q_t = _dot_nt(w_uqt_ref[...], qn)
        cos_t, sin_t = cos_t_ref[:, rows] * scale, sin_t_ref[:, rows] * scale
        for h in range(MLA_HEADS):
            base = h * LANE
            x1 = q_t[base + MLA_NOPE:base + MLA_NOPE + hr, :]
            x2 = q_t[base + MLA_NOPE + hr:base + MLA_NOPE + MLA_ROPE, :]
            qt_ref[0, 0, base:base + MLA_NOPE, rows] = (q_t[base:base + MLA_NOPE, :] * scale).astype(BF16)
            qt_ref[0, 0, base + MLA_NOPE:base + MLA_NOPE + hr, rows] = (x1 * cos_t - x2 * sin_t).astype(BF16)
            qt_ref[0, 0, base + MLA_NOPE + hr:base + MLA_NOPE + MLA_ROPE, rows] = (
                x1 * sin_t + x2 * cos_t).astype(BF16)
            qt_ref[0, 0, base + MLA_NOPE + MLA_ROPE:base + LANE, rows] = jnp.zeros(
                (LANE - MLA_NOPE - MLA_ROPE, sub), BF16)
        kvn = _rmsnorm(p[:, off + MLA_Q_RANK:off + MLA_Q_RANK + MLA_KV_RANK], kv_g_ref[...]).astype(BF16)
        k_nope = _dot(kvn, w_uk_ref[...])
        kpe_r = _rope_mla(p[:, off + MLA_Q_RANK + MLA_KV_RANK:],
                          rope_c_ref[rows, :], rope_up_ref[rows, :], rope_dn_ref[rows, :])
        for h in range(MLA_HEADS):
            sl = slice(h * LANE, (h + 1) * LANE)
            k_ref[0, rows, sl] = (k_nope[:, sl] + kpe_r).astype(BF16)
        vt_ref[0, 0, :, rows] = _dot_nt(w_uvt_ref[...], kvn).astype(BF16)

    h0 = [h_scr[SUBLANE - 1:SUBLANE, c * LANE:(c + 1) * LANE] for c in range(n_slabs)]
    h_last = _lru_scan_blocked(a_scr, b_scr, h0, ts)
    h_scr[SUBLANE - 1:SUBLANE, :] = jnp.concatenate(h_last, axis=1)
    h = jnp.concatenate([b_scr[c] for c in range(n_slabs)], axis=1)
    y_rec_ref[0] = (h * g_scr[...]).astype(BF16)


def _attn_kernel(qt_ref, k_ref, vt_ref, k_meta_ref, vt_meta_ref, o_ref, m_scr, acc_scr):
    tq = ATT_TILE
    tw = tq // ATT_SPLIT
    chains = [(hh, slice(hh * LANE, (hh + 1) * LANE), slice(hh * MLA_V, (hh + 1) * MLA_V),
               part, slice(part * tw, (part + 1) * tw))
              for hh in range(2) for part in range(ATT_SPLIT)]
    def with_ones(v_t):
        return jnp.concatenate([v_t, jnp.ones((ATT_ONES, v_t.shape[1]), BF16)], axis=0)

    def n_keys(part, diagonal):
        return (part + 1) * tw if diagonal else tq

    def meta_scores(qi):
        return [_dot(k_meta_ref[:, sl], qt_ref[0, qi, sl, cols]) for _, sl, _, _, cols in chains]

    def meta_softmax_pv(qi, scores):
        for (hh, sl, vrows, part, cols), s in zip(chains, scores):
            m = jnp.max(s, axis=0, keepdims=True)
            e = jnp.exp2(s - m)
            m_scr[qi, hh, :, cols] = m
            acc_scr[qi, hh, :, cols] = _dot(with_ones(vt_meta_ref[vrows, :]), e.astype(BF16))

    def score(qi, chain, kj, diagonal):
        _, sl, _, part, cols = chain
        return _dot(k_ref[0, kj * tq:kj * tq + n_keys(part, diagonal), sl], qt_ref[0, qi, sl, cols])

    def softmax_pv(qi, chain, kj, s, diagonal):
        hh, sl, vrows, part, cols = chain
        nk = n_keys(part, diagonal)
        if diagonal:
            key = lax.broadcasted_iota(jnp.int32, (nk, tw), 0)
            qry = lax.broadcasted_iota(jnp.int32, (nk, tw), 1) + part * tw
            s = jnp.where(key <= qry, s, NEG_INF)
        m_old = m_scr[qi, hh, :, cols]
        m_new = jnp.maximum(m_old, jnp.max(s, axis=0, keepdims=True))
        alpha = jnp.exp2(m_old - m_new)
        e = jnp.exp2(s - m_new)
        m_scr[qi, hh, :, cols] = m_new
        acc_scr[qi, hh, :, cols] = alpha * acc_scr[qi, hh, :, cols] + _dot(
            with_ones(vt_ref[0, kj, vrows, 0:nk]), e.astype(BF16))

    def run(qi):
        s_meta = meta_scores(qi)
        scores = [score(qi, c, 0, qi == 0) for c in chains]
        meta_softmax_pv(qi, s_meta)
        for kj in range(qi):
            nxt = [score(qi, c, kj + 1, kj + 1 == qi) for c in chains]
            for c, s in zip(chains, scores):
                softmax_pv(qi, c, kj, s, False)
            scores = nxt
        for c, s in zip(chains, scores):
            softmax_pv(qi, c, qi, s, True)
        out_t = jnp.concatenate([acc_scr[qi, hh, 0:MLA_V, :] / acc_scr[qi, hh, MLA_V:MLA_V + 1, :]
                                 for hh in range(2)], axis=0)
        o_ref[0, qi * tq:(qi + 1) * tq, :] = out_t.T.astype(BF16)

    for qi in range(k_ref.shape[1] // tq):
        run(qi)


def _mix_mlp_kernel(*refs, n_mix):
    h_ref = refs[0]
    y_refs = refs[1:1 + n_mix]
    w_out_ref, ln1_g, ln1_b, w1_ref, w2_ref, ln2_g, ln2_b, o_ref = refs[1 + n_mix:]
    sub = h_ref.shape[0] // ROW_SPLIT
    spans = [slice(s * sub, (s + 1) * sub) for s in range(ROW_SPLIT)]
    mixes = []
    for rows in spans:
        mix = None
        row = 0
        for y_ref in y_refs:
            width = y_ref.shape[-1]
            part = _dot(y_ref[rows, :], w_out_ref[row:row + width, :])
            mix = part if mix is None else mix + part
            row += width
        mixes.append(mix)
    h1s = [_layernorm(DN_ALPHA * h_ref[rows, :] + mix, ln1_g[...], ln1_b[...]) for rows, mix in zip(spans, mixes)]
    fs = [_ffn(h1, w1_ref, w2_ref) for h1 in h1s]
    for rows, h1, f in zip(spans, h1s, fs):
        o_ref[rows, :] = _layernorm(DN_ALPHA * h1 + f, ln2_g[...], ln2_b[...])


def _ret_fused_kernel(h_ref, wq_ref, wk_ref, wv_ref, wg_ref, cos_ref, sin_ref, s0_ref, lg_ref, y_ref, s_scr):
    c = RET_CHUNK
    half = RET_QK_DIM // 2
    k_scale = RET_QK_DIM ** -0.5
    log_gamma = lg_ref[0][:, 0:1]
    ii = lax.broadcasted_iota(jnp.int32, (c, c), 0)
    jj = lax.broadcasted_iota(jnp.int32, (c, c), 1)
    diff = (ii - jj).astype(F32)
    decay = jnp.where(diff >= 0, jnp.exp(log_gamma * jnp.maximum(diff, 0.0)), 0.0)
    idx = lax.broadcasted_iota(jnp.int32, (c, 1), 0).astype(F32)
    q_decay = jnp.exp(log_gamma * (idx + 1.0))
    k_decay = jnp.exp(log_gamma * (c - 1.0 - idx))
    chunk_decay = jnp.exp(log_gamma * c)
    s_scr[...] = s0_ref[0]

    wq, wk, wv, wg = (r[...].astype(BF16) for r in (wq_ref, wk_ref, wv_ref, wg_ref))

    def project(ci):
        rows = slice(ci * c, (ci + 1) * c)
        hb = h_ref[0, rows, :].astype(BF16)
        cos, sin = cos_ref[rows, :], sin_ref[rows, :]
        pq = _dot(hb, wq)
        q1, q2 = pq[:, 0:half], pq[:, half:]
        q = jnp.concatenate([q1 * cos - q2 * sin, q1 * sin + q2 * cos], axis=1)
        kcos, ksin = cos * k_scale, sin * k_scale
        pk = _dot(hb, wk)
        k1, k2 = pk[:, 0:half], pk[:, half:]
        k = jnp.concatenate([k1 * kcos - k2 * ksin, k1 * ksin + k2 * kcos], axis=1)
        v = _dot(hb, wv).astype(BF16)
        half_g = 0.5 * _dot(hb, wg)
        gate = (half_g + half_g * jnp.tanh(half_g)).astype(BF16)
        qb, kb = q.astype(BF16), k.astype(BF16)
        scores = (_dot_nt(qb, kb) * decay).astype(BF16)
        return dict(scores=scores, qd=(q * q_decay).astype(BF16), kd=(k * k_decay).astype(BF16), v=v, gate=gate)

    def recur(t):
        s_prev = s_scr[...]
        o = _dot(jnp.concatenate([t["scores"], t["qd"]], axis=1),
                 jnp.concatenate([t["v"], s_prev.astype(BF16)], axis=0))
        s_scr[...] = chunk_decay * s_prev + _dot_tn(t["kd"], t["v"])
        return o

    def finish(ci, o, gate):
        rows = slice(ci * c, (ci + 1) * c)
        o = o * lax.rsqrt(jnp.mean(o * o, axis=-1, keepdims=True) + EPS)
        y_ref[0, rows, :] = (gate.astype(F32) * o).astype(BF16)

    n_chunks = h_ref.shape[1] // c
    cur = project(0)
    prev = None
    for ci in range(n_chunks):
        nxt = project(ci + 1) if ci + 1 < n_chunks else None
        o = recur(cur)
        if prev is not None:
            finish(ci - 1, *prev)
        prev, cur = (o, cur["gate"]), nxt
    finish(n_chunks - 1, *prev)


def _const_spec(shape):
    zeros = (0,) * len(shape)
    return pl.BlockSpec(shape, lambda *_: zeros, pipeline_mode=pl.Buffered(1))


def _params(*semantics):
    return pltpu.CompilerParams(dimension_semantics=semantics, vmem_limit_bytes=VMEM_LIMIT)


def _rope_tables(positions, half):
    inv = ROPE_BASE ** (-jnp.arange(half, dtype=F32) / half)
    ang = positions.astype(F32)[:, None] * inv[None, :]
    return jnp.cos(ang), jnp.sin(ang)


def _mla_rope_tables(positions):
    cos, sin = _rope_tables(positions, MLA_ROPE // 2)
    n = positions.shape[0]
    h = MLA_ROPE // 2
    ones = jnp.ones((n, MLA_NOPE), F32)
    zeros = lambda w: jnp.zeros((n, w), F32)
    c = jnp.concatenate([ones, cos, cos, zeros(LANE - MLA_NOPE - MLA_ROPE)], axis=1)
    s_up = jnp.concatenate([zeros(MLA_NOPE + h), sin, zeros(LANE - MLA_NOPE - MLA_ROPE)], axis=1)
    s_dn = jnp.concatenate([zeros(MLA_NOPE), -sin, zeros(LANE - MLA_NOPE - h)], axis=1)
    return c, s_up, s_dn


def _layer_spec(stacked_shape, layer):
    return pl.BlockSpec((None,) + tuple(stacked_shape[1:]), lambda *_: (layer, 0, 0),
                        pipeline_mode=pl.Buffered(1))


def _mix_mlp_call(name, layer, h, ys, w_out, ln1_g, ln1_b, w1, w2, ln2_g, ln2_b):
    m = h.shape[0]
    tm = ROW_TILE
    row_spec = lambda w: pl.BlockSpec((tm, w), lambda i: (i, 0))
    vec = _const_spec((1, D_MODEL))
    return pl.pallas_call(
        functools.partial(_mix_mlp_kernel, n_mix=len(ys)),
        grid=(m // tm,),
        in_specs=[row_spec(D_MODEL)] + [row_spec(y.shape[1]) for y in ys] + [
            _const_spec(w_out.shape), vec, vec, _layer_spec(w1.shape, layer), _layer_spec(w2.shape, layer),
            vec, vec],
        out_specs=row_spec(D_MODEL),
        out_shape=jax.ShapeDtypeStruct((m, D_MODEL), F32),
        compiler_params=_params("parallel"),
        name=name,
    )(h, *ys, w_out, ln1_g, ln1_b, w1, w2, ln2_g, ln2_b)


def kernel(x, meta_tokens, ev_w_in, ev_conv_w, ev_conv_b, ev_w_rg_a, ev_b_rg_a, ev_w_rg_x, ev_b_rg_x,
           ev_lru_lambda, ev_q_norm_g, ev_w_uq, ev_kv_norm_g, ev_w_ukv, ev_w_out, od_w_in, od_w_out,
           ln_mix_g, ln_mix_b, mlp_w1, mlp_w2, ln_mlp_g, ln_mlp_b):
    bsz, seq, d_model = x.shape
    assert d_model == D_MODEL and meta_tokens.shape == (N_META, D_MODEL)
    assert seq % SEQ_TILE == 0 and seq % RET_CHUNK == 0 and (bsz * seq) % ROW_TILE == 0
    row = lambda v: v.reshape(1, -1).astype(F32)

    w_in0 = ev_w_in[0]
    lat0 = 2 * LRU_WIDTH
    kpe0 = lat0 + MLA_Q_RANK + MLA_KV_RANK
    w_kpe = jnp.zeros((D_MODEL, LANE), F32).at[:, MLA_NOPE:MLA_NOPE + MLA_ROPE].set(w_in0[:, kpe0:])
    w_in_ev = jnp.concatenate([w_in0[:, :kpe0], w_kpe], axis=1).astype(BF16)
    gate_w = jnp.concatenate([ev_w_rg_a[0], ev_w_rg_x[0]], axis=2).astype(BF16)
    w_uq = ev_w_uq[0].reshape(MLA_Q_RANK, MLA_HEADS, MLA_NOPE + MLA_ROPE)
    w_uq = jnp.pad(w_uq, ((0, 0), (0, 0), (0, LANE - MLA_NOPE - MLA_ROPE)))
    w_uq = w_uq.reshape(MLA_Q_RANK, MLA_HEADS * LANE).astype(BF16)
    w_ukv = ev_w_ukv[0].reshape(MLA_KV_RANK, MLA_HEADS, MLA_NOPE + MLA_V)
    w_uk = jnp.pad(w_ukv[:, :, :MLA_NOPE], ((0, 0), (0, 0), (0, LANE - MLA_NOPE)))
    w_uk = w_uk.reshape(MLA_KV_RANK, MLA_HEADS * LANE)
    w_uv = w_ukv[:, :, MLA_NOPE:].reshape(MLA_KV_RANK, MLA_HEADS * MLA_V)
    w_ukv_p = jnp.concatenate([w_uk, w_uv], axis=1).astype(BF16)
    w_uq_t = w_uq.T
    w_uk = w_uk.astype(BF16)
    w_uv_t = w_uv.T.astype(BF16)
    w_out_ev = ev_w_out[0].astype(BF16)
    w_in_od = od_w_in[0]
    w_out_od = od_w_out[0].astype(BF16)
    w1 = mlp_w1.astype(BF16)
    w2 = mlp_w2.astype(BF16)
    conv_w = ev_conv_w[0].astype(F32)
    conv_b, b_a, b_x, lam = row(ev_conv_b[0]), row(ev_b_rg_a[0]), row(ev_b_rg_x[0]), row(ev_lru_lambda[0])
    q_g, kv_g = row(ev_q_norm_g[0]), row(ev_kv_norm_g[0])
    ln = [(row(ln_mix_g[l]), row(ln_mix_b[l]), row(ln_mlp_g[l]), row(ln_mlp_b[l])) for l in range(2)]

    pos = jnp.arange(N_META + seq, dtype=jnp.int32)
    mla_tabs = _mla_rope_tables(pos)
    cos0, sin0 = _rope_tables(pos, MLA_ROPE // 2)
    cos0_t, sin0_t = cos0.T, sin0.T
    cos1, sin1 = _rope_tables(pos, RET_QK_DIM // 2)
    k_scale = RET_QK_DIM ** -0.5
    qk = RET_HEADS * RET_QK_DIM
    log_gamma = jnp.log(1.0 - 2.0 ** (-5.0 - jnp.arange(RET_HEADS, dtype=F32)))
    log_gamma = jnp.broadcast_to(log_gamma[:, None, None], (RET_HEADS, 1, LANE))

    mixw = RET_HEADS * RET_V_DIM
    meta_out_shapes = (jax.ShapeDtypeStruct((SUBLANE, LRU_WIDTH), F32),
                       jax.ShapeDtypeStruct((SUBLANE, LRU_WIDTH), F32),
                       jax.ShapeDtypeStruct((N_META, MLA_HEADS * LANE), BF16),
                       jax.ShapeDtypeStruct((MLA_HEADS * MLA_V, N_META), BF16),
                       jax.ShapeDtypeStruct((RET_HEADS, RET_QK_DIM, RET_V_DIM), F32))
    meta_args = [meta_tokens.astype(F32), w_in_ev, conv_w, conv_b, gate_w, b_a, b_x, lam, q_g, kv_g, w_uq,
                 w_ukv_p, w_uv_t, *[t[:N_META] for t in mla_tabs], w_out_ev, *ln[0][:2]]
    meta_specs = [_const_spec(a.shape) for a in meta_args]
    meta_args += [w1, w2, *ln[0][2:], w_in_od, w_in_od]
    meta_specs += [_layer_spec(w1.shape, 0), _layer_spec(w2.shape, 0), _const_spec(ln[0][2].shape),
                   _const_spec(ln[0][3].shape),
                   pl.BlockSpec((D_MODEL, qk), lambda i: (0, 1), pipeline_mode=pl.Buffered(1)),
                   pl.BlockSpec((D_MODEL, mixw), lambda i: (0, 1), pipeline_mode=pl.Buffered(1))]
    tail_args = [cos1[:N_META] * k_scale, sin1[:N_META] * k_scale, log_gamma]
    meta_args += tail_args
    meta_specs += [_const_spec(a.shape) for a in tail_args]
    meta_out = pl.pallas_call(
        _meta_kernel,
        grid=(1,),
        in_specs=meta_specs,
        out_specs=[pl.BlockSpec(s.shape, lambda i, nd=len(s.shape): (0,) * nd) for s in meta_out_shapes],
        out_shape=meta_out_shapes,
        scratch_shapes=[pltpu.VMEM((SUBLANE + N_META, LRU_WIDTH), F32),
                        pltpu.VMEM((N_META, LRU_WIDTH), F32),
                        pltpu.VMEM((N_META, LRU_WIDTH), F32)],
        compiler_params=_params("arbitrary"),
        name="meta_tokens",
    )(*meta_args)
    rec_tail, h_tail, k_meta, vt_meta, s_meta = meta_out

    ts = SEQ_TILE
    nt = seq // ts
    tab_spec = pl.BlockSpec((ts, LANE), lambda b, t: (t, 0))
    tab_t_spec = pl.BlockSpec((MLA_ROPE // 2, ts), lambda b, t: (0, t))
    seq_spec = lambda w: pl.BlockSpec((1, ts, w), lambda b, t: (b, t, 0))
    seq_t_spec = lambda w: pl.BlockSpec((1, 1, w, ts), lambda b, t: (b, t, 0, 0))
    y_rec, qt0, k0, vt0 = pl.pallas_call(
        _seq0_kernel,
        grid=(bsz, nt),
        in_specs=[seq_spec(D_MODEL), _const_spec(w_in_ev.shape), _const_spec(conv_w.shape),
                  _const_spec(conv_b.shape), _const_spec(gate_w.shape), _const_spec(b_a.shape),
                  _const_spec(b_x.shape), _const_spec(lam.shape), _const_spec(q_g.shape),
                  _const_spec(kv_g.shape), _const_spec(w_uq_t.shape), _const_spec(w_uk.shape),
                  _const_spec(w_uv_t.shape), tab_spec, tab_spec, tab_spec, tab_t_spec, tab_t_spec,
                  _const_spec(rec_tail.shape), _const_spec(h_tail.shape)],
        out_specs=[seq_spec(LRU_WIDTH), seq_t_spec(MLA_HEADS * LANE), seq_spec(MLA_HEADS * LANE),
                   seq_t_spec(MLA_HEADS * MLA_V)],
        out_shape=(jax.ShapeDtypeStruct((bsz, seq, LRU_WIDTH), BF16),
                   jax.ShapeDtypeStruct((bsz, nt, MLA_HEADS * LANE, ts), BF16),
                   jax.ShapeDtypeStruct((bsz, seq, MLA_HEADS * LANE), BF16),
                   jax.ShapeDtypeStruct((bsz, nt, MLA_HEADS * MLA_V, ts), BF16)),
        scratch_shapes=[pltpu.VMEM((SUBLANE + ts // SEQ_SPLIT, LRU_WIDTH), F32),
                        pltpu.VMEM((LRU_WIDTH // LANE, ts, LANE), F32),
                        pltpu.VMEM((LRU_WIDTH // LANE, ts, LANE), F32),
                        pltpu.VMEM((ts, LRU_WIDTH), F32),
                        pltpu.VMEM((SUBLANE, LRU_WIDTH), F32)],
        compiler_params=_params("parallel", "arbitrary"),
        name="seq0_mixer_proj",
    )(x, w_in_ev, conv_w, conv_b, gate_w, b_a, b_x, lam, q_g, kv_g, w_uq_t, w_uk, w_uv_t,
      *[t[N_META:] for t in mla_tabs], cos0_t[:, N_META:], sin0_t[:, N_META:], rec_tail, h_tail)

    tq = ATT_TILE
    assert tq == ts
    y_att = pl.pallas_call(
        _attn_kernel,
        grid=(bsz, MLA_HEADS // 2),
        in_specs=[pl.BlockSpec((1, nt, 2 * LANE, tq), lambda b, j: (b, 0, j, 0)),
                  pl.BlockSpec((1, seq, 2 * LANE), lambda b, j: (b, 0, j)),
                  pl.BlockSpec((1, nt, LANE, tq), lambda b, j: (b, 0, j, 0)),
                  pl.BlockSpec((N_META, 2 * LANE), lambda b, j: (0, j)),
                  pl.BlockSpec((LANE, N_META), lambda b, j: (j, 0))],
        out_specs=pl.BlockSpec((1, seq, LANE), lambda b, j: (b, 0, j)),
        out_shape=jax.ShapeDtypeStruct((bsz, seq, MLA_HEADS * MLA_V), BF16),
        scratch_shapes=[pltpu.VMEM((nt, 2, 1, tq), F32), pltpu.VMEM((nt, 2, MLA_V + ATT_ONES, tq), F32)],
        compiler_params=_params("parallel", "parallel"),
        name="mla_attention",
    )(qt0, k0, vt0, k_meta, vt_meta)

    m = bsz * seq
    h1 = _mix_mlp_call("layer0_out_mlp", 0, x.reshape(m, D_MODEL),
                       [y_rec.reshape(m, -1), y_att.reshape(m, -1)], w_out_ev,
                       ln[0][0], ln[0][1], w1, w2, ln[0][2], ln[0][3])

    mixw = RET_HEADS * RET_V_DIM
    col_spec = lambda width, first: pl.BlockSpec((D_MODEL, width), lambda b, h: (0, first + h))
    y_ret = pl.pallas_call(
        _ret_fused_kernel,
        grid=(bsz, RET_HEADS),
        in_specs=[pl.BlockSpec((1, seq, D_MODEL), lambda b, h: (b, 0, 0)),
                  col_spec(RET_QK_DIM, 0), col_spec(RET_QK_DIM, RET_HEADS),
                  col_spec(RET_V_DIM, RET_HEADS), col_spec(RET_V_DIM, 2 * RET_HEADS),
                  _const_spec((seq, RET_QK_DIM // 2)), _const_spec((seq, RET_QK_DIM // 2)),
                  pl.BlockSpec((1, RET_QK_DIM, RET_V_DIM), lambda b, h: (h, 0, 0)),
                  pl.BlockSpec((1, 1, LANE), lambda b, h: (h, 0, 0))],
        out_specs=pl.BlockSpec((1, seq, RET_V_DIM), lambda b, h: (b, 0, h)),
        out_shape=jax.ShapeDtypeStruct((bsz, seq, mixw), BF16),
        scratch_shapes=[pltpu.VMEM((RET_QK_DIM, RET_V_DIM), F32)],
        compiler_params=_params("parallel", "arbitrary"),
        name="layer1_retention",
    )(h1.reshape(bsz, seq, D_MODEL), w_in_od, w_in_od, w_in_od, w_in_od, cos1[N_META:], sin1[N_META:],
      s_meta, log_gamma)

    out = _mix_mlp_call("layer1_out_mlp", 1, h1, [y_ret.reshape(m, mixw)], w_out_od,
                        ln[1][0], ln[1][1], w1, w2, ln[1][2], ln[1][3])
    return out.reshape(bsz, seq, D_MODEL)
```

```python
import functools
import math

import jax
import jax.numpy as jnp
from jax import lax
from jax.experimental import pallas as pl
from jax.experimental.pallas import tpu as pltpu

D_MODEL = 1024
N_META = 16
LRU_WIDTH = 512
LRU_HEADS = 4
LRU_HEAD_DIM = 128
CONV_WIDTH = 4
LRU_C = 8.0
MLA_HEADS = 8
MLA_NOPE = 64
MLA_ROPE = 32
MLA_V = 64
MLA_Q_RANK = 256
MLA_KV_RANK = 128
RET_HEADS = 4
RET_QK_DIM = 256
RET_V_DIM = 512
D_FF = 4096
ROPE_BASE = 10000.0
DN_ALPHA = 4.0 ** 0.25
EPS = 1e-5
NEG_INF = -1e30

LANE = 128
SUBLANE = 8
VMEM_LIMIT = 56 * 1024 * 1024

BF16 = jnp.bfloat16
F32 = jnp.float32

SEQ_TILE = 512
SEQ_SPLIT = 2
SCAN_STEP = 4
SCAN_BLOCK = SUBLANE * SCAN_STEP
ATT_TILE = 512
ATT_SPLIT = 2
ATT_ONES = 16
ROW_TILE = 512
ROW_SPLIT = 2
FF_TILE = 1024
RET_CHUNK = 256


def _dot(a, b):
    return jnp.dot(a, b, preferred_element_type=F32)


def _dot_nt(a, b):
    return lax.dot_general(a, b, (((1,), (1,)), ((), ())), preferred_element_type=F32)


def _dot_tn(a, b):
    return lax.dot_general(a, b, (((0,), (0,)), ((), ())), preferred_element_type=F32)


def _layernorm(x, g, b):
    mu = jnp.mean(x, axis=-1, keepdims=True)
    xc = x - mu
    var = jnp.mean(xc * xc, axis=-1, keepdims=True)
    return xc * lax.rsqrt(var + EPS) * g + b


def _rmsnorm(x, g):
    return x * lax.rsqrt(jnp.mean(x * x, axis=-1, keepdims=True) + EPS) * g


def _rope_mla(x, c, s_up, s_dn):
    return x * c + pltpu.roll(x, MLA_ROPE // 2, 1) * s_up + pltpu.roll(x, LANE - MLA_ROPE // 2, 1) * s_dn


def _lru_gates(xc, gate_w_ref, b_a, b_x, sp_lambda):
    rs, is_ = [], []
    for h in range(LRU_HEADS):
        g = _dot(xc[:, h * LRU_HEAD_DIM:(h + 1) * LRU_HEAD_DIM].astype(BF16), gate_w_ref[h])
        rs.append(g[:, :LRU_HEAD_DIM])
        is_.append(g[:, LRU_HEAD_DIM:])
    r = jax.nn.sigmoid(jnp.concatenate(rs, axis=1) + b_a)
    i = jax.nn.sigmoid(jnp.concatenate(is_, axis=1) + b_x)
    log_a = -LRU_C * r * sp_lambda
    a = jnp.exp(log_a)
    y = 1.0 - a * a
    mult = jnp.where(y > 0.0, y * lax.rsqrt(y), 0.0)
    return a, mult * (i * xc)


def _scan8(a, b):
    row = lax.broadcasted_iota(jnp.int32, a.shape, 0)
    for k in (1, 2, 4):
        keep = row >= k
        a_prev = jnp.where(keep, pltpu.roll(a, k, 0), 1.0)
        b_prev = jnp.where(keep, pltpu.roll(b, k, 0), 0.0)
        b = a * b_prev + b
        a = a * a_prev
    return a, b


def _lru_scan(a_ref, b_ref, h0, rows):
    def body(g, h_prev):
        sl = pl.ds(pl.multiple_of(g * SUBLANE, SUBLANE), SUBLANE)
        a_c, b_c = _scan8(a_ref[sl, :], b_ref[sl, :])
        h = a_c * h_prev + b_c
        b_ref[sl, :] = h
        return h[SUBLANE - 1:SUBLANE, :]
    return lax.fori_loop(0, rows // SUBLANE, body, h0, unroll=4)


def _lru_scan_blocked(a_ref, b_ref, h0, rows):
    n_slabs = a_ref.shape[0]
    row = lax.broadcasted_iota(jnp.int32, (SUBLANE, LANE), 0)

    def body(g, carry):
        base = g * SCAN_BLOCK
        out = []
        for c in range(n_slabs):
            idx = [pl.ds(base + l, SUBLANE, stride=SCAN_STEP) for l in range(SCAN_STEP)]
            a = [a_ref[c, i, :] for i in idx]
            h = [b_ref[c, idx[0], :]]
            p = [a[0]]
            for l in range(1, SCAN_STEP):
                h.append(a[l] * h[l - 1] + b_ref[c, idx[l], :])
                p.append(a[l] * p[l - 1])
            p_seg, h_seg = _scan8(p[-1], h[-1])
            end = p_seg * carry[c] + h_seg
            start = jnp.where(row == 0, carry[c], pltpu.roll(end, 1, 0))
            for l in range(SCAN_STEP):
                b_ref[c, idx[l], :] = h[l] + p[l] * start
            out.append(end[SUBLANE - 1:SUBLANE, :])
        return tuple(out)
    return lax.fori_loop(0, rows // SCAN_BLOCK, body, tuple(h0), unroll=True)


def _mla_project(qlat, kvlat, kpe, q_g, kv_g, w_uq_ref, w_ukv_ref, rope_c, rope_up, rope_dn):
    scale = (MLA_NOPE + MLA_ROPE) ** -0.5
    q_all = _dot(_rmsnorm(qlat, q_g).astype(BF16), w_uq_ref[...])
    kv_all = _dot(_rmsnorm(kvlat, kv_g).astype(BF16), w_ukv_ref[...])
    kpe_r = _rope_mla(kpe, rope_c, rope_up, rope_dn)
    qs, ks = [], []
    for h in range(MLA_HEADS):
        sl = slice(h * LANE, (h + 1) * LANE)
        qs.append((_rope_mla(q_all[:, sl], rope_c, rope_up, rope_dn) * scale).astype(BF16))
        ks.append((kv_all[:, sl] + kpe_r).astype(BF16))
    v = kv_all[:, MLA_HEADS * LANE:].astype(BF16)
    return jnp.concatenate(qs, axis=1), jnp.concatenate(ks, axis=1), v


def _ffn(h1, w1_ref, w2_ref):
    h1b = h1.astype(BF16)
    f = None
    for c in range(D_FF // FF_TILE):
        a = _dot(h1b, w1_ref[:, c * FF_TILE:(c + 1) * FF_TILE])
        a = jnp.maximum(a, 0.0)
        part = _dot((a * a).astype(BF16), w2_ref[c * FF_TILE:(c + 1) * FF_TILE, :])
        f = part if f is None else f + part
    return f


def _mlp_block(h_in, mix, ln1_g, ln1_b, w1_ref, w2_ref, ln2_g, ln2_b):
    h1 = _layernorm(DN_ALPHA * h_in + mix, ln1_g, ln1_b)
    return _layernorm(DN_ALPHA * h1 + _ffn(h1, w1_ref, w2_ref), ln2_g, ln2_b)


def _meta_kernel(meta_ref, w_in_ref, conv_w_ref, conv_b_ref, gate_w_ref, b_a_ref, b_x_ref, lam_ref,
                 q_g_ref, kv_g_ref, w_uq_ref, w_ukv_ref, w_uvt_ref, rope_c_ref, rope_up_ref, rope_dn_ref,
                 w_out_ref, ln1_g_ref, ln1_b_ref, w1_ref, w2_ref, ln2_g_ref, ln2_b_ref,
                 w_k_ref, w_v_ref, cos1_ref, sin1_ref, lg_ref,
                 rec_tail_ref, h_tail_ref, k_meta_ref, vt_meta_ref, s_meta_ref,
                 conv_scr, a_scr, b_scr):
    n = N_META
    x = meta_ref[...]
    p = _dot(x.astype(BF16), w_in_ref[...])
    gate, rec = p[:, :LRU_WIDTH], p[:, LRU_WIDTH:2 * LRU_WIDTH]
    conv_scr[0:SUBLANE, :] = jnp.zeros((SUBLANE, LRU_WIDTH), F32)
    conv_scr[SUBLANE:SUBLANE + n, :] = rec
    cw = conv_w_ref[...]
    xc = conv_b_ref[...] + cw[3:4, :] * rec
    for j in range(CONV_WIDTH - 1):
        off = SUBLANE - (CONV_WIDTH - 1) + j
        xc = xc + cw[j:j + 1, :] * conv_scr[off:off + n, :]
    sp_lambda = jax.nn.softplus(-lam_ref[...])
    a, b = _lru_gates(xc, gate_w_ref, b_a_ref[...], b_x_ref[...], sp_lambda)
    a_scr[...] = a
    b_scr[...] = b
    _lru_scan(a_scr, b_scr, jnp.zeros((1, LRU_WIDTH), F32), n)
    h = b_scr[...]
    y_rec = (h * jax.nn.gelu(gate)).astype(BF16)
    rec_tail_ref[...] = rec[n - SUBLANE:, :]
    h_tail_ref[...] = h[n - SUBLANE:, :]

    off = 2 * LRU_WIDTH
    q, k, v = _mla_project(p[:, off:off + MLA_Q_RANK],
                           p[:, off + MLA_Q_RANK:off + MLA_Q_RANK + MLA_KV_RANK],
                           p[:, off + MLA_Q_RANK + MLA_KV_RANK:],
                           q_g_ref[...], kv_g_ref[...], w_uq_ref, w_ukv_ref,
                           rope_c_ref[...], rope_up_ref[...], rope_dn_ref[...])
    k_meta_ref[...] = k
    kvn = _rmsnorm(p[:, off + MLA_Q_RANK:off + MLA_Q_RANK + MLA_KV_RANK], kv_g_ref[...]).astype(BF16)
    vt_meta_ref[...] = _dot_nt(w_uvt_ref[...], kvn).astype(BF16)
    causal = (lax.broadcasted_iota(jnp.int32, (n, n), 1) <= lax.broadcasted_iota(jnp.int32, (n, n), 0))
    outs = []
    for hh in range(MLA_HEADS):
        sl = slice(hh * LANE, (hh + 1) * LANE)
        s = jnp.where(causal, _dot_nt(q[:, sl], k[:, sl]), NEG_INF)
        e = jnp.exp(s - jnp.max(s, axis=-1, keepdims=True))
        pr = e / jnp.sum(e, axis=-1, keepdims=True)
        outs.append(_dot(pr.astype(BF16), v[:, hh * MLA_V:(hh + 1) * MLA_V]))
    y_att = jnp.concatenate(outs, axis=1).astype(BF16)
    mix = _dot(y_rec, w_out_ref[0:LRU_WIDTH, :]) + _dot(y_att, w_out_ref[LRU_WIDTH:, :])
    h2 = _mlp_block(x, mix, ln1_g_ref[...], ln1_b_ref[...], w1_ref, w2_ref, ln2_g_ref[...], ln2_b_ref[...])

    h2b = h2.astype(BF16)
    kk = _dot(h2b, w_k_ref[...].astype(BF16))
    vv = _dot(h2b, w_v_ref[...].astype(BF16)).astype(BF16)
    cos, sin = cos1_ref[...], sin1_ref[...]
    idx = lax.broadcasted_iota(jnp.int32, (n, 1), 0).astype(F32)
    half = RET_QK_DIM // 2
    for hh in range(RET_HEADS):
        log_gamma = lg_ref[hh][:, 0:1]
        k1 = kk[:, hh * RET_QK_DIM:hh * RET_QK_DIM + half]
        k2 = kk[:, hh * RET_QK_DIM + half:(hh + 1) * RET_QK_DIM]
        kr = jnp.concatenate([k1 * cos - k2 * sin, k1 * sin + k2 * cos], axis=1).astype(BF16)
        k_dec = jnp.exp(log_gamma * (n - 1.0 - idx))
        kd = (kr.astype(F32) * k_dec).astype(BF16)
        s_meta_ref[hh] = _dot_tn(kd, vv[:, hh * RET_V_DIM:(hh + 1) * RET_V_DIM])


def _seq0_kernel(x_ref, w_in_ref, conv_w_ref, conv_b_ref, gate_w_ref, b_a_ref, b_x_ref, lam_ref,
                 q_g_ref, kv_g_ref, w_uqt_ref, w_uk_ref, w_uvt_ref, rope_c_ref, rope_up_ref, rope_dn_ref,
                 cos_t_ref, sin_t_ref, rec_tail_ref, h_tail_ref,
                 y_rec_ref, qt_ref, k_ref, vt_ref,
                 conv_scr, a_scr, b_scr, g_scr, h_scr):
    ts = SEQ_TILE
    t = pl.program_id(1)

    n_slabs = LRU_WIDTH // LANE

    @pl.when(t == 0)
    def _():
        conv_scr[0:SUBLANE, :] = rec_tail_ref[...]
        h_scr[...] = h_tail_ref[...]

    sub = ts // SEQ_SPLIT
    spans = [slice(i * sub, (i + 1) * sub) for i in range(SEQ_SPLIT)]
    ps = [_dot(x_ref[0, rows, :].astype(BF16), w_in_ref[...]) for rows in spans]
    cw = conv_w_ref[...]
    sp_lambda = jax.nn.softplus(-lam_ref[...])
    scale = (MLA_NOPE + MLA_ROPE) ** -0.5 * math.log2(math.e)
    hr = MLA_ROPE // 2
    off = 2 * LRU_WIDTH
    for rows, p in zip(spans, ps):
        gate, rec = p[:, :LRU_WIDTH], p[:, LRU_WIDTH:2 * LRU_WIDTH]
        conv_scr[SUBLANE:SUBLANE + sub, :] = rec
        xc = conv_b_ref[...] + cw[3:4, :] * rec
        for j in range(CONV_WIDTH - 1):
            o = SUBLANE - (CONV_WIDTH - 1) + j
            xc = xc + cw[j:j + 1, :] * conv_scr[o:o + sub, :]
        conv_scr[0:SUBLANE, :] = rec[sub - SUBLANE:, :]
        a, b = _lru_gates(xc, gate_w_ref, b_a_ref[...], b_x_ref[...], sp_lambda)
        for c in range(LRU_WIDTH // LANE):
            a_scr[c, rows, :] = a[:, c * LANE:(c + 1) * LANE]
            b_scr[c, rows, :] = b[:, c * LANE:(c + 1) * LANE]
        g_scr[rows, :] = jax.nn.gelu(gate)

        qn = _rmsnorm(p[:, off:off + MLA_Q_RANK], q_g_ref[...]).astype(BF16)
        q_t = _dot_nt(w_uqt_ref[...], qn)
        cos_t, sin_t = cos_t_ref[:, rows] * scale, sin_t_ref[:, rows] * scale
        for h in range(MLA_HEADS):
            base = h * LANE
            x1 = q_t[base + MLA_NOPE:base + MLA_NOPE + hr, :]
            x2 = q_t[base + MLA_NOPE + hr:base + MLA_NOPE + MLA_ROPE, :]
            qt_ref[0, 0, base:base + MLA_NOPE, rows] = (q_t[base:base + MLA_NOPE, :] * scale).astype(BF16)
            qt_ref[0, 0, base + MLA_NOPE:base + MLA_NOPE + hr, rows] = (x1 * cos_t - x2 * sin_t).astype(BF16)
            qt_ref[0, 0, base + MLA_NOPE + hr:base + MLA_NOPE + MLA_ROPE, rows] = (
                x1 * sin_t + x2 * cos_t).astype(BF16)
            qt_ref[0, 0, base + MLA_NOPE + MLA_ROPE:base + LANE, rows] = jnp.zeros(
                (LANE - MLA_NOPE - MLA_ROPE, sub), BF16)
        kvn = _rmsnorm(p[:, off + MLA_Q_RANK:off + MLA_Q_RANK + MLA_KV_RANK], kv_g_ref[...]).astype(BF16)
        k_nope = _dot(kvn, w_uk_ref[...])
        kpe_r = _rope_mla(p[:, off + MLA_Q_RANK + MLA_KV_RANK:],
                          rope_c_ref[rows, :], rope_up_ref[rows, :], rope_dn_ref[rows, :])
        for h in range(MLA_HEADS):
            sl = slice(h * LANE, (h + 1) * LANE)
            k_ref[0, rows, sl] = (k_nope[:, sl] + kpe_r).astype(BF16)
        vt_ref[0, 0, :, rows] = _dot_nt(w_uvt_ref[...], kvn).astype(BF16)

    h0 = [h_scr[SUBLANE - 1:SUBLANE, c * LANE:(c + 1) * LANE] for c in range(n_slabs)]
    h_last = _lru_scan_blocked(a_scr, b_scr, h0, ts)
    h_scr[SUBLANE - 1:SUBLANE, :] = jnp.concatenate(h_last, axis=1)
    h = jnp.concatenate([b_scr[c] for c in range(n_slabs)], axis=1)
    y_rec_ref[0] = (h * g_scr[...]).astype(BF16)


def _attn_kernel(qt_ref, k_ref, vt_ref, k_meta_ref, vt_meta_ref, o_ref, m_scr, acc_scr):
    tq = ATT_TILE
    tw = tq // ATT_SPLIT
    chains = [(hh, slice(hh * LANE, (hh + 1) * LANE), slice(hh * MLA_V, (hh + 1) * MLA_V),
               part, slice(part * tw, (part + 1) * tw))
              for hh in range(2) for part in range(ATT_SPLIT)]
    def with_ones(v_t):
        return jnp.concatenate([v_t, jnp.ones((ATT_ONES, v_t.shape[1]), BF16)], axis=0)

    def n_keys(part, diagonal):
        return (part + 1) * tw if diagonal else tq

    def meta_scores(qi):
        return [_dot(k_meta_ref[:, sl], qt_ref[0, qi, sl, cols]) for _, sl, _, _, cols in chains]

    def meta_softmax_pv(qi, scores):
        for (hh, sl, vrows, part, cols), s in zip(chains, scores):
            m = jnp.max(s, axis=0, keepdims=True)
            e = jnp.exp2(s - m)
            m_scr[qi, hh, :, cols] = m
            acc_scr[qi, hh, :, cols] = _dot(with_ones(vt_meta_ref[vrows, :]), e.astype(BF16))

    def score(qi, chain, kj, diagonal):
        _, sl, _, part, cols = chain
        return _dot(k_ref[0, kj * tq:kj * tq + n_keys(part, diagonal), sl], qt_ref[0, qi, sl, cols])

    def softmax_pv(qi, chain, kj, s, diagonal):
        hh, sl, vrows, part, cols = chain
        nk = n_keys(part, diagonal)
        if diagonal:
            key = lax.broadcasted_iota(jnp.int32, (nk, tw), 0)
            qry = lax.broadcasted_iota(jnp.int32, (nk, tw), 1) + part * tw
            s = jnp.where(key <= qry, s, NEG_INF)
        m_old = m_scr[qi, hh, :, cols]
        m_new = jnp.maximum(m_old, jnp.max(s, axis=0, keepdims=True))
        alpha = jnp.exp2(m_old - m_new)
        e = jnp.exp2(s - m_new)
        m_scr[qi, hh, :, cols] = m_new
        acc_scr[qi, hh, :, cols] = alpha * acc_scr[qi, hh, :, cols] + _dot(
            with_ones(vt_ref[0, kj, vrows, 0:nk]), e.astype(BF16))

    def run(qi):
        s_meta = meta_scores(qi)
        scores = [score(qi, c, 0, qi == 0) for c in chains]
        meta_softmax_pv(qi, s_meta)
        for kj in range(qi):
            nxt = [score(qi, c, kj + 1, kj + 1 == qi) for c in chains]
            for c, s in zip(chains, scores):
                softmax_pv(qi, c, kj, s, False)
            scores = nxt
        for c, s in zip(chains, scores):
            softmax_pv(qi, c, qi, s, True)
        out_t = jnp.concatenate([acc_scr[qi, hh, 0:MLA_V, :] / acc_scr[qi, hh, MLA_V:MLA_V + 1, :]
                                 for hh in range(2)], axis=0)
        o_ref[0, qi * tq:(qi + 1) * tq, :] = out_t.T.astype(BF16)

    for qi in range(k_ref.shape[1] // tq):
        run(qi)


def _mix_mlp_kernel(*refs, n_mix):
    h_ref = refs[0]
    y_refs = refs[1:1 + n_mix]
    w_out_ref, ln1_g, ln1_b, w1_ref, w2_ref, ln2_g, ln2_b, o_ref = refs[1 + n_mix:]
    sub = h_ref.shape[0] // ROW_SPLIT
    spans = [slice(s * sub, (s + 1) * sub) for s in range(ROW_SPLIT)]
    mixes = []
    for rows in spans:
        mix = None
        row = 0
        for y_ref in y_refs:
            width = y_ref.shape[-1]
            part = _dot(y_ref[rows, :], w_out_ref[row:row + width, :])
            mix = part if mix is None else mix + part
            row += width
        mixes.append(mix)
    h1s = [_layernorm(DN_ALPHA * h_ref[rows, :] + mix, ln1_g[...], ln1_b[...]) for rows, mix in zip(spans, mixes)]
    fs = [_ffn(h1, w1_ref, w2_ref) for h1 in h1s]
    for rows, h1, f in zip(spans, h1s, fs):
        o_ref[rows, :] = _layernorm(DN_ALPHA * h1 + f, ln2_g[...], ln2_b[...])


def _ret_fused_kernel(h_ref, wq_ref, wk_ref, wv_ref, wg_ref, cos_ref, sin_ref, s0_ref, lg_ref, y_ref, s_scr):
    c = RET_CHUNK
    half = RET_QK_DIM // 2
    k_scale = RET_QK_DIM ** -0.5
    log_gamma = lg_ref[0][:, 0:1]
    ii = lax.broadcasted_iota(jnp.int32, (c, c), 0)
    jj = lax.broadcasted_iota(jnp.int32, (c, c), 1)
    diff = (ii - jj).astype(F32)
    decay = jnp.where(diff >= 0, jnp.exp(log_gamma * jnp.maximum(diff, 0.0)), 0.0)
    idx = lax.broadcasted_iota(jnp.int32, (c, 1), 0).astype(F32)
    q_decay = jnp.exp(log_gamma * (idx + 1.0))
    k_decay = jnp.exp(log_gamma * (c - 1.0 - idx))
    chunk_decay = jnp.exp(log_gamma * c)
    s_scr[...] = s0_ref[0]

    wq, wk, wv, wg = (r[...].astype(BF16) for r in (wq_ref, wk_ref, wv_ref, wg_ref))

    def project(ci):
        rows = slice(ci * c, (ci + 1) * c)
        hb = h_ref[0, rows, :].astype(BF16)
        cos, sin = cos_ref[rows, :], sin_ref[rows, :]
        pq = _dot(hb, wq)
        q1, q2 = pq[:, 0:half], pq[:, half:]
        q = jnp.concatenate([q1 * cos - q2 * sin, q1 * sin + q2 * cos], axis=1)
        kcos, ksin = cos * k_scale, sin * k_scale
        pk = _dot(hb, wk)
        k1, k2 = pk[:, 0:half], pk[:, half:]
        k = jnp.concatenate([k1 * kcos - k2 * ksin, k1 * ksin + k2 * kcos], axis=1)
        v = _dot(hb, wv).astype(BF16)
        half_g = 0.5 * _dot(hb, wg)
        gate = (half_g + half_g * jnp.tanh(half_g)).astype(BF16)
        qb, kb = q.astype(BF16), k.astype(BF16)
        scores = (_dot_nt(qb, kb) * decay).astype(BF16)
        return dict(scores=scores, qd=(q * q_decay).astype(BF16), kd=(k * k_decay).astype(BF16), v=v, gate=gate)

    def recur(t):
        s_prev = s_scr[...]
        o = _dot(jnp.concatenate([t["scores"], t["qd"]], axis=1),
                 jnp.concatenate([t["v"], s_prev.astype(BF16)], axis=0))
        s_scr[...] = chunk_decay * s_prev + _dot_tn(t["kd"], t["v"])
        return o

    def finish(ci, o, gate):
        rows = slice(ci * c, (ci + 1) * c)
        o = o * lax.rsqrt(jnp.mean(o * o, axis=-1, keepdims=True) + EPS)
        y_ref[0, rows, :] = (gate.astype(F32) * o).astype(BF16)

    n_chunks = h_ref.shape[1] // c
    cur = project(0)
    prev = None
    for ci in range(n_chunks):
        nxt = project(ci + 1) if ci + 1 < n_chunks else None
        o = recur(cur)
        if prev is not None:
            finish(ci - 1, *prev)
        prev, cur = (o, cur["gate"]), nxt
    finish(n_chunks - 1, *prev)


def _const_spec(shape):
    zeros = (0,) * len(shape)
    return pl.BlockSpec(shape, lambda *_: zeros, pipeline_mode=pl.Buffered(1))


def _params(*semantics):
    return pltpu.CompilerParams(dimension_semantics=semantics, vmem_limit_bytes=VMEM_LIMIT)


def _rope_tables(positions, half):
    inv = ROPE_BASE ** (-jnp.arange(half, dtype=F32) / half)
    ang = positions.astype(F32)[:, None] * inv[None, :]
    return jnp.cos(ang), jnp.sin(ang)


def _mla_rope_tables(positions):
    cos, sin = _rope_tables(positions, MLA_ROPE // 2)
    n = positions.shape[0]
    h = MLA_ROPE // 2
    ones = jnp.ones((n, MLA_NOPE), F32)
    zeros = lambda w: jnp.zeros((n, w), F32)
    c = jnp.concatenate([ones, cos, cos, zeros(LANE - MLA_NOPE - MLA_ROPE)], axis=1)
    s_up = jnp.concatenate([zeros(MLA_NOPE + h), sin, zeros(LANE - MLA_NOPE - MLA_ROPE)], axis=1)
    s_dn = jnp.concatenate([zeros(MLA_NOPE), -sin, zeros(LANE - MLA_NOPE - h)], axis=1)
    return c, s_up, s_dn


def _layer_spec(stacked_shape, layer):
    return pl.BlockSpec((None,) + tuple(stacked_shape[1:]), lambda *_: (layer, 0, 0),
                        pipeline_mode=pl.Buffered(1))


def _mix_mlp_call(name, layer, h, ys, w_out, ln1_g, ln1_b, w1, w2, ln2_g, ln2_b):
    m = h.shape[0]
    tm = ROW_TILE
    row_spec = lambda w: pl.BlockSpec((tm, w), lambda i: (i, 0))
    vec = _const_spec((1, D_MODEL))
    return pl.pallas_call(
        functools.partial(_mix_mlp_kernel, n_mix=len(ys)),
        grid=(m // tm,),
        in_specs=[row_spec(D_MODEL)] + [row_spec(y.shape[1]) for y in ys] + [
            _const_spec(w_out.shape), vec, vec, _layer_spec(w1.shape, layer), _layer_spec(w2.shape, layer),
            vec, vec],
        out_specs=row_spec(D_MODEL),
        out_shape=jax.ShapeDtypeStruct((m, D_MODEL), F32),
        compiler_params=_params("parallel"),
        name=name,
    )(h, *ys, w_out, ln1_g, ln1_b, w1, w2, ln2_g, ln2_b)


def kernel(x, meta_tokens, ev_w_in, ev_conv_w, ev_conv_b, ev_w_rg_a, ev_b_rg_a, ev_w_rg_x, ev_b_rg_x,
           ev_lru_lambda, ev_q_norm_g, ev_w_uq, ev_kv_norm_g, ev_w_ukv, ev_w_out, od_w_in, od_w_out,
           ln_mix_g, ln_mix_b, mlp_w1, mlp_w2, ln_mlp_g, ln_mlp_b):
    bsz, seq, d_model = x.shape
    assert d_model == D_MODEL and meta_tokens.shape == (N_META, D_MODEL)
    assert seq % SEQ_TILE == 0 and seq % RET_CHUNK == 0 and (bsz * seq) % ROW_TILE == 0
    row = lambda v: v.reshape(1, -1).astype(F32)

    w_in0 = ev_w_in[0]
    lat0 = 2 * LRU_WIDTH
    kpe0 = lat0 + MLA_Q_RANK + MLA_KV_RANK
    w_kpe = jnp.zeros((D_MODEL, LANE), F32).at[:, MLA_NOPE:MLA_NOPE + MLA_ROPE].set(w_in0[:, kpe0:])
    w_in_ev = jnp.concatenate([w_in0[:, :kpe0], w_kpe], axis=1).astype(BF16)
    gate_w = jnp.concatenate([ev_w_rg_a[0], ev_w_rg_x[0]], axis=2).astype(BF16)
    w_uq = ev_w_uq[0].reshape(MLA_Q_RANK, MLA_HEADS, MLA_NOPE + MLA_ROPE)
    w_uq = jnp.pad(w_uq, ((0, 0), (0, 0), (0, LANE - MLA_NOPE - MLA_ROPE)))
    w_uq = w_uq.reshape(MLA_Q_RANK, MLA_HEADS * LANE).astype(BF16)
    w_ukv = ev_w_ukv[0].reshape(MLA_KV_RANK, MLA_HEADS, MLA_NOPE + MLA_V)
    w_uk = jnp.pad(w_ukv[:, :, :MLA_NOPE], ((0, 0), (0, 0), (0, LANE - MLA_NOPE)))
    w_uk = w_uk.reshape(MLA_KV_RANK, MLA_HEADS * LANE)
    w_uv = w_ukv[:, :, MLA_NOPE:].reshape(MLA_KV_RANK, MLA_HEADS * MLA_V)
    w_ukv_p = jnp.concatenate([w_uk, w_uv], axis=1).astype(BF16)
    w_uq_t = w_uq.T
    w_uk = w_uk.astype(BF16)
    w_uv_t = w_uv.T.astype(BF16)
    w_out_ev = ev_w_out[0].astype(BF16)
    w_in_od = od_w_in[0]
    w_out_od = od_w_out[0].astype(BF16)
    w1 = mlp_w1.astype(BF16)
    w2 = mlp_w2.astype(BF16)
    conv_w = ev_conv_w[0].astype(F32)
    conv_b, b_a, b_x, lam = row(ev_conv_b[0]), row(ev_b_rg_a[0]), row(ev_b_rg_x[0]), row(ev_lru_lambda[0])
    q_g, kv_g = row(ev_q_norm_g[0]), row(ev_kv_norm_g[0])
    ln = [(row(ln_mix_g[l]), row(ln_mix_b[l]), row(ln_mlp_g[l]), row(ln_mlp_b[l])) for l in range(2)]

    pos = jnp.arange(N_META + seq, dtype=jnp.int32)
    mla_tabs = _mla_rope_tables(pos)
    cos0, sin0 = _rope_tables(pos, MLA_ROPE // 2)
    cos0_t, sin0_t = cos0.T, sin0.T
    cos1, sin1 = _rope_tables(pos, RET_QK_DIM // 2)
    k_scale = RET_QK_DIM ** -0.5
    qk = RET_HEADS * RET_QK_DIM
    log_gamma = jnp.log(1.0 - 2.0 ** (-5.0 - jnp.arange(RET_HEADS, dtype=F32)))
    log_gamma = jnp.broadcast_to(log_gamma[:, None, None], (RET_HEADS, 1, LANE))

    mixw = RET_HEADS * RET_V_DIM
    meta_out_shapes = (jax.ShapeDtypeStruct((SUBLANE, LRU_WIDTH), F32),
                       jax.ShapeDtypeStruct((SUBLANE, LRU_WIDTH), F32),
                       jax.ShapeDtypeStruct((N_META, MLA_HEADS * LANE), BF16),
                       jax.ShapeDtypeStruct((MLA_HEADS * MLA_V, N_META), BF16),
                       jax.ShapeDtypeStruct((RET_HEADS, RET_QK_DIM, RET_V_DIM), F32))
    meta_args = [meta_tokens.astype(F32), w_in_ev, conv_w, conv_b, gate_w, b_a, b_x, lam, q_g, kv_g, w_uq,
                 w_ukv_p, w_uv_t, *[t[:N_META] for t in mla_tabs], w_out_ev, *ln[0][:2]]
    meta_specs = [_const_spec(a.shape) for a in meta_args]
    meta_args += [w1, w2, *ln[0][2:], w_in_od, w_in_od]
    meta_specs += [_layer_spec(w1.shape, 0), _layer_spec(w2.shape, 0), _const_spec(ln[0][2].shape),
                   _const_spec(ln[0][3].shape),
                   pl.BlockSpec((D_MODEL, qk), lambda i: (0, 1), pipeline_mode=pl.Buffered(1)),
                   pl.BlockSpec((D_MODEL, mixw), lambda i: (0, 1), pipeline_mode=pl.Buffered(1))]
    tail_args = [cos1[:N_META] * k_scale, sin1[:N_META] * k_scale, log_gamma]
    meta_args += tail_args
    meta_specs += [_const_spec(a.shape) for a in tail_args]
    meta_out = pl.pallas_call(
        _meta_kernel,
        grid=(1,),
        in_specs=meta_specs,
        out_specs=[pl.BlockSpec(s.shape, lambda i, nd=len(s.shape): (0,) * nd) for s in meta_out_shapes],
        out_shape=meta_out_shapes,
        scratch_shapes=[pltpu.VMEM((SUBLANE + N_META, LRU_WIDTH), F32),
                        pltpu.VMEM((N_META, LRU_WIDTH), F32),
                        pltpu.VMEM((N_META, LRU_WIDTH), F32)],
        compiler_params=_params("arbitrary"),
        name="meta_tokens",
    )(*meta_args)
    rec_tail, h_tail, k_meta, vt_meta, s_meta = meta_out

    ts = SEQ_TILE
    nt = seq // ts
    tab_spec = pl.BlockSpec((ts, LANE), lambda b, t: (t, 0))
    tab_t_spec = pl.BlockSpec((MLA_ROPE // 2, ts), lambda b, t: (0, t))
    seq_spec = lambda w: pl.BlockSpec((1, ts, w), lambda b, t: (b, t, 0))
    seq_t_spec = lambda w: pl.BlockSpec((1, 1, w, ts), lambda b, t: (b, t, 0, 0))
    y_rec, qt0, k0, vt0 = pl.pallas_call(
        _seq0_kernel,
        grid=(bsz, nt),
        in_specs=[seq_spec(D_MODEL), _const_spec(w_in_ev.shape), _const_spec(conv_w.shape),
                  _const_spec(conv_b.shape), _const_spec(gate_w.shape), _const_spec(b_a.shape),
                  _const_spec(b_x.shape), _const_spec(lam.shape), _const_spec(q_g.shape),
                  _const_spec(kv_g.shape), _const_spec(w_uq_t.shape), _const_spec(w_uk.shape),
                  _const_spec(w_uv_t.shape), tab_spec, tab_spec, tab_spec, tab_t_spec, tab_t_spec,
                  _const_spec(rec_tail.shape), _const_spec(h_tail.shape)],
        out_specs=[seq_spec(LRU_WIDTH), seq_t_spec(MLA_HEADS * LANE), seq_spec(MLA_HEADS * LANE),
                   seq_t_spec(MLA_HEADS * MLA_V)],
        out_shape=(jax.ShapeDtypeStruct((bsz, seq, LRU_WIDTH), BF16),
                   jax.ShapeDtypeStruct((bsz, nt, MLA_HEADS * LANE, ts), BF16),
                   jax.ShapeDtypeStruct((bsz, seq, MLA_HEADS * LANE), BF16),
                   jax.ShapeDtypeStruct((bsz, nt, MLA_HEADS * MLA_V, ts), BF16)),
        scratch_shapes=[pltpu.VMEM((SUBLANE + ts // SEQ_SPLIT, LRU_WIDTH), F32),
                        pltpu.VMEM((LRU_WIDTH // LANE, ts, LANE), F32),
                        pltpu.VMEM((LRU_WIDTH // LANE, ts, LANE), F32),
                        pltpu.VMEM((ts, LRU_WIDTH), F32),
                        pltpu.VMEM((SUBLANE, LRU_WIDTH), F32)],
        compiler_params=_params("parallel", "arbitrary"),
        name="seq0_mixer_proj",
    )(x, w_in_ev, conv_w, conv_b, gate_w, b_a, b_x, lam, q_g, kv_g, w_uq_t, w_uk, w_uv_t,
      *[t[N_META:] for t in mla_tabs], cos0_t[:, N_META:], sin0_t[:, N_META:], rec_tail, h_tail)

    tq = ATT_TILE
    assert tq == ts
    y_att = pl.pallas_call(
        _attn_kernel,
        grid=(bsz, MLA_HEADS // 2),
        in_specs=[pl.BlockSpec((1, nt, 2 * LANE, tq), lambda b, j: (b, 0, j, 0)),
                  pl.BlockSpec((1, seq, 2 * LANE), lambda b, j: (b, 0, j)),
                  pl.BlockSpec((1, nt, LANE, tq), lambda b, j: (b, 0, j, 0)),
                  pl.BlockSpec((N_META, 2 * LANE), lambda b, j: (0, j)),
                  pl.BlockSpec((LANE, N_META), lambda b, j: (j, 0))],
        out_specs=pl.BlockSpec((1, seq, LANE), lambda b, j: (b, 0, j)),
        out_shape=jax.ShapeDtypeStruct((bsz, seq, MLA_HEADS * MLA_V), BF16),
        scratch_shapes=[pltpu.VMEM((nt, 2, 1, tq), F32), pltpu.VMEM((nt, 2, MLA_V + ATT_ONES, tq), F32)],
        compiler_params=_params("parallel", "parallel"),
        name="mla_attention",
    )(qt0, k0, vt0, k_meta, vt_meta)

    m = bsz * seq
    h1 = _mix_mlp_call("layer0_out_mlp", 0, x.reshape(m, D_MODEL),
                       [y_rec.reshape(m, -1), y_att.reshape(m, -1)], w_out_ev,
                       ln[0][0], ln[0][1], w1, w2, ln[0][2], ln[0][3])

    mixw = RET_HEADS * RET_V_DIM
    col_spec = lambda width, first: pl.BlockSpec((D_MODEL, width), lambda b, h: (0, first + h))
    y_ret = pl.pallas_call(
        _ret_fused_kernel,
        grid=(bsz, RET_HEADS),
        in_specs=[pl.BlockSpec((1, seq, D_MODEL), lambda b, h: (b, 0, 0)),
                  col_spec(RET_QK_DIM, 0), col_spec(RET_QK_DIM, RET_HEADS),
                  col_spec(RET_V_DIM, RET_HEADS), col_spec(RET_V_DIM, 2 * RET_HEADS),
                  _const_spec((seq, RET_QK_DIM // 2)), _const_spec((seq, RET_QK_DIM // 2)),
                  pl.BlockSpec((1, RET_QK_DIM, RET_V_DIM), lambda b, h: (h, 0, 0)),
                  pl.BlockSpec((1, 1, LANE), lambda b, h: (h, 0, 0))],
        out_specs=pl.BlockSpec((1, seq, RET_V_DIM), lambda b, h: (b, 0, h)),
        out_shape=jax.ShapeDtypeStruct((bsz, seq, mixw), BF16),
        scratch_shapes=[pltpu.VMEM((RET_QK_DIM, RET_V_DIM), F32)],
        compiler_params=_params("parallel", "arbitrary"),
        name="layer1_retention",
    )(h1.reshape(bsz, seq, D_MODEL), w_in_od, w_in_od, w_in_od, w_in_od, cos1[N_META:], sin1[N_META:],
      s_meta, log_gamma)

    out = _mix_mlp_call("layer1_out_mlp", 1, h1, [y_ret.reshape(m, mixw)], w_out_od,
                        ln[1][0], ln[1][1], w1, w2, ln[1][2], ln[1][3])
    return out.reshape(bsz, seq, D_MODEL)
```

```python
import functools
import math

import jax
import jax.numpy as jnp
from jax import lax
from jax.experimental import pallas as pl
from jax.experimental.pallas import tpu as pltpu

D_MODEL = 1024
N_META = 16
LRU_WIDTH = 512
LRU_HEADS = 4
LRU_HEAD_DIM = 128
CONV_WIDTH = 4
LRU_C = 8.0
MLA_HEADS = 8
MLA_NOPE = 64
MLA_ROPE = 32
MLA_V = 64
MLA_Q_RANK = 256
MLA_KV_RANK = 128
RET_HEADS = 4
RET_QK_DIM = 256
RET_V_DIM = 512
D_FF = 4096
ROPE_BASE = 10000.0
DN_ALPHA = 4.0 ** 0.25
EPS = 1e-5
NEG_INF = -1e30

LANE = 128
SUBLANE = 8
VMEM_LIMIT = 56 * 1024 * 1024

BF16 = jnp.bfloat16
F32 = jnp.float32

SEQ_TILE = 512
SEQ_SPLIT = 2
SCAN_STEP = 4
SCAN_BLOCK = SUBLANE * SCAN_STEP
ATT_TILE = 512
ATT_SPLIT = 2
ATT_ONES = 16
ROW_TILE = 512
ROW_SPLIT = 2
FF_TILE = 1024
RET_CHUNK = 256
RET_AHEAD = 2


def _dot(a, b):
    return jnp.dot(a, b, preferred_element_type=F32)


def _dot_nt(a, b):
    return lax.dot_general(a, b, (((1,), (1,)), ((), ())), preferred_element_type=F32)


def _dot_tn(a, b):
    return lax.dot_general(a, b, (((0,), (0,)), ((), ())), preferred_element_type=F32)


def _layernorm(x, g, b):
    mu = jnp.mean(x, axis=-1, keepdims=True)
    xc = x - mu
    var = jnp.mean(xc * xc, axis=-1, keepdims=True)
    return xc * lax.rsqrt(var + EPS) * g + b


def _rmsnorm(x, g):
    return x * lax.rsqrt(jnp.mean(x * x, axis=-1, keepdims=True) + EPS) * g


def _rope_mla(x, c, s_up, s_dn):
    return x * c + pltpu.roll(x, MLA_ROPE // 2, 1) * s_up + pltpu.roll(x, LANE - MLA_ROPE // 2, 1) * s_dn


def _lru_gates(xc, gate_w_ref, b_a, b_x, sp_lambda):
    rs, is_ = [], []
    for h in range(LRU_HEADS):
        g = _dot(xc[:, h * LRU_HEAD_DIM:(h + 1) * LRU_HEAD_DIM].astype(BF16), gate_w_ref[h])
        rs.append(g[:, :LRU_HEAD_DIM])
        is_.append(g[:, LRU_HEAD_DIM:])
    r = jax.nn.sigmoid(jnp.concatenate(rs, axis=1) + b_a)
    i = jax.nn.sigmoid(jnp.concatenate(is_, axis=1) + b_x)
    log_a = -LRU_C * r * sp_lambda
    a = jnp.exp(log_a)
    y = 1.0 - a * a
    mult = jnp.where(y > 0.0, y * lax.rsqrt(y), 0.0)
    return a, mult * (i * xc)


def _scan8(a, b):
    row = lax.broadcasted_iota(jnp.int32, a.shape, 0)
    for k in (1, 2, 4):
        keep = row >= k
        a_prev = jnp.where(keep, pltpu.roll(a, k, 0), 1.0)
        b_prev = jnp.where(keep, pltpu.roll(b, k, 0), 0.0)
        b = a * b_prev + b
        a = a * a_prev
    return a, b


def _lru_scan(a_ref, b_ref, h0, rows):
    def body(g, h_prev):
        sl = pl.ds(pl.multiple_of(g * SUBLANE, SUBLANE), SUBLANE)
        a_c, b_c = _scan8(a_ref[sl, :], b_ref[sl, :])
        h = a_c * h_prev + b_c
        b_ref[sl, :] = h
        return h[SUBLANE - 1:SUBLANE, :]
    return lax.fori_loop(0, rows // SUBLANE, body, h0, unroll=4)


def _lru_scan_blocked(a_ref, b_ref, h0, rows):
    n_slabs = a_ref.shape[0]
    row = lax.broadcasted_iota(jnp.int32, (SUBLANE, LANE), 0)

    def body(g, carry):
        base = g * SCAN_BLOCK
        out = []
        for c in range(n_slabs):
            idx = [pl.ds(base + l, SUBLANE, stride=SCAN_STEP) for l in range(SCAN_STEP)]
            a = [a_ref[c, i, :] for i in idx]
            h = [b_ref[c, idx[0], :]]
            p = [a[0]]
            for l in range(1, SCAN_STEP):
                h.append(a[l] * h[l - 1] + b_ref[c, idx[l], :])
                p.append(a[l] * p[l - 1])
            p_seg, h_seg = _scan8(p[-1], h[-1])
            end = p_seg * carry[c] + h_seg
            start = jnp.where(row == 0, carry[c], pltpu.roll(end, 1, 0))
            for l in range(SCAN_STEP):
                b_ref[c, idx[l], :] = h[l] + p[l] * start
            out.append(end[SUBLANE - 1:SUBLANE, :])
        return tuple(out)
    return lax.fori_loop(0, rows // SCAN_BLOCK, body, tuple(h0), unroll=2)


def _mla_project(qlat, kvlat, kpe, q_g, kv_g, w_uq_ref, w_ukv_ref, rope_c, rope_up, rope_dn):
    scale = (MLA_NOPE + MLA_ROPE) ** -0.5
    q_all = _dot(_rmsnorm(qlat, q_g).astype(BF16), w_uq_ref[...])
    kv_all = _dot(_rmsnorm(kvlat, kv_g).astype(BF16), w_ukv_ref[...])
    kpe_r = _rope_mla(kpe, rope_c, rope_up, rope_dn)
    qs, ks = [], []
    for h in range(MLA_HEADS):
        sl = slice(h * LANE, (h + 1) * LANE)
        qs.append((_rope_mla(q_all[:, sl], rope_c, rope_up, rope_dn) * scale).astype(BF16))
        ks.append((kv_all[:, sl] + kpe_r).astype(BF16))
    v = kv_all[:, MLA_HEADS * LANE:].astype(BF16)
    return jnp.concatenate(qs, axis=1), jnp.concatenate(ks, axis=1), v


def _ffn(h1, w1_ref, w2_ref):
    h1b = h1.astype(BF16)
    f = None
    for c in range(D_FF // FF_TILE):
        a = _dot(h1b, w1_ref[:, c * FF_TILE:(c + 1) * FF_TILE])
        a = jnp.maximum(a, 0.0)
        part = _dot((a * a).astype(BF16), w2_ref[c * FF_TILE:(c + 1) * FF_TILE, :])
        f = part if f is None else f + part
    return f


def _mlp_block(h_in, mix, ln1_g, ln1_b, w1_ref, w2_ref, ln2_g, ln2_b):
    h1 = _layernorm(DN_ALPHA * h_in + mix, ln1_g, ln1_b)
    return _layernorm(DN_ALPHA * h1 + _ffn(h1, w1_ref, w2_ref), ln2_g, ln2_b)


def _meta_kernel(meta_ref, w_in_ref, conv_w_ref, conv_b_ref, gate_w_ref, b_a_ref, b_x_ref, lam_ref,
                 q_g_ref, kv_g_ref, w_uq_ref, w_ukv_ref, w_uvt_ref, rope_c_ref, rope_up_ref, rope_dn_ref,
                 w_out_ref, ln1_g_ref, ln1_b_ref, w1_ref, w2_ref, ln2_g_ref, ln2_b_ref,
                 w_k_ref, w_v_ref, cos1_ref, sin1_ref, lg_ref,
                 rec_tail_ref, h_tail_ref, k_meta_ref, vt_meta_ref, s_meta_ref,
                 conv_scr, a_scr, b_scr):
    n = N_META
    x = meta_ref[...]
    p = _dot(x.astype(BF16), w_in_ref[...])
    gate, rec = p[:, :LRU_WIDTH], p[:, LRU_WIDTH:2 * LRU_WIDTH]
    conv_scr[0:SUBLANE, :] = jnp.zeros((SUBLANE, LRU_WIDTH), F32)
    conv_scr[SUBLANE:SUBLANE + n, :] = rec
    cw = conv_w_ref[...]
    xc = conv_b_ref[...] + cw[3:4, :] * rec
    for j in range(CONV_WIDTH - 1):
        off = SUBLANE - (CONV_WIDTH - 1) + j
        xc = xc + cw[j:j + 1, :] * conv_scr[off:off + n, :]
    sp_lambda = jax.nn.softplus(-lam_ref[...])
    a, b = _lru_gates(xc, gate_w_ref, b_a_ref[...], b_x_ref[...], sp_lambda)
    a_scr[...] = a
    b_scr[...] = b
    _lru_scan(a_scr, b_scr, jnp.zeros((1, LRU_WIDTH), F32), n)
    h = b_scr[...]
    y_rec = (h * jax.nn.gelu(gate)).astype(BF16)
    rec_tail_ref[...] = rec[n - SUBLANE:, :]
    h_tail_ref[...] = h[n - SUBLANE:, :]

    off = 2 * LRU_WIDTH
    q, k, v = _mla_project(p[:, off:off + MLA_Q_RANK],
                           p[:, off + MLA_Q_RANK:off + MLA_Q_RANK + MLA_KV_RANK],
                           p[:, off + MLA_Q_RANK + MLA_KV_RANK:],
                           q_g_ref[...], kv_g_ref[...], w_uq_ref, w_ukv_ref,
                           rope_c_ref[...], rope_up_ref[...], rope_dn_ref[...])
    k_meta_ref[...] = k
    kvn = _rmsnorm(p[:, off + MLA_Q_RANK:off + MLA_Q_RANK + MLA_KV_RANK], kv_g_ref[...]).astype(BF16)
    vt_meta_ref[...] = _dot_nt(w_uvt_ref[...], kvn).astype(BF16)
    causal = (lax.broadcasted_iota(jnp.int32, (n, n), 1) <= lax.broadcasted_iota(jnp.int32, (n, n), 0))
    outs = []
    for hh in range(MLA_HEADS):
        sl = slice(hh * LANE, (hh + 1) * LANE)
        s = jnp.where(causal, _dot_nt(q[:, sl], k[:, sl]), NEG_INF)
        e = jnp.exp(s - jnp.max(s, axis=-1, keepdims=True))
        pr = e / jnp.sum(e, axis=-1, keepdims=True)
        outs.append(_dot(pr.astype(BF16), v[:, hh * MLA_V:(hh + 1) * MLA_V]))
    y_att = jnp.concatenate(outs, axis=1).astype(BF16)
    mix = _dot(y_rec, w_out_ref[0:LRU_WIDTH, :]) + _dot(y_att, w_out_ref[LRU_WIDTH:, :])
    h2 = _mlp_block(x, mix, ln1_g_ref[...], ln1_b_ref[...], w1_ref, w2_ref, ln2_g_ref[...], ln2_b_ref[...])

    h2b = h2.astype(BF16)
    kk = _dot(h2b, w_k_ref[...].astype(BF16))
    vv = _dot(h2b, w_v_ref[...].astype(BF16)).astype(BF16)
    cos, sin = cos1_ref[...], sin1_ref[...]
    idx = lax.broadcasted_iota(jnp.int32, (n, 1), 0).astype(F32)
    half = RET_QK_DIM // 2
    for hh in range(RET_HEADS):
        log_gamma = lg_ref[hh][:, 0:1]
        k1 = kk[:, hh * RET_QK_DIM:hh * RET_QK_DIM + half]
        k2 = kk[:, hh * RET_QK_DIM + half:(hh + 1) * RET_QK_DIM]
        kr = jnp.concatenate([k1 * cos - k2 * sin, k1 * sin + k2 * cos], axis=1).astype(BF16)
        k_dec = jnp.exp(log_gamma * (n - 1.0 - idx))
        kd = (kr.astype(F32) * k_dec).astype(BF16)
        s_meta_ref[hh] = _dot_tn(kd, vv[:, hh * RET_V_DIM:(hh + 1) * RET_V_DIM])


def _seq0_kernel(x_ref, w_in_ref, conv_w_ref, conv_b_ref, gate_w_ref, b_a_ref, b_x_ref, lam_ref,
                 q_g_ref, kv_g_ref, w_uqt_ref, w_uk_ref, w_uvt_ref, rope_c_ref, rope_up_ref, rope_dn_ref,
                 cos_t_ref, sin_t_ref, rec_tail_ref, h_tail_ref,
                 y_rec_ref, qt_ref, k_ref, vt_ref,
                 conv_scr, a_scr, b_scr, g_scr, h_scr):
    ts = SEQ_TILE
    t = pl.program_id(1)

    n_slabs = LRU_WIDTH // LANE

    @pl.when(t == 0)
    def _():
        conv_scr[0:SUBLANE, :] = rec_tail_ref[...]
        h_scr[...] = h_tail_ref[...]

    sub = ts // SEQ_SPLIT
    spans = [slice(i * sub, (i + 1) * sub) for i in range(SEQ_SPLIT)]
    ps = [_dot(x_ref[0, rows, :].astype(BF16), w_in_ref[...]) for rows in spans]
    cw = conv_w_ref[...]
    sp_lambda = jax.nn.softplus(-lam_ref[...])
    scale = (MLA_NOPE + MLA_ROPE) ** -0.5 * math.log2(math.e)
    hr = MLA_ROPE // 2
    off = 2 * LRU_WIDTH
    for rows, p in zip(spans, ps):
        gate, rec = p[:, :LRU_WIDTH], p[:, LRU_WIDTH:2 * LRU_WIDTH]
        conv_scr[SUBLANE:SUBLANE + sub, :] = rec
        xc = conv_b_ref[...] + cw[3:4, :] * rec
        for j in range(CONV_WIDTH - 1):
            o = SUBLANE - (CONV_WIDTH - 1) + j
            xc = xc + cw[j:j + 1, :] * conv_scr[o:o + sub, :]
        conv_scr[0:SUBLANE, :] = rec[sub - SUBLANE:, :]
        a, b = _lru_gates(xc, gate_w_ref, b_a_ref[...], b_x_ref[...], sp_lambda)
        for c in range(LRU_WIDTH // LANE):
            a_scr[c, rows, :] = a[:, c * LANE:(c + 1) * LANE]
            b_scr[c, rows, :] = b[:, c * LANE:(c + 1) * LANE]
        g_scr[rows, :] = jax.nn.gelu(gate)

        qn = _rmsnorm(p[:, off:off + MLA_Q_RANK], q_g_ref[...]).astype(BF16)
        q_t = _dot_nt(w_uqt_ref[...], qn)
        cos_t, sin_t = cos_t_ref[:, rows] * scale, sin_t_ref[:, rows] * scale
        for h in range(MLA_HEADS):
            base = h * LANE
            x1 = q_t[base + MLA_NOPE:base + MLA_NOPE + hr, :]
            x2 = q_t[base + MLA_NOPE + hr:base + MLA_NOPE + MLA_ROPE, :]
            qt_ref[0, 0, base:base + MLA_NOPE, rows] = (q_t[base:base + MLA_NOPE, :] * scale).astype(BF16)
            qt_ref[0, 0, base + MLA_NOPE:base + MLA_NOPE + hr, rows] = (x1 * cos_t - x2 * sin_t).astype(BF16)
            qt_ref[0, 0, base + MLA_NOPE + hr:base + MLA_NOPE + MLA_ROPE, rows] = (
                x1 * sin_t + x2 * cos_t).astype(BF16)
            qt_ref[0, 0, base + MLA_NOPE + MLA_ROPE:base + LANE, rows] = jnp.zeros(
                (LANE - MLA_NOPE - MLA_ROPE, sub), BF16)
        kvn = _rmsnorm(p[:, off + MLA_Q_RANK:off + MLA_Q_RANK + MLA_KV_RANK], kv_g_ref[...]).astype(BF16)
        k_nope = _dot(kvn, w_uk_ref[...])
        kpe_r = _rope_mla(p[:, off + MLA_Q_RANK + MLA_KV_RANK:],
                          rope_c_ref[rows, :], rope_up_ref[rows, :], rope_dn_ref[rows, :])
        for h in range(MLA_HEADS):
            sl = slice(h * LANE, (h + 1) * LANE)
            k_ref[0, rows, sl] = (k_nope[:, sl] + kpe_r).astype(BF16)
        vt_ref[0, 0, :, rows] = _dot_nt(w_uvt_ref[...], kvn).astype(BF16)

    h0 = [h_scr[SUBLANE - 1:SUBLANE, c * LANE:(c + 1) * LANE] for c in range(n_slabs)]
    h_last = _lru_scan_blocked(a_scr, b_scr, h0, ts)
    h_scr[SUBLANE - 1:SUBLANE, :] = jnp.concatenate(h_last, axis=1)
    h = jnp.concatenate([b_scr[c] for c in range(n_slabs)], axis=1)
    y_rec_ref[0] = (h * g_scr[...]).astype(BF16)


def _attn_kernel(qt_ref, k_ref, vt_ref, k_meta_ref, vt_meta_ref, o_ref, m_scr, acc_scr):
    tq = ATT_TILE
    tw = tq // ATT_SPLIT
    chains = [(hh, slice(hh * LANE, (hh + 1) * LANE), slice(hh * MLA_V, (hh + 1) * MLA_V),
               part, slice(part * tw, (part + 1) * tw))
              for hh in range(2) for part in range(ATT_SPLIT)]
    def with_ones(v_t):
        return jnp.concatenate([v_t, jnp.ones((ATT_ONES, v_t.shape[1]), BF16)], axis=0)

    def n_keys(part, diagonal):
        return (part + 1) * tw if diagonal else tq

    def meta_scores(qi):
        return [_dot(k_meta_ref[:, sl], qt_ref[0, qi, sl, cols]) for _, sl, _, _, cols in chains]

    def meta_softmax_pv(qi, scores):
        for (hh, sl, vrows, part, cols), s in zip(chains, scores):
            m = jnp.max(s, axis=0, keepdims=True)
            e = jnp.exp2(s - m)
            m_scr[qi, hh, :, cols] = m
            acc_scr[qi, hh, :, cols] = _dot(with_ones(vt_meta_ref[vrows, :]), e.astype(BF16))

    def score(qi, chain, kj, diagonal):
        _, sl, _, part, cols = chain
        return _dot(k_ref[0, kj * tq:kj * tq + n_keys(part, diagonal), sl], qt_ref[0, qi, sl, cols])

    def softmax_pv(qi, chain, kj, s, diagonal):
        hh, sl, vrows, part, cols = chain
        nk = n_keys(part, diagonal)
        if diagonal:
            key = lax.broadcasted_iota(jnp.int32, (nk, tw), 0)
            qry = lax.broadcasted_iota(jnp.int32, (nk, tw), 1) + part * tw
            s = jnp.where(key <= qry, s, NEG_INF)
        m_old = m_scr[qi, hh, :, cols]
        m_new = jnp.maximum(m_old, jnp.max(s, axis=0, keepdims=True))
        alpha = jnp.exp2(m_old - m_new)
        e = jnp.exp2(s - m_new)
        m_scr[qi, hh, :, cols] = m_new
        acc_scr[qi, hh, :, cols] = alpha * acc_scr[qi, hh, :, cols] + _dot(
            with_ones(vt_ref[0, kj, vrows, 0:nk]), e.astype(BF16))

    def run(qi):
        s_meta = meta_scores(qi)
        scores = [score(qi, c, 0, qi == 0) for c in chains]
        meta_softmax_pv(qi, s_meta)
        for kj in range(qi):
            nxt = [score(qi, c, kj + 1, kj + 1 == qi) for c in chains]
            for c, s in zip(chains, scores):
                softmax_pv(qi, c, kj, s, False)
            scores = nxt
        for c, s in zip(chains, scores):
            softmax_pv(qi, c, qi, s, True)
        out_t = jnp.concatenate([acc_scr[qi, hh, 0:MLA_V, :] / acc_scr[qi, hh, MLA_V:MLA_V + 1, :]
                                 for hh in range(2)], axis=0)
        o_ref[0, qi * tq:(qi + 1) * tq, :] = out_t.T.astype(BF16)

    for qi in range(k_ref.shape[1] // tq):
        run(qi)


def _mix_mlp_kernel(*refs, n_mix):
    h_ref = refs[0]
    y_refs = refs[1:1 + n_mix]
    w_out_ref, ln1_g, ln1_b, w1_ref, w2_ref, ln2_g, ln2_b, o_ref = refs[1 + n_mix:]
    sub = h_ref.shape[0] // ROW_SPLIT
    spans = [slice(s * sub, (s + 1) * sub) for s in range(ROW_SPLIT)]
    mixes = []
    for rows in spans:
        mix = None
        row = 0
        for y_ref in y_refs:
            width = y_ref.shape[-1]
            part = _dot(y_ref[rows, :], w_out_ref[row:row + width, :])
            mix = part if mix is None else mix + part
            row += width
        mixes.append(mix)
    h1s = [_layernorm(DN_ALPHA * h_ref[rows, :] + mix, ln1_g[...], ln1_b[...]) for rows, mix in zip(spans, mixes)]
    fs = [_ffn(h1, w1_ref, w2_ref) for h1 in h1s]
    for rows, h1, f in zip(spans, h1s, fs):
        o_ref[rows, :] = _layernorm(DN_ALPHA * h1 + f, ln2_g[...], ln2_b[...])


def _ret_fused_kernel(h_ref, wq_ref, wk_ref, wv_ref, wg_ref, cos_ref, sin_ref, s0_ref, lg_ref, y_ref, s_scr):
    c = RET_CHUNK
    half = RET_QK_DIM // 2
    k_scale = RET_QK_DIM ** -0.5
    log_gamma = lg_ref[0][:, 0:1]
    ii = lax.broadcasted_iota(jnp.int32, (c, c), 0)
    jj = lax.broadcasted_iota(jnp.int32, (c, c), 1)
    diff = (ii - jj).astype(F32)
    decay = jnp.where(diff >= 0, jnp.exp(log_gamma * jnp.maximum(diff, 0.0)), 0.0)
    idx = lax.broadcasted_iota(jnp.int32, (c, 1), 0).astype(F32)
    q_decay = jnp.exp(log_gamma * (idx + 1.0))
    k_decay = jnp.exp(log_gamma * (c - 1.0 - idx))
    chunk_decay = jnp.exp(log_gamma * c)
    s_scr[...] = s0_ref[0]

    wq, wk, wv, wg = (r[...].astype(BF16) for r in (wq_ref, wk_ref, wv_ref, wg_ref))

    def project(ci):
        rows = slice(ci * c, (ci + 1) * c)
        hb = h_ref[0, rows, :].astype(BF16)
        cos, sin = cos_ref[rows, :], sin_ref[rows, :]
        pq = _dot(hb, wq)
        q1, q2 = pq[:, 0:half], pq[:, half:]
        q = jnp.concatenate([q1 * cos - q2 * sin, q1 * sin + q2 * cos], axis=1)
        kcos, ksin = cos * k_scale, sin * k_scale
        pk = _dot(hb, wk)
        k1, k2 = pk[:, 0:half], pk[:, half:]
        k = jnp.concatenate([k1 * kcos - k2 * ksin, k1 * ksin + k2 * kcos], axis=1)
        v = _dot(hb, wv).astype(BF16)
        half_g = 0.5 * _dot(hb, wg)
        gate = (half_g + half_g * jnp.tanh(half_g)).astype(BF16)
        qb, kb = q.astype(BF16), k.astype(BF16)
        scores = (_dot_nt(qb, kb) * decay).astype(BF16)
        return dict(scores=scores, qd=(q * q_decay).astype(BF16), kd=(k * k_decay).astype(BF16), v=v, gate=gate)

    def recur(t):
        s_prev = s_scr[...]
        o = _dot(jnp.concatenate([t["scores"], t["qd"]], axis=1),
                 jnp.concatenate([t["v"], s_prev.astype(BF16)], axis=0))
        s_scr[...] = chunk_decay * s_prev + _dot_tn(t["kd"], t["v"])
        return o

    def finish(ci, o, gate):
        rows = slice(ci * c, (ci + 1) * c)
        o = o * lax.rsqrt(jnp.mean(o * o, axis=-1, keepdims=True) + EPS)
        y_ref[0, rows, :] = (gate.astype(F32) * o).astype(BF16)

    n_chunks = h_ref.shape[1] // c
    ahead = RET_AHEAD
    pending = [project(ci) for ci in range(min(ahead, n_chunks))]
    prev = None
    for ci in range(n_chunks):
        if ci + ahead < n_chunks:
            pending.append(project(ci + ahead))
        cur = pending.pop(0)
        o = recur(cur)
        if prev is not None:
            finish(ci - 1, *prev)
        prev = (o, cur["gate"])
    finish(n_chunks - 1, *prev)


def _const_spec(shape):
    zeros = (0,) * len(shape)
    return pl.BlockSpec(shape, lambda *_: zeros, pipeline_mode=pl.Buffered(1))


def _params(*semantics):
    return pltpu.CompilerParams(dimension_semantics=semantics, vmem_limit_bytes=VMEM_LIMIT)


def _rope_tables(positions, half):
    inv = ROPE_BASE ** (-jnp.arange(half, dtype=F32) / half)
    ang = positions.astype(F32)[:, None] * inv[None, :]
    return jnp.cos(ang), jnp.sin(ang)


def _mla_rope_tables(positions):
    cos, sin = _rope_tables(positions, MLA_ROPE // 2)
    n = positions.shape[0]
    h = MLA_ROPE // 2
    ones = jnp.ones((n, MLA_NOPE), F32)
    zeros = lambda w: jnp.zeros((n, w), F32)
    c = jnp.concatenate([ones, cos, cos, zeros(LANE - MLA_NOPE - MLA_ROPE)], axis=1)
    s_up = jnp.concatenate([zeros(MLA_NOPE + h), sin, zeros(LANE - MLA_NOPE - MLA_ROPE)], axis=1)
    s_dn = jnp.concatenate([zeros(MLA_NOPE), -sin, zeros(LANE - MLA_NOPE - h)], axis=1)
    return c, s_up, s_dn


def _layer_spec(stacked_shape, layer):
    return pl.BlockSpec((None,) + tuple(stacked_shape[1:]), lambda *_: (layer, 0, 0),
                        pipeline_mode=pl.Buffered(1))


def _mix_mlp_call(name, layer, h, ys, w_out, ln1_g, ln1_b, w1, w2, ln2_g, ln2_b):
    m = h.shape[0]
    tm = ROW_TILE
    row_spec = lambda w: pl.BlockSpec((tm, w), lambda i: (i, 0))
    vec = _const_spec((1, D_MODEL))
    return pl.pallas_call(
        functools.partial(_mix_mlp_kernel, n_mix=len(ys)),
        grid=(m // tm,),
        in_specs=[row_spec(D_MODEL)] + [row_spec(y.shape[1]) for y in ys] + [
            _const_spec(w_out.shape), vec, vec, _layer_spec(w1.shape, layer), _layer_spec(w2.shape, layer),
            vec, vec],
        out_specs=row_spec(D_MODEL),
        out_shape=jax.ShapeDtypeStruct((m, D_MODEL), F32),
        compiler_params=_params("parallel"),
        name=name,
    )(h, *ys, w_out, ln1_g, ln1_b, w1, w2, ln2_g, ln2_b)


def kernel(x, meta_tokens, ev_w_in, ev_conv_w, ev_conv_b, ev_w_rg_a, ev_b_rg_a, ev_w_rg_x, ev_b_rg_x,
           ev_lru_lambda, ev_q_norm_g, ev_w_uq, ev_kv_norm_g, ev_w_ukv, ev_w_out, od_w_in, od_w_out,
           ln_mix_g, ln_mix_b, mlp_w1, mlp_w2, ln_mlp_g, ln_mlp_b):
    bsz, seq, d_model = x.shape
    assert d_model == D_MODEL and meta_tokens.shape == (N_META, D_MODEL)
    assert seq % SEQ_TILE == 0 and seq % RET_CHUNK == 0 and (bsz * seq) % ROW_TILE == 0
    row = lambda v: v.reshape(1, -1).astype(F32)

    w_in0 = ev_w_in[0]
    lat0 = 2 * LRU_WIDTH
    kpe0 = lat0 + MLA_Q_RANK + MLA_KV_RANK
    w_kpe = jnp.zeros((D_MODEL, LANE), F32).at[:, MLA_NOPE:MLA_NOPE + MLA_ROPE].set(w_in0[:, kpe0:])
    w_in_ev = jnp.concatenate([w_in0[:, :kpe0], w_kpe], axis=1).astype(BF16)
    gate_w = jnp.concatenate([ev_w_rg_a[0], ev_w_rg_x[0]], axis=2).astype(BF16)
    w_uq = ev_w_uq[0].reshape(MLA_Q_RANK, MLA_HEADS, MLA_NOPE + MLA_ROPE)
    w_uq = jnp.pad(w_uq, ((0, 0), (0, 0), (0, LANE - MLA_NOPE - MLA_ROPE)))
    w_uq = w_uq.reshape(MLA_Q_RANK, MLA_HEADS * LANE).astype(BF16)
    w_ukv = ev_w_ukv[0].reshape(MLA_KV_RANK, MLA_HEADS, MLA_NOPE + MLA_V)
    w_uk = jnp.pad(w_ukv[:, :, :MLA_NOPE], ((0, 0), (0, 0), (0, LANE - MLA_NOPE)))
    w_uk = w_uk.reshape(MLA_KV_RANK, MLA_HEADS * LANE)
    w_uv = w_ukv[:, :, MLA_NOPE:].reshape(MLA_KV_RANK, MLA_HEADS * MLA_V)
    w_ukv_p = jnp.concatenate([w_uk, w_uv], axis=1).astype(BF16)
    w_uq_t = w_uq.T
    w_uk = w_uk.astype(BF16)
    w_uv_t = w_uv.T.astype(BF16)
    w_out_ev = ev_w_out[0].astype(BF16)
    w_in_od = od_w_in[0]
    w_out_od = od_w_out[0].astype(BF16)
    w1 = mlp_w1.astype(BF16)
    w2 = mlp_w2.astype(BF16)
    conv_w = ev_conv_w[0].astype(F32)
    conv_b, b_a, b_x, lam = row(ev_conv_b[0]), row(ev_b_rg_a[0]), row(ev_b_rg_x[0]), row(ev_lru_lambda[0])
    q_g, kv_g = row(ev_q_norm_g[0]), row(ev_kv_norm_g[0])
    ln = [(row(ln_mix_g[l]), row(ln_mix_b[l]), row(ln_mlp_g[l]), row(ln_mlp_b[l])) for l in range(2)]

    pos = jnp.arange(N_META + seq, dtype=jnp.int32)
    mla_tabs = _mla_rope_tables(pos)
    cos0, sin0 = _rope_tables(pos, MLA_ROPE // 2)
    cos0_t, sin0_t = cos0.T, sin0.T
    cos1, sin1 = _rope_tables(pos, RET_QK_DIM // 2)
    k_scale = RET_QK_DIM ** -0.5
    qk = RET_HEADS * RET_QK_DIM
    log_gamma = jnp.log(1.0 - 2.0 ** (-5.0 - jnp.arange(RET_HEADS, dtype=F32)))
    log_gamma = jnp.broadcast_to(log_gamma[:, None, None], (RET_HEADS, 1, LANE))

    mixw = RET_HEADS * RET_V_DIM
    meta_out_shapes = (jax.ShapeDtypeStruct((SUBLANE, LRU_WIDTH), F32),
                       jax.ShapeDtypeStruct((SUBLANE, LRU_WIDTH), F32),
                       jax.ShapeDtypeStruct((N_META, MLA_HEADS * LANE), BF16),
                       jax.ShapeDtypeStruct((MLA_HEADS * MLA_V, N_META), BF16),
                       jax.ShapeDtypeStruct((RET_HEADS, RET_QK_DIM, RET_V_DIM), F32))
    meta_args = [meta_tokens.astype(F32), w_in_ev, conv_w, conv_b, gate_w, b_a, b_x, lam, q_g, kv_g, w_uq,
                 w_ukv_p, w_uv_t, *[t[:N_META] for t in mla_tabs], w_out_ev, *ln[0][:2]]
    meta_specs = [_const_spec(a.shape) for a in meta_args]
    meta_args += [w1, w2, *ln[0][2:], w_in_od, w_in_od]
    meta_specs += [_layer_spec(w1.shape, 0), _layer_spec(w2.shape, 0), _const_spec(ln[0][2].shape),
                   _const_spec(ln[0][3].shape),
                   pl.BlockSpec((D_MODEL, qk), lambda i: (0, 1), pipeline_mode=pl.Buffered(1)),
                   pl.BlockSpec((D_MODEL, mixw), lambda i: (0, 1), pipeline_mode=pl.Buffered(1))]
    tail_args = [cos1[:N_META] * k_scale, sin1[:N_META] * k_scale, log_gamma]
    meta_args += tail_args
    meta_specs += [_const_spec(a.shape) for a in tail_args]
    meta_out = pl.pallas_call(
        _meta_kernel,
        grid=(1,),
        in_specs=meta_specs,
        out_specs=[pl.BlockSpec(s.shape, lambda i, nd=len(s.shape): (0,) * nd) for s in meta_out_shapes],
        out_shape=meta_out_shapes,
        scratch_shapes=[pltpu.VMEM((SUBLANE + N_META, LRU_WIDTH), F32),
                        pltpu.VMEM((N_META, LRU_WIDTH), F32),
                        pltpu.VMEM((N_META, LRU_WIDTH), F32)],
        compiler_params=_params("arbitrary"),
        name="meta_tokens",
    )(*meta_args)
    rec_tail, h_tail, k_meta, vt_meta, s_meta = meta_out

    ts = SEQ_TILE
    nt = seq // ts
    tab_spec = pl.BlockSpec((ts, LANE), lambda b, t: (t, 0))
    tab_t_spec = pl.BlockSpec((MLA_ROPE // 2, ts), lambda b, t: (0, t))
    seq_spec = lambda w: pl.BlockSpec((1, ts, w), lambda b, t: (b, t, 0))
    seq_t_spec = lambda w: pl.BlockSpec((1, 1, w, ts), lambda b, t: (b, t, 0, 0))
    y_rec, qt0, k0, vt0 = pl.pallas_call(
        _seq0_kernel,
        grid=(bsz, nt),
        in_specs=[seq_spec(D_MODEL), _const_spec(w_in_ev.shape), _const_spec(conv_w.shape),
                  _const_spec(conv_b.shape), _const_spec(gate_w.shape), _const_spec(b_a.shape),
                  _const_spec(b_x.shape), _const_spec(lam.shape), _const_spec(q_g.shape),
                  _const_spec(kv_g.shape), _const_spec(w_uq_t.shape), _const_spec(w_uk.shape),
                  _const_spec(w_uv_t.shape), tab_spec, tab_spec, tab_spec, tab_t_spec, tab_t_spec,
                  _const_spec(rec_tail.shape), _const_spec(h_tail.shape)],
        out_specs=[seq_spec(LRU_WIDTH), seq_t_spec(MLA_HEADS * LANE), seq_spec(MLA_HEADS * LANE),
                   seq_t_spec(MLA_HEADS * MLA_V)],
        out_shape=(jax.ShapeDtypeStruct((bsz, seq, LRU_WIDTH), BF16),
                   jax.ShapeDtypeStruct((bsz, nt, MLA_HEADS * LANE, ts), BF16),
                   jax.ShapeDtypeStruct((bsz, seq, MLA_HEADS * LANE), BF16),
                   jax.ShapeDtypeStruct((bsz, nt, MLA_HEADS * MLA_V, ts), BF16)),
        scratch_shapes=[pltpu.VMEM((SUBLANE + ts // SEQ_SPLIT, LRU_WIDTH), F32),
                        pltpu.VMEM((LRU_WIDTH // LANE, ts, LANE), F32),
                        pltpu.VMEM((LRU_WIDTH // LANE, ts, LANE), F32),
                        pltpu.VMEM((ts, LRU_WIDTH), F32),
                        pltpu.VMEM((SUBLANE, LRU_WIDTH), F32)],
        compiler_params=_params("parallel", "arbitrary"),
        name="seq0_mixer_proj",
    )(x, w_in_ev, conv_w, conv_b, gate_w, b_a, b_x, lam, q_g, kv_g, w_uq_t, w_uk, w_uv_t,
      *[t[N_META:] for t in mla_tabs], cos0_t[:, N_META:], sin0_t[:, N_META:], rec_tail, h_tail)

    tq = ATT_TILE
    assert tq == ts
    y_att = pl.pallas_call(
        _attn_kernel,
        grid=(bsz, MLA_HEADS // 2),
        in_specs=[pl.BlockSpec((1, nt, 2 * LANE, tq), lambda b, j: (b, 0, j, 0)),
                  pl.BlockSpec((1, seq, 2 * LANE), lambda b, j: (b, 0, j)),
                  pl.BlockSpec((1, nt, LANE, tq), lambda b, j: (b, 0, j, 0)),
                  pl.BlockSpec((N_META, 2 * LANE), lambda b, j: (0, j)),
                  pl.BlockSpec((LANE, N_META), lambda b, j: (j, 0))],
        out_specs=pl.BlockSpec((1, seq, LANE), lambda b, j: (b, 0, j)),
        out_shape=jax.ShapeDtypeStruct((bsz, seq, MLA_HEADS * MLA_V), BF16),
        scratch_shapes=[pltpu.VMEM((nt, 2, 1, tq), F32), pltpu.VMEM((nt, 2, MLA_V + ATT_ONES, tq), F32)],
        compiler_params=_params("parallel", "parallel"),
        name="mla_attention",
    )(qt0, k0, vt0, k_meta, vt_meta)

    m = bsz * seq
    h1 = _mix_mlp_call("layer0_out_mlp", 0, x.reshape(m, D_MODEL),
                       [y_rec.reshape(m, -1), y_att.reshape(m, -1)], w_out_ev,
                       ln[0][0], ln[0][1], w1, w2, ln[0][2], ln[0][3])

    mixw = RET_HEADS * RET_V_DIM
    col_spec = lambda width, first: pl.BlockSpec((D_MODEL, width), lambda b, h: (0, first + h))
    y_ret = pl.pallas_call(
        _ret_fused_kernel,
        grid=(bsz, RET_HEADS),
        in_specs=[pl.BlockSpec((1, seq, D_MODEL), lambda b, h: (b, 0, 0)),
                  col_spec(RET_QK_DIM, 0), col_spec(RET_QK_DIM, RET_HEADS),
                  col_spec(RET_V_DIM, RET_HEADS), col_spec(RET_V_DIM, 2 * RET_HEADS),
                  _const_spec((seq, RET_QK_DIM // 2)), _const_spec((seq, RET_QK_DIM // 2)),
                  pl.BlockSpec((1, RET_QK_DIM, RET_V_DIM), lambda b, h: (h, 0, 0)),
                  pl.BlockSpec((1, 1, LANE), lambda b, h: (h, 0, 0))],
        out_specs=pl.BlockSpec((1, seq, RET_V_DIM), lambda b, h: (b, 0, h)),
        out_shape=jax.ShapeDtypeStruct((bsz, seq, mixw), BF16),
        scratch_shapes=[pltpu.VMEM((RET_QK_DIM, RET_V_DIM), F32)],
        compiler_params=_params("parallel", "arbitrary"),
        name="layer1_retention",
    )(h1.reshape(bsz, seq, D_MODEL), w_in_od, w_in_od, w_in_od, w_in_od, cos1[N_META:], sin1[N_META:],
      s_meta, log_gamma)

    out = _mix_mlp_call("layer1_out_mlp", 1, h1, [y_ret.reshape(m, mixw)], w_out_od,
                        ln[1][0], ln[1][1], w1, w2, ln[1][2], ln[1][3])
    return out.reshape(bsz, seq, D_MODEL)
```
